```python
import jax, jax.numpy as jnp
from jax import lax
import numpy as np

D_MODEL = 1024
BATCH = 8
SEQ = 2048
DEPTH = 2

CHUNK = 64
CONV_WIDTH = 3
D_CONV = D_MODEL
HEAD_DIM = 64
N_HEADS = D_MODEL // HEAD_DIM
D_RWKV = N_HEADS * HEAD_DIM
DECAY_RANK = 64
AAA_RANK = 64
VMIX_RANK = 32
GATE_RANK = 128
N_EXPERTS = 32
N_GROUPS = 8
EXPERTS_PER_GROUP = N_EXPERTS // N_GROUPS
TOP_K = 2
D_EXPERT = 512
M_BLK = 128
NORM_EPS = 1e-6
LN_X_EPS = 64e-5

N_PLAIN = 3 * D_CONV + 2 * D_MODEL
N_SHIFT = 3 * D_RWKV + DECAY_RANK + AAA_RANK + GATE_RANK
N_IN = N_PLAIN + N_SHIFT

kernel_name = 'hybrid_conv_rwkv7_grouped_moe_trunk'


def rms_norm(x, gain):
    x32 = x.astype(jnp.float32)
    y = x32 * lax.rsqrt(jnp.mean(x32 * x32, axis=-1, keepdims=True) + NORM_EPS)
    return (y * gain.astype(jnp.float32)).astype(x.dtype)


def token_shift(p):
    return jnp.pad(p, ((0, 0), (1, 0), (0, 0)))[:, :-1]


def causal_dwconv(u, w):
    s = u.shape[1]
    up = jnp.pad(u, ((0, 0), (CONV_WIDTH - 1, 0), (0, 0)))
    return sum(up[:, j:j + s] * w[j] for j in range(CONV_WIDTH))


def wkv7(r, decay, k, v, kk, a):
    bn, s = r.shape[0], r.shape[1]
    nc = s // CHUNK

    def to_chunks(t):
        return t.reshape(bn, nc, CHUNK, N_HEADS, HEAD_DIM).transpose(1, 2, 0, 3, 4)

    xs = (to_chunks(r), to_chunks(decay), to_chunks(k), to_chunks(v), to_chunks(kk), to_chunks(a))

    def step(state, inp):
        r_t, w_t, k_t, v_t, kk_t, a_t = inp
        sa = jnp.einsum('bhvk,bhk->bhv', state, kk_t)
        state = (state * w_t[:, :, None, :]
                 - sa[..., None] * (kk_t * a_t)[:, :, None, :]
                 + v_t[..., None] * k_t[:, :, None, :])
        y = jnp.einsum('bhvk,bhk->bhv', state, r_t)
        return state, y

    def chunk_step(state, chunk_inp):
        return lax.scan(step, state, chunk_inp)

    s0 = jnp.zeros((bn, N_HEADS, HEAD_DIM, HEAD_DIM), jnp.float32)
    _, y = lax.scan(chunk_step, s0, xs)
    return y.transpose(2, 0, 1, 3, 4).reshape(bn, s, N_HEADS, HEAD_DIM)


def heads(t):
    return t.reshape(t.shape[0], t.shape[1], N_HEADS, HEAD_DIM)


def token_mixer(h, v_first, w_in, token_mu, conv_w, w0, decay_up, a0, aaa_up, gate_up,
                k_k, k_a, r_k, ln_x_w, ln_x_b, vmix, w_out):
    bn, s, _ = h.shape
    p = h @ w_in
    plain, rw = p[..., :N_PLAIN], p[..., N_PLAIN:]
    rw = rw + (token_shift(rw) - rw) * token_mu

    cb, cc, ch, ga, gb = jnp.split(plain, [D_CONV, 2 * D_CONV, 3 * D_CONV, 3 * D_CONV + D_MODEL], axis=-1)
    y_conv = cb * causal_dwconv(cc * ch, conv_w)

    r, k, v, wd, ad, gd = jnp.split(
        rw, [D_RWKV, 2 * D_RWKV, 3 * D_RWKV, 3 * D_RWKV + DECAY_RANK, 3 * D_RWKV + DECAY_RANK + AAA_RANK], axis=-1)
    w_log = -jax.nn.softplus(-(w0 + jnp.tanh(wd) @ decay_up).astype(jnp.float32)) - 0.5
    decay = jnp.exp(-jnp.exp(w_log))
    a = jax.nn.sigmoid(a0 + ad @ aaa_up)
    g = jax.nn.sigmoid(gd) @ gate_up
    if vmix is None:
        v_first = v
    else:
        vmix_down, vmix_up, v0 = vmix
        v = v + (v_first - v) * jax.nn.sigmoid(v0 + (h @ vmix_down) @ vmix_up)
    kk = heads(k * k_k).astype(jnp.float32)
    kk = kk / jnp.maximum(jnp.sqrt(jnp.sum(kk * kk, axis=-1, keepdims=True)), 1e-12)
    k = k * (1.0 + (a - 1.0) * k_a)
    r_h = heads(r).astype(jnp.float32)
    k_h = heads(k).astype(jnp.float32)
    v_h = heads(v).astype(jnp.float32)
    a_h = heads(a).astype(jnp.float32)
    y = wkv7(r_h, heads(decay), k_h, v_h, kk, a_h)
    mu = jnp.mean(y, axis=-1, keepdims=True)
    var = jnp.mean(jnp.square(y - mu), axis=-1, keepdims=True)
    yn = ((y - mu) * lax.rsqrt(var + LN_X_EPS)).reshape(bn, s, D_RWKV)
    yn = yn * ln_x_w.astype(jnp.float32) + ln_x_b.astype(jnp.float32)
    bonus = jnp.sum(r_h * k_h * r_k.astype(jnp.float32), axis=-1, keepdims=True) * v_h
    y_rwkv = ((yn + bonus.reshape(bn, s, D_RWKV)) * g.astype(jnp.float32)).astype(h.dtype)

    merged = jax.nn.sigmoid(ga) * y_conv + jax.nn.sigmoid(gb) * y_rwkv
    return merged @ w_out, v_first


def moe(h, w_router, router_bias, w_e_in, w_e_out):
    bn, s, d = h.shape
    hf = h.reshape(bn * s, d)
    t = hf.shape[0]
    n_assign = t * TOP_K
    aff = jax.nn.sigmoid((hf @ w_router).astype(jnp.float32))
    sel = (aff + router_bias.astype(jnp.float32)).reshape(t, N_GROUPS, EXPERTS_PER_GROUP)
    gscore = jnp.sum(lax.top_k(sel, TOP_K)[0], axis=-1)
    gsel = jnp.argmax(gscore, axis=-1).astype(jnp.int32)
    cand = jnp.take_along_axis(sel, gsel[:, None, None], axis=1)[:, 0]
    _, local = lax.top_k(cand, TOP_K)
    eidx = gsel[:, None] * EXPERTS_PER_GROUP + local
    wts = jnp.take_along_axis(aff, eidx, axis=-1)
    wts = wts / jnp.sum(wts, axis=-1, keepdims=True)

    flat_e = eidx.reshape(n_assign)
    order = jnp.argsort(flat_e)
    sorted_e = flat_e[order]
    tok = order // TOP_K
    sizes = jnp.bincount(flat_e, length=N_EXPERTS)
    padded = (sizes + M_BLK - 1) // M_BLK * M_BLK
    pad_end = jnp.cumsum(padded)
    pad_start = pad_end - padded
    start = jnp.cumsum(sizes) - sizes
    dest = pad_start[sorted_e] + jnp.arange(n_assign) - start[sorted_e]
    n_rows = n_assign + N_EXPERTS * M_BLK
    n_blocks = n_rows // M_BLK
    buf = jnp.zeros((n_rows, d), h.dtype).at[dest].set(hf[tok])
    blk_e = jnp.minimum(jnp.searchsorted(pad_end, jnp.arange(n_blocks) * M_BLK, side='right'), N_EXPERTS - 1)

    def expert_block(args):
        xb, e = args
        gate, up = jnp.split(xb @ w_e_in[e], 2, axis=-1)
        return (jax.nn.silu(gate) * up) @ w_e_out[e]

    yb = lax.map(expert_block, (buf.reshape(n_blocks, M_BLK, d), blk_e))
    ye = yb.reshape(n_rows, d)[dest] * wts.reshape(n_assign)[order][:, None].astype(h.dtype)
    return jnp.zeros_like(hf).at[tok].add(ye).reshape(bn, s, d)


def setup_inputs(seed: int = 0) -> dict:
    key = jax.random.key(seed)
    ks = jax.random.split(key, 32)
    f32 = jnp.float32
    nrm = lambda k, shape, scale: jax.random.normal(k, shape, f32) * scale
    inv = lambda n: float(n) ** -0.5
    return {
        'x': nrm(ks[0], (BATCH, SEQ, D_MODEL), 1.0),
        'c': nrm(ks[1], (BATCH, D_MODEL), 1.0),
        'w_ada': nrm(ks[2], (DEPTH, D_MODEL, 6 * D_MODEL), 0.5 * inv(D_MODEL)),
        'b_ada': nrm(ks[3], (DEPTH, 6 * D_MODEL), 0.01),
        'norm1_gain': 1.0 + nrm(ks[4], (DEPTH, D_MODEL), 0.05),
        'norm2_gain': 1.0 + nrm(ks[5], (DEPTH, D_MODEL), 0.05),
        'w_in': nrm(ks[6], (DEPTH, D_MODEL, N_IN), inv(D_MODEL)),
        'token_mu': jax.random.uniform(ks[7], (DEPTH, N_SHIFT), f32),
        'conv_w': nrm(ks[8], (DEPTH, CONV_WIDTH, D_CONV), inv(CONV_WIDTH)),
        'w0': jnp.linspace(-6.0, -1.0, D_RWKV, dtype=f32)[None, :] + nrm(ks[9], (DEPTH, D_RWKV), 0.1),
        'decay_up': nrm(ks[10], (DEPTH, DECAY_RANK, D_RWKV), 0.1 * inv(DECAY_RANK)),
        'a0': nrm(ks[11], (DEPTH, D_RWKV), 0.1),
        'aaa_up': nrm(ks[12], (DEPTH, AAA_RANK, D_RWKV), 0.5 * inv(AAA_RANK)),
        'gate_up': nrm(ks[13], (DEPTH, GATE_RANK, D_RWKV), inv(GATE_RANK)),
        'k_k': 0.85 + nrm(ks[14], (DEPTH, D_RWKV), 0.05),
        'k_a': 1.0 + nrm(ks[15], (DEPTH, D_RWKV), 0.05),
        'r_k': nrm(ks[16], (DEPTH, N_HEADS, HEAD_DIM), 0.1),
        'ln_x_w': 1.0 + nrm(ks[17], (DEPTH, D_RWKV), 0.05),
        'ln_x_b': nrm(ks[18], (DEPTH, D_RWKV), 0.01),
        'vmix_down': nrm(ks[19], (DEPTH - 1, D_MODEL, VMIX_RANK), inv(D_MODEL)),
        'vmix_up': nrm(ks[20], (DEPTH - 1, VMIX_RANK, D_RWKV), inv(VMIX_RANK)),
        'v0': nrm(ks[21], (DEPTH - 1, D_RWKV), 0.1),
        'w_out': nrm(ks[22], (DEPTH, D_MODEL, D_MODEL), inv(D_MODEL)),
        'w_router': nrm(ks[23], (D_MODEL, N_EXPERTS), inv(D_MODEL)),
        'router_bias': nrm(ks[24], (N_EXPERTS,), 0.01),
        'w_expert_in': nrm(ks[25], (DEPTH, N_EXPERTS, D_MODEL, 2 * D_EXPERT), inv(D_MODEL)),
        'w_expert_out': nrm(ks[26], (DEPTH, N_EXPERTS, D_EXPERT, D_MODEL), inv(D_EXPERT)),
        'final_gain': 1.0 + nrm(ks[27], (D_MODEL,), 0.05),
    }


def reference(x, c, w_ada, b_ada, norm1_gain, norm2_gain, w_in, token_mu, conv_w, w0,
              decay_up, a0, aaa_up, gate_up, k_k, k_a, r_k, ln_x_w, ln_x_b,
              vmix_down, vmix_up, v0, w_out, w_router, router_bias,
              w_expert_in, w_expert_out, final_gain):
    v_first = None
    cond = jax.nn.silu(c)
    for l in range(DEPTH):
        ada = (cond @ w_ada[l] + b_ada[l])[:, None, :]
        sh1, sc1, g1, sh2, sc2, g2 = jnp.split(ada, 6, axis=-1)
        h = rms_norm(x, norm1_gain[l]) * (1.0 + sc1) + sh1
        vmix = None if l == 0 else (vmix_down[l - 1], vmix_up[l - 1], v0[l - 1])
        mix, v_first = token_mixer(h, v_first, w_in[l], token_mu[l], conv_w[l], w0[l], decay_up[l],
                                   a0[l], aaa_up[l], gate_up[l], k_k[l], k_a[l], r_k[l],
                                   ln_x_w[l], ln_x_b[l], vmix, w_out[l])
        x = x + g1 * mix
        h2 = rms_norm(x, norm2_gain[l]) * (1.0 + sc2) + sh2
        x = x + g2 * moe(h2, w_router, router_bias, w_expert_in[l], w_expert_out[l])
    return rms_norm(x, final_gain)
```

```python
import functools

import jax
import jax.numpy as jnp
from jax import lax
from jax.experimental import pallas as pl
from jax.experimental.pallas import tpu as pltpu

F32 = jnp.float32
BF16 = jnp.bfloat16
I32 = jnp.int32
HIGHEST = lax.Precision.HIGHEST

HEAD_DIM = 64
CONV_WIDTH = 3
DECAY_RANK = 64
AAA_RANK = 64
GATE_RANK = 128
LOW_RANK = DECAY_RANK + AAA_RANK + GATE_RANK
VMIX_RANK = 32
N_EXPERTS = 32
N_GROUPS = 8
EXPERTS_PER_GROUP = N_EXPERTS // N_GROUPS
TOP_K = 2
M_BLK = 128
NORM_EPS = 1e-6
LN_X_EPS = 64e-5

LANES = 128
SUBLANES = 8
HEADS_PER_VREG = LANES // HEAD_DIM
VMEM_LIMIT_BYTES = 56 * 1024 * 1024

WKV_CHUNK = 64
WKV_SEQ_BLOCK = 512
INPROJ_ROWS = 1024
INPROJ_COLS = 1408
PREP_ROWS = 128
OUTPROJ_ROWS = 256
RANK_COLS = 512
SCATTER_ROWS = 256
COMBINE_ROWS = 256


def _dot_f32(a, b, dims=(((1,), (0,)), ((), ()))):
  return lax.dot_general(a, b, dims, precision=HIGHEST, preferred_element_type=F32)


def _dot_bf16(a, b, dims=(((1,), (0,)), ((), ()))):
  return lax.dot_general(a.astype(BF16), b.astype(BF16), dims,
                         preferred_element_type=F32)


_NT = (((1,), (1,)), ((), ()))
_TN = (((0,), (0,)), ((), ()))


def _sigmoid(x):
  return 1.0 / (1.0 + jnp.exp(-x))


def _params(semantics):
  return pltpu.CompilerParams(dimension_semantics=semantics,
                              vmem_limit_bytes=VMEM_LIMIT_BYTES)


def _ada_kernel(c_ref, w_ref, b_ref, o_ref):
  c = c_ref[...]
  cond = c * _sigmoid(c)
  o_ref[0] = _dot_f32(cond, w_ref[0]) + b_ref[0]


def _ada(c, w_ada, b_ada):
  depth, d, n = w_ada.shape
  bsz = c.shape[0]
  tn = d
  return pl.pallas_call(
      _ada_kernel,
      grid=(depth, n // tn),
      in_specs=[
          pl.BlockSpec((bsz, d), lambda l, j: (0, 0)),
          pl.BlockSpec((1, d, tn), lambda l, j: (l, 0, j)),
          pl.BlockSpec((1, 1, tn), lambda l, j: (l, 0, j)),
      ],
      out_specs=pl.BlockSpec((1, bsz, tn), lambda l, j: (l, 0, j)),
      out_shape=jax.ShapeDtypeStruct((depth, bsz, n), F32),
      compiler_params=_params(("arbitrary", "arbitrary")),
      name="ada",
  )(c, w_ada, b_ada.reshape(depth, 1, n))


def _modulated_norm(x, gain, scale, shift):
  y = x * lax.rsqrt(jnp.mean(x * x, axis=-1, keepdims=True) + NORM_EPS)
  return (y * gain) * (1.0 + scale) + shift


def _inproj_kernel(x_ref, gain_ref, sc_ref, sh_ref, w_ref, *rest, has_vmix):
  if has_vmix:
    wv_ref, p_ref, hv_ref, h_scr = rest
  else:
    p_ref, h_scr = rest

  @pl.when(pl.program_id(2) == 0)
  def _():
    h = _modulated_norm(x_ref[0], gain_ref[...], sc_ref[0], sh_ref[0])
    hb = h.astype(BF16)
    h_scr[...] = hb
    if has_vmix:
      hv_ref[0] = jnp.dot(hb, wv_ref[...], preferred_element_type=F32)

  p_ref[0] = jnp.dot(h_scr[...], w_ref[...], preferred_element_type=F32)


def _inproj(x, gain, sc, sh, w_bf16, vmix_down_bf16):
  bsz, s, d = x.shape
  n = w_bf16.shape[1]
  tm = min(INPROJ_ROWS, s)
  tn = INPROJ_COLS
  has_vmix = vmix_down_bf16 is not None
  in_specs = [
      pl.BlockSpec((1, tm, d), lambda b, i, j: (b, i, 0)),
      pl.BlockSpec((1, d), lambda b, i, j: (0, 0)),
      pl.BlockSpec((1, 1, d), lambda b, i, j: (b, 0, 0)),
      pl.BlockSpec((1, 1, d), lambda b, i, j: (b, 0, 0)),
      pl.BlockSpec((d, tn), lambda b, i, j: (0, j)),
  ]
  args = [x, gain.reshape(1, d), sc, sh, w_bf16]
  out_specs = [pl.BlockSpec((1, tm, tn), lambda b, i, j: (b, i, j))]
  out_shape = [jax.ShapeDtypeStruct((bsz, s, n), F32)]
  if has_vmix:
    in_specs.append(pl.BlockSpec((d, LANES), lambda b, i, j: (0, 0)))
    args.append(vmix_down_bf16)
    out_specs.append(pl.BlockSpec((1, tm, LANES), lambda b, i, j: (b, i, 0)))
    out_shape.append(jax.ShapeDtypeStruct((bsz, s, LANES), F32))
  outs = pl.pallas_call(
      functools.partial(_inproj_kernel, has_vmix=has_vmix),
      grid=(bsz, s // tm, n // tn),
      in_specs=in_specs,
      out_specs=out_specs,
      out_shape=out_shape,
      scratch_shapes=[pltpu.VMEM((tm, d), BF16)],
      compiler_params=_params(("arbitrary", "arbitrary", "arbitrary")),
      name="inproj",
  )(*args)
  return outs if has_vmix else (outs[0], None)


def _head_sum(x, bd_ones):
  parts = [
      _dot_f32(x[:, j * LANES:(j + 1) * LANES], bd_ones)
      for j in range(x.shape[1] // LANES)
  ]
  return parts[0] if len(parts) == 1 else jnp.concatenate(parts, axis=1)


def _shift_rows(cur, prev_tail, n):
  rolled = pltpu.roll(cur, n, axis=0)
  row = lax.broadcasted_iota(I32, cur.shape, 0)
  out = rolled
  for j in range(n):
    out = jnp.where(row == j, prev_tail[SUBLANES - n + j:SUBLANES - n + j + 1, :], out)
  return out


def _prep_kernel(cb_ref, cc_ref, ch_ref, ga_ref, gb_ref, r_ref, k_ref, v_ref, lr_ref,
                 mu_ref, mulr_ref, convw_ref, vec_ref, dup_ref, aup_ref, gup_ref, bd_ref,
                 *rest, has_vmix):
  if has_vmix:
    hv_ref, vfirst_ref, vup_ref, v0_ref = rest[:4]
    rest = rest[4:]
  (ro_ref, ko_ref, vo_ref, lw_ref, kk_ref, beta_ref, g_ref, mconv_ref, sgb_ref,
   pr_scr, pk_scr, pv_scr, plr_scr, pu_scr) = rest

  @pl.when(pl.program_id(1) == 0)
  def _():
    pr_scr[...] = jnp.zeros_like(pr_scr)
    pk_scr[...] = jnp.zeros_like(pk_scr)
    pv_scr[...] = jnp.zeros_like(pv_scr)
    plr_scr[...] = jnp.zeros_like(plr_scr)
    pu_scr[...] = jnp.zeros_like(pu_scr)

  rows = r_ref.shape[1]
  tail = slice(rows - SUBLANES, rows)

  def token_shift(ref, scr, mu):
    cur = ref[0]
    prev = _shift_rows(cur, scr[...], 1)
    scr[...] = cur[tail, :]
    return cur + (prev - cur) * mu

  r = token_shift(r_ref, pr_scr, mu_ref[0:1, :])
  k = token_shift(k_ref, pk_scr, mu_ref[1:2, :])
  v = token_shift(v_ref, pv_scr, mu_ref[2:3, :])
  lr = token_shift(lr_ref, plr_scr, mulr_ref[...])

  w0 = vec_ref[0:1, :]
  a0 = vec_ref[1:2, :]
  k_k = vec_ref[2:3, :]
  k_a = vec_ref[3:4, :]

  w_pre = w0 + _dot_f32(jnp.tanh(lr), dup_ref[...])
  z = -w_pre
  softplus = jnp.maximum(z, 0.0) + jnp.log(1.0 + jnp.exp(-jnp.abs(z)))
  lw_ref[0] = -jnp.exp(-softplus - 0.5)
  a = _sigmoid(a0 + _dot_f32(lr, aup_ref[...]))
  g_ref[0] = _dot_f32(_sigmoid(lr), gup_ref[...])
  if has_vmix:
    mix = _sigmoid(v0_ref[...] + _dot_f32(hv_ref[0], vup_ref[...]))
    v = v + (vfirst_ref[0] - v) * mix
  kk = k * k_k
  norm = jnp.sqrt(_head_sum(kk * kk, bd_ref[...]))
  kk = kk / jnp.maximum(norm, 1e-12)
  ro_ref[0] = r
  ko_ref[0] = k * (1.0 + (a - 1.0) * k_a)
  vo_ref[0] = v
  kk_ref[0] = kk
  beta_ref[0] = a * kk

  u = cc_ref[0] * ch_ref[0]
  prev_u = pu_scr[...]
  conv = (convw_ref[0:1, :] * _shift_rows(u, prev_u, 2)
          + convw_ref[1:2, :] * _shift_rows(u, prev_u, 1)
          + convw_ref[2:3, :] * u)
  pu_scr[...] = u[tail, :]
  mconv_ref[0] = _sigmoid(ga_ref[0]) * (cb_ref[0] * conv)
  sgb_ref[0] = _sigmoid(gb_ref[0])


def _prep(p, mu3, mu_lr, conv_w, vecs, decay_up_pad, aaa_up_pad, gate_up_pad, bd_ones,
          hv, v_first, vmix_up_pad, v0):
  bsz, s, _ = p.shape
  d = conv_w.shape[1]
  ts = min(PREP_ROWS, s)
  has_vmix = hv is not None
  col = lambda c: pl.BlockSpec((1, ts, d), lambda b, i, c=c: (b, i, c))
  full = lambda a: pl.BlockSpec(a.shape, lambda b, i: (0,) * a.ndim)
  in_specs = [col(c) for c in range(8)]
  in_specs.append(pl.BlockSpec((1, ts, LOW_RANK), lambda b, i: (b, i, 8 * d // LOW_RANK)))
  consts = [mu3, mu_lr, conv_w, vecs, decay_up_pad, aaa_up_pad, gate_up_pad, bd_ones]
  in_specs += [full(a) for a in consts]
  args = [p] * 9 + consts
  if has_vmix:
    in_specs += [pl.BlockSpec((1, ts, LANES), lambda b, i: (b, i, 0)),
                 pl.BlockSpec((1, ts, d), lambda b, i: (b, i, 0)),
                 full(vmix_up_pad), full(v0)]
    args += [hv, v_first, vmix_up_pad, v0]
  out_spec = pl.BlockSpec((1, ts, d), lambda b, i: (b, i, 0))
  out_sds = jax.ShapeDtypeStruct((bsz, s, d), F32)
  return pl.pallas_call(
      functools.partial(_prep_kernel, has_vmix=has_vmix),
      grid=(bsz, s // ts),
      in_specs=in_specs,
      out_specs=[out_spec] * 9,
      out_shape=[out_sds] * 9,
      scratch_shapes=[pltpu.VMEM((SUBLANES, d), F32)] * 3
      + [pltpu.VMEM((SUBLANES, LOW_RANK), F32), pltpu.VMEM((SUBLANES, d), F32)],
      compiler_params=_params(("arbitrary", "arbitrary")),
      name="prep",
  )(*args)


def _wkv_kernel(r_ref, k_ref, v_ref, lw_ref, kk_ref, beta_ref, g_ref, mconv_ref, sgb_ref,
                tri_ref, bd_ref, lnw_ref, lnb_ref, rk_ref, o_ref, s_scr):
  c = WKV_CHUNK
  n_chunks = r_ref.shape[1] // c

  @pl.when(pl.program_id(2) == 0)
  def _():
    s_scr[...] = jnp.zeros_like(s_scr)

  lane = lax.broadcasted_iota(I32, (c, LANES), 1)
  head_masks = [(lane // HEAD_DIM == h).astype(F32) for h in range(HEADS_PER_VREG)]
  n_stack = HEADS_PER_VREG * c
  t_row = lax.broadcasted_iota(I32, (n_stack, n_stack), 0) % c
  t_col = lax.broadcasted_iota(I32, (n_stack, n_stack), 1) % c
  strict = t_row > t_col
  incl = t_row >= t_col
  eye = (lax.broadcasted_iota(I32, (n_stack, n_stack), 0)
         == lax.broadcasted_iota(I32, (n_stack, n_stack), 1)).astype(F32)

  def stack(x):
    return jnp.concatenate([x * m for m in head_masks], axis=0)

  def unstack(x):
    out = x[0:c]
    for h in range(1, HEADS_PER_VREG):
      out = out + x[h * c:(h + 1) * c]
    return out

  tri = tri_ref[...]
  bd_ones = bd_ref[...]

  def chunk_body(ci, carry):
    sl = pl.ds(pl.multiple_of(ci * c, c), c)
    r = r_ref[0, sl, :]
    k = k_ref[0, sl, :]
    v = v_ref[0, sl, :]
    lw = lw_ref[0, sl, :]
    kk = kk_ref[0, sl, :]
    beta = beta_ref[0, sl, :]

    cum = _dot_f32(tri, lw)
    cum_last = cum[c - 1:c, :]
    inv_p = jnp.exp(-cum)
    rem_p = jnp.exp(cum_last - cum)
    r_s = stack(r * jnp.exp(cum))
    a_s = stack(kk * jnp.exp(cum - lw))
    kh_s = stack(k * inv_p)
    bh_s = stack(beta * inv_p)
    kb_s = stack(k * rem_p)
    bb_s = stack(beta * rem_p)
    v_s = stack(v)

    sc = _dot_f32(jnp.concatenate([a_s, r_s], axis=0),
                  jnp.concatenate([kh_s, bh_s], axis=0), _NT)
    m_k = jnp.where(strict, sc[:n_stack, :n_stack], 0.0)
    m_b = jnp.where(strict, sc[:n_stack, n_stack:], 0.0)
    q_k = jnp.where(incl, sc[n_stack:, :n_stack], 0.0)
    q_b = jnp.where(incl, sc[n_stack:, n_stack:], 0.0)

    w_inv = eye - m_b
    power = m_b
    for _ in range(c.bit_length() - 2):
      power = _dot_f32(power, power)
      w_inv = w_inv + _dot_f32(w_inv, power)

    kv = _dot_f32(jnp.concatenate([m_k, q_k], axis=0), v_s)
    wa = _dot_f32(w_inv, jnp.concatenate([a_s, kv[:n_stack]], axis=1))
    a2_s = wa[:, :LANES]
    v2_s = wa[:, LANES:]

    state = s_scr[...]
    res = _dot_f32(jnp.concatenate([a2_s, r_s], axis=0), state, _NT)
    u_s = res[:n_stack] + v2_s
    y_s = res[n_stack:] + kv[n_stack:] - _dot_f32(q_b, u_s)
    s_scr[...] = state * jnp.exp(cum_last) + _dot_f32(
        jnp.concatenate([v_s, u_s], axis=0),
        jnp.concatenate([kb_s, -bb_s], axis=0), _TN)
    y = unstack(y_s)

    inv_n = 1.0 / HEAD_DIM
    sums = _head_sum(jnp.concatenate([y, r * k * rk_ref[...]], axis=0), bd_ones)
    yc = y - sums[:c] * inv_n
    var = _head_sum(yc * yc, bd_ones) * inv_n
    yn = yc * lax.rsqrt(var + LN_X_EPS) * lnw_ref[...] + lnb_ref[...]
    y_rwkv = (yn + sums[c:] * v) * g_ref[0, sl, :]
    o_ref[0, sl, :] = mconv_ref[0, sl, :] + sgb_ref[0, sl, :] * y_rwkv
    return carry

  lax.fori_loop(0, n_chunks, chunk_body, 0)


def _wkv(r, k, v, lw, kk, beta, g, mconv, sgb, tri, bd_ones, ln_w, ln_b, r_k):
  bsz, s, d = r.shape
  sb = min(WKV_SEQ_BLOCK, s)
  seq = pl.BlockSpec((1, sb, LANES), lambda b, hp, i: (b, i, hp))
  vec = pl.BlockSpec((1, LANES), lambda b, hp, i: (0, hp))
  full = lambda a: pl.BlockSpec(a.shape, lambda b, hp, i: (0,) * a.ndim)
  return pl.pallas_call(
      _wkv_kernel,
      grid=(bsz, d // LANES, s // sb),
      in_specs=[seq] * 9 + [full(tri), full(bd_ones), vec, vec, vec],
      out_specs=seq,
      out_shape=jax.ShapeDtypeStruct((bsz, s, d), F32),
      scratch_shapes=[pltpu.VMEM((LANES, LANES), F32)],
      compiler_params=_params(("arbitrary", "arbitrary", "arbitrary")),
      name="wkv",
  )(r, k, v, lw, kk, beta, g, mconv, sgb, tri, bd_ones,
    ln_w.reshape(1, d), ln_b.reshape(1, d), r_k.reshape(1, d))


def _outproj_kernel(m_ref, x_ref, w_ref, g1_ref, gain_ref, sc_ref, sh_ref, wr_ref, rb_ref,
                    xo_ref, h2_ref, eid_ref, wrow_ref):
  mix = jnp.dot(m_ref[0].astype(BF16), w_ref[...], preferred_element_type=F32)
  x = x_ref[0] + g1_ref[0] * mix
  xo_ref[0] = x
  h2 = _modulated_norm(x, gain_ref[...], sc_ref[0], sh_ref[0])
  h2_ref[0] = h2

  logits = _dot_f32(wr_ref[...], h2, _NT)
  tm = logits.shape[1]
  epg, ng = EXPERTS_PER_GROUP, N_GROUPS
  aff = [_sigmoid(logits[j * ng:(j + 1) * ng, :]) for j in range(epg)]
  sel = [aff[j] + rb_ref[j * ng:(j + 1) * ng, :] for j in range(epg)]
  hi01, lo01 = jnp.maximum(sel[0], sel[1]), jnp.minimum(sel[0], sel[1])
  hi23, lo23 = jnp.maximum(sel[2], sel[3]), jnp.minimum(sel[2], sel[3])
  gscore = jnp.maximum(hi01, hi23) + jnp.maximum(jnp.minimum(hi01, hi23),
                                                 jnp.maximum(lo01, lo23))
  gid = lax.broadcasted_iota(I32, (ng, tm), 0)
  gmax = jnp.max(gscore, axis=0, keepdims=True)
  gsel = jnp.min(jnp.where(gscore == gmax, gid, ng), axis=0, keepdims=True)
  in_g = gid == gsel
  cand = [jnp.sum(jnp.where(in_g, sel[j], 0.0), axis=0, keepdims=True) for j in range(epg)]
  affc = [jnp.sum(jnp.where(in_g, aff[j], 0.0), axis=0, keepdims=True) for j in range(epg)]

  def first_argmax(vals):
    best = vals[0]
    for val in vals[1:]:
      best = jnp.maximum(best, val)
    idx = jnp.full(best.shape, epg, I32)
    for j in reversed(range(epg)):
      idx = jnp.where(vals[j] == best, j, idx)
    return idx

  i1 = first_argmax(cand)
  i2 = first_argmax([jnp.where(i1 == j, -jnp.inf, cand[j]) for j in range(epg)])
  pick = lambda idx: sum(jnp.where(idx == j, affc[j], 0.0) for j in range(epg))
  a1, a2 = pick(i1), pick(i2)
  denom = a1 + a2
  eid_ref[0] = jnp.concatenate([gsel * epg + i1, gsel * epg + i2], axis=0)
  wpad = jnp.concatenate([a1 / denom, a2 / denom, jnp.zeros((LANES - TOP_K, tm), F32)], axis=0)
  wrow_ref[0] = wpad.T


def _outproj(merged, x, w_out_bf16, g1, gain2, sc2, sh2, w_router_t, router_bias_col):
  bsz, s, d = x.shape
  tm = min(OUTPROJ_ROWS, s)
  row = pl.BlockSpec((1, tm, d), lambda b, i: (b, i, 0))
  mod = pl.BlockSpec((1, 1, d), lambda b, i: (b, 0, 0))
  full = lambda a: pl.BlockSpec(a.shape, lambda b, i: (0,) * a.ndim)
  gain2 = gain2.reshape(1, d)
  return pl.pallas_call(
      _outproj_kernel,
      grid=(bsz, s // tm),
      in_specs=[row, row, full(w_out_bf16), mod, full(gain2), mod, mod,
                full(w_router_t), full(router_bias_col)],
      out_specs=[row, row,
                 pl.BlockSpec((1, TOP_K, tm), lambda b, i: (b, 0, i)),
                 pl.BlockSpec((1, tm, LANES), lambda b, i: (b, i, 0))],
      out_shape=[jax.ShapeDtypeStruct((bsz, s, d), F32),
                 jax.ShapeDtypeStruct((bsz, s, d), F32),
                 jax.ShapeDtypeStruct((bsz, TOP_K, s), I32),
                 jax.ShapeDtypeStruct((bsz, s, LANES), F32)],
      compiler_params=_params(("arbitrary", "arbitrary")),
      name="outproj_router",
  )(merged, x, w_out_bf16, g1, gain2, sc2, sh2, w_router_t, router_bias_col)


def _rank_kernel(eid_ref, triu_ref, rank_ref, size_ref, carry_scr):
  @pl.when((pl.program_id(0) == 0) & (pl.program_id(1) == 0))
  def _():
    carry_scr[...] = jnp.zeros_like(carry_scr)

  tn = eid_ref.shape[2]
  eid = eid_ref[0]
  expert = lax.broadcasted_iota(I32, (N_EXPERTS, tn), 0)
  hot = [(expert == eid[slot:slot + 1, :]).astype(F32) for slot in range(TOP_K)]
  both = hot[0] + hot[1]
  before = jnp.dot(both.astype(BF16), triu_ref[...], preferred_element_type=F32)
  carry = carry_scr[...]
  count = before + jnp.concatenate([carry] * (tn // LANES), axis=1)
  rank_ref[0] = jnp.concatenate(
      [jnp.sum(hot[slot] * count, axis=0, keepdims=True) for slot in range(TOP_K)],
      axis=0).astype(I32)
  total = carry + jnp.dot(both.astype(BF16), jnp.ones((tn, LANES), BF16),
                          preferred_element_type=F32)
  carry_scr[...] = total
  size_ref[...] = total.astype(I32)


def _rank(eid, triu_bf16):
  bsz, _, s = eid.shape
  tn = min(RANK_COLS, s)
  return pl.pallas_call(
      _rank_kernel,
      grid=(bsz, s // tn),
      in_specs=[pl.BlockSpec((1, TOP_K, tn), lambda b, i: (b, 0, i)),
                pl.BlockSpec((tn, tn), lambda b, i: (0, 0))],
      out_specs=[pl.BlockSpec((1, TOP_K, tn), lambda b, i: (b, 0, i)),
                 pl.BlockSpec((N_EXPERTS, LANES), lambda b, i: (0, 0))],
      out_shape=[jax.ShapeDtypeStruct((bsz, TOP_K, s), I32),
                 jax.ShapeDtypeStruct((N_EXPERTS, LANES), I32)],
      scratch_shapes=[pltpu.VMEM((N_EXPERTS, LANES), F32)],
      compiler_params=_params(("arbitrary", "arbitrary")),
      name="rank",
  )(eid, triu_bf16)


def _row_copy(src_ref, src_row, dst_ref, dst_row, sem):
  return pltpu.make_async_copy(src_ref.at[pl.ds(src_row, 1), :],
                               dst_ref.at[pl.ds(dst_row, 1), :], sem)


def _scatter_kernel(start_ref, eid_ref, rank_ref, h_ref, buf_in_ref, buf_ref, sem):
  del buf_in_ref
  tn = h_ref.shape[1]
  src = h_ref.at[0]

  def issue(t, carry):
    for slot in range(TOP_K):
      dst = start_ref[eid_ref[0, slot, t]] + rank_ref[0, slot, t]
      _row_copy(src, t, buf_ref, dst, sem).start()
    return carry

  lax.fori_loop(0, tn, issue, 0)

  def drain(t, carry):
    for slot in range(TOP_K):
      _row_copy(src, 0, buf_ref, 0, sem).wait()
    return carry

  lax.fori_loop(0, tn, drain, 0)


def _scatter(pad_start, eid, rank, h2, n_rows):
  bsz, s, d = h2.shape
  tn = min(SCATTER_ROWS, s)
  smem = pl.BlockSpec((1, TOP_K, tn), lambda b, i, st: (b, 0, i), memory_space=pltpu.SMEM)
  grid_spec = pltpu.PrefetchScalarGridSpec(
      num_scalar_prefetch=1,
      grid=(bsz, s // tn),
      in_specs=[smem, smem,
                pl.BlockSpec((1, tn, d), lambda b, i, st: (b, i, 0)),
                pl.BlockSpec(memory_space=pl.ANY)],
      out_specs=pl.BlockSpec(memory_space=pl.ANY),
      scratch_shapes=[pltpu.SemaphoreType.DMA(())],
  )
  return pl.pallas_call(
      _scatter_kernel,
      grid_spec=grid_spec,
      out_shape=jax.ShapeDtypeStruct((n_rows, d), F32),
      input_output_aliases={4: 0},
      compiler_params=_params(("arbitrary", "arbitrary")),
      name="scatter_rows",
  )(pad_start, eid, rank, h2, jnp.zeros((n_rows, d), F32))


def _expert_kernel(blk_e_ref, x_ref, win_ref, wout_ref, o_ref, win_scr, wout_scr):
  i = pl.program_id(0)
  changed = (i == 0) | (blk_e_ref[i] != blk_e_ref[jnp.maximum(i - 1, 0)])

  @pl.when(changed)
  def _():
    win_scr[...] = win_ref[0].astype(BF16)
    wout_scr[...] = wout_ref[0].astype(BF16)

  hidden = jnp.dot(x_ref[...].astype(BF16), win_scr[...], preferred_element_type=F32)
  half = hidden.shape[1] // 2
  gate, up = hidden[:, :half], hidden[:, half:]
  act = (gate * _sigmoid(gate)) * up
  o_ref[...] = jnp.dot(act.astype(BF16), wout_scr[...], preferred_element_type=F32)


def _experts(blk_e, buf, w_e_in, w_e_out):
  n_rows, d = buf.shape
  d_hidden2 = w_e_in.shape[2]
  grid_spec = pltpu.PrefetchScalarGridSpec(
      num_scalar_prefetch=1,
      grid=(n_rows // M_BLK,),
      in_specs=[pl.BlockSpec((M_BLK, d), lambda i, be: (i, 0)),
                pl.BlockSpec((1, d, d_hidden2), lambda i, be: (be[i], 0, 0)),
                pl.BlockSpec((1, d_hidden2 // 2, d), lambda i, be: (be[i], 0, 0))],
      out_specs=pl.BlockSpec((M_BLK, d), lambda i, be: (i, 0)),
      scratch_shapes=[pltpu.VMEM((d, d_hidden2), BF16),
                      pltpu.VMEM((d_hidden2 // 2, d), BF16)],
  )
  return pl.pallas_call(
      _expert_kernel,
      grid_spec=grid_spec,
      out_shape=jax.ShapeDtypeStruct((n_rows, d), F32),
      compiler_params=_params(("arbitrary",)),
      name="experts",
  )(blk_e, buf, w_e_in, w_e_out)


def _combine_kernel(start_ref, eid_ref, rank_ref, x_ref, wrow_ref, g2_ref, gain_ref, yb_ref,
                    o_ref, y_scr, sem, final_norm):
  tn = x_ref.shape[1]

  def issue(t, carry):
    for slot in range(TOP_K):
      src = start_ref[eid_ref[0, slot, t]] + rank_ref[0, slot, t]
      _row_copy(yb_ref, src, y_scr.at[slot], t, sem).start()
    return carry

  lax.fori_loop(0, tn, issue, 0)

  def drain(t, carry):
    for slot in range(TOP_K):
      _row_copy(yb_ref, 0, y_scr.at[slot], 0, sem).wait()
    return carry

  lax.fori_loop(0, tn, drain, 0)

  wrow = wrow_ref[0]
  moe = wrow[:, 0:1] * y_scr[0] + wrow[:, 1:2] * y_scr[1]
  x = x_ref[0] + g2_ref[0] * moe
  if final_norm:
    x = (x * lax.rsqrt(jnp.mean(x * x, axis=-1, keepdims=True) + NORM_EPS)) * gain_ref[...]
  o_ref[0] = x


def _combine(pad_start, eid, rank, x, wrow, g2, final_gain, yb, final_norm):
  bsz, s, d = x.shape
  tn = min(COMBINE_ROWS, s)
  smem = pl.BlockSpec((1, TOP_K, tn), lambda b, i, st: (b, 0, i), memory_space=pltpu.SMEM)
  row = pl.BlockSpec((1, tn, d), lambda b, i, st: (b, i, 0))
  grid_spec = pltpu.PrefetchScalarGridSpec(
      num_scalar_prefetch=1,
      grid=(bsz, s // tn),
      in_specs=[smem, smem, row,
                pl.BlockSpec((1, tn, LANES), lambda b, i, st: (b, i, 0)),
                pl.BlockSpec((1, 1, d), lambda b, i, st: (b, 0, 0)),
                pl.BlockSpec((1, d), lambda b, i, st: (0, 0)),
                pl.BlockSpec(memory_space=pl.ANY)],
      out_specs=row,
      scratch_shapes=[pltpu.VMEM((TOP_K, tn, d), F32), pltpu.SemaphoreType.DMA(())],
  )
  return pl.pallas_call(
      functools.partial(_combine_kernel, final_norm=final_norm),
      grid_spec=grid_spec,
      out_shape=jax.ShapeDtypeStruct((bsz, s, d), F32),
      compiler_params=_params(("arbitrary", "arbitrary")),
      name="combine",
  )(pad_start, eid, rank, x, wrow, g2, final_gain.reshape(1, d), yb)


def _pad_rows(w, offset, total):
  return jnp.zeros((total, w.shape[1]), w.dtype).at[offset:offset + w.shape[0]].set(w)


def kernel(x, c, w_ada, b_ada, norm1_gain, norm2_gain, w_in, token_mu, conv_w, w0, decay_up,
           a0, aaa_up, gate_up, k_k, k_a, r_k, ln_x_w, ln_x_b, vmix_down, vmix_up, v0, w_out,
           w_router, router_bias, w_expert_in, w_expert_out, final_gain):
  bsz, s, d = x.shape
  depth = w_in.shape[0]
  n_tok = bsz * s
  n_rows = n_tok * TOP_K + N_EXPERTS * M_BLK

  ada = _ada(c, w_ada, b_ada)
  mods = ada.reshape(depth, bsz, 6, 1, d)

  lane = jnp.arange(LANES)
  bd_ones = (lane[:, None] // HEAD_DIM == lane[None, :] // HEAD_DIM).astype(F32)
  t = jnp.arange(WKV_CHUNK)
  tri = (t[:, None] >= t[None, :]).astype(F32)
  tc = jnp.arange(min(RANK_COLS, s))
  triu = (tc[:, None] < tc[None, :]).astype(BF16)

  perm = (jnp.arange(N_EXPERTS) % N_GROUPS) * EXPERTS_PER_GROUP + jnp.arange(N_EXPERTS) // N_GROUPS
  w_router_t = w_router.T[perm]
  router_bias_col = router_bias[perm].reshape(N_EXPERTS, 1)

  v_first = None
  for l in range(depth):
    sh1, sc1, g1, sh2, sc2, g2 = (mods[l, :, j] for j in range(6))
    has_vmix = l > 0
    vmix_down_pad = None
    if has_vmix:
      vmix_down_pad = jnp.zeros((d, LANES), BF16).at[:, :VMIX_RANK].set(
          vmix_down[l - 1].astype(BF16))
    p, hv = _inproj(x, norm1_gain[l], sc1, sh1, w_in[l].astype(BF16), vmix_down_pad)

    n_shift = token_mu.shape[1]
    mu3 = token_mu[l, :3 * d].reshape(3, d)
    mu_lr = token_mu[l, 3 * d:].reshape(1, n_shift - 3 * d)
    vecs = jnp.stack([w0[l], a0[l], k_k[l], k_a[l]])
    prep_out = _prep(
        p, mu3, mu_lr, conv_w[l], vecs,
        _pad_rows(decay_up[l], 0, LOW_RANK),
        _pad_rows(aaa_up[l], DECAY_RANK, LOW_RANK),
        _pad_rows(gate_up[l], DECAY_RANK + AAA_RANK, LOW_RANK),
        bd_ones,
        hv, v_first,
        _pad_rows(vmix_up[l - 1], 0, LANES) if has_vmix else None,
        v0[l - 1].reshape(1, d) if has_vmix else None)
    r_, k_, v_, lw, kk, beta, g, mconv, sgb = prep_out
    if not has_vmix:
      v_first = v_

    merged = _wkv(r_, k_, v_, lw, kk, beta, g, mconv, sgb, tri, bd_ones,
                  ln_x_w[l], ln_x_b[l], r_k[l])
    x, h2, eid, wrow = _outproj(merged, x, w_out[l].astype(BF16), g1, norm2_gain[l], sc2, sh2,
                                w_router_t, router_bias_col)

    rank, sizes = _rank(eid, triu)
    sizes = sizes[:, 0]
    padded = (sizes + M_BLK - 1) // M_BLK * M_BLK
    pad_end = jnp.cumsum(padded)
    pad_start = (pad_end - padded).astype(I32)
    blk_e = jnp.minimum(
        jnp.searchsorted(pad_end, jnp.arange(n_rows // M_BLK) * M_BLK, side="right"),
        N_EXPERTS - 1).astype(I32)

    buf = _scatter(pad_start, eid, rank, h2.reshape(n_tok, d).reshape(bsz, s, d), n_rows)
    yb = _experts(blk_e, buf, w_expert_in[l], w_expert_out[l])
    x = _combine(pad_start, eid, rank, x, wrow, g2, final_gain, yb, final_norm=(l == depth - 1))
  return x
```

```python
import functools
import itertools

import jax
import jax.numpy as jnp
from jax import lax
from jax.experimental import pallas as pl
from jax.experimental.pallas import tpu as pltpu

F32 = jnp.float32
BF16 = jnp.bfloat16
I32 = jnp.int32
HIGHEST = lax.Precision.HIGHEST

HEAD_DIM = 64
CONV_WIDTH = 3
DECAY_RANK = 64
AAA_RANK = 64
GATE_RANK = 128
LOW_RANK = DECAY_RANK + AAA_RANK + GATE_RANK
VMIX_RANK = 32
N_EXPERTS = 32
N_GROUPS = 8
EXPERTS_PER_GROUP = N_EXPERTS // N_GROUPS
TOP_K = 2
M_BLK = 128
NORM_EPS = 1e-6
LN_X_EPS = 64e-5

LANES = 128
SUBLANES = 8
HEADS_PER_VREG = LANES // HEAD_DIM
VMEM_LIMIT_BYTES = 56 * 1024 * 1024

WKV_CHUNK = 64
WKV_SEQ_BLOCK = 256
WKV_PAIRS = 8
INPROJ_ROWS = 1024
INPROJ_COLS = 1408
PREP_ROWS = 128
OUTPROJ_ROWS = 256
RANK_COLS = 512
SCATTER_ROWS = 256
COMBINE_ROWS = 256


def _dot_f32(a, b, dims=(((1,), (0,)), ((), ()))):
  return lax.dot_general(a, b, dims, precision=HIGHEST, preferred_element_type=F32)


def _dot_bf16(a, b, dims=(((1,), (0,)), ((), ()))):
  return lax.dot_general(a.astype(BF16), b.astype(BF16), dims,
                         preferred_element_type=F32)


_NT = (((1,), (1,)), ((), ()))
_TN = (((0,), (0,)), ((), ()))


def _sigmoid(x):
  return 1.0 / (1.0 + jnp.exp(-x))


def _params(semantics):
  return pltpu.CompilerParams(dimension_semantics=semantics,
                              vmem_limit_bytes=VMEM_LIMIT_BYTES)


def _ada_kernel(c_ref, w_ref, b_ref, o_ref):
  c = c_ref[...]
  cond = c * _sigmoid(c)
  o_ref[0] = _dot_f32(cond, w_ref[0]) + b_ref[0]


def _ada(c, w_ada, b_ada):
  depth, d, n = w_ada.shape
  bsz = c.shape[0]
  tn = d
  return pl.pallas_call(
      _ada_kernel,
      grid=(depth, n // tn),
      in_specs=[
          pl.BlockSpec((bsz, d), lambda l, j: (0, 0)),
          pl.BlockSpec((1, d, tn), lambda l, j: (l, 0, j)),
          pl.BlockSpec((1, 1, tn), lambda l, j: (l, 0, j)),
      ],
      out_specs=pl.BlockSpec((1, bsz, tn), lambda l, j: (l, 0, j)),
      out_shape=jax.ShapeDtypeStruct((depth, bsz, n), F32),
      compiler_params=_params(("arbitrary", "arbitrary")),
      name="ada",
  )(c, w_ada, b_ada.reshape(depth, 1, n))


def _modulated_norm(x, gain, scale, shift):
  y = x * lax.rsqrt(jnp.mean(x * x, axis=-1, keepdims=True) + NORM_EPS)
  return (y * gain) * (1.0 + scale) + shift


def _inproj_kernel(x_ref, gain_ref, sc_ref, sh_ref, w_ref, *rest, has_vmix):
  if has_vmix:
    wv_ref, p_ref, hv_ref, h_scr = rest
  else:
    p_ref, h_scr = rest

  @pl.when(pl.program_id(2) == 0)
  def _():
    h = _modulated_norm(x_ref[0], gain_ref[...], sc_ref[0], sh_ref[0])
    hb = h.astype(BF16)
    h_scr[...] = hb
    if has_vmix:
      hv_ref[0] = jnp.dot(hb, wv_ref[...], preferred_element_type=F32)

  p_ref[0] = jnp.dot(h_scr[...], w_ref[...], preferred_element_type=F32)


def _inproj(x, gain, sc, sh, w_bf16, vmix_down_bf16):
  bsz, s, d = x.shape
  n = w_bf16.shape[1]
  tm = min(INPROJ_ROWS, s)
  tn = INPROJ_COLS
  has_vmix = vmix_down_bf16 is not None
  in_specs = [
      pl.BlockSpec((1, tm, d), lambda b, i, j: (b, i, 0)),
      pl.BlockSpec((1, d), lambda b, i, j: (0, 0)),
      pl.BlockSpec((1, 1, d), lambda b, i, j: (b, 0, 0)),
      pl.BlockSpec((1, 1, d), lambda b, i, j: (b, 0, 0)),
      pl.BlockSpec((d, tn), lambda b, i, j: (0, j)),
  ]
  args = [x, gain.reshape(1, d), sc, sh, w_bf16]
  out_specs = [pl.BlockSpec((1, tm, tn), lambda b, i, j: (b, i, j))]
  out_shape = [jax.ShapeDtypeStruct((bsz, s, n), F32)]
  if has_vmix:
    in_specs.append(pl.BlockSpec((d, LANES), lambda b, i, j: (0, 0)))
    args.append(vmix_down_bf16)
    out_specs.append(pl.BlockSpec((1, tm, LANES), lambda b, i, j: (b, i, 0)))
    out_shape.append(jax.ShapeDtypeStruct((bsz, s, LANES), F32))
  outs = pl.pallas_call(
      functools.partial(_inproj_kernel, has_vmix=has_vmix),
      grid=(bsz, s // tm, n // tn),
      in_specs=in_specs,
      out_specs=out_specs,
      out_shape=out_shape,
      scratch_shapes=[pltpu.VMEM((tm, d), BF16)],
      compiler_params=_params(("arbitrary", "arbitrary", "arbitrary")),
      name="inproj",
  )(*args)
  return outs if has_vmix else (outs[0], None)


def _split_bf16(x, terms):
  parts = []
  for _ in range(terms - 1):
    part = x.astype(BF16)
    parts.append(part)
    x = x - part.astype(F32)
  parts.append(x.astype(BF16))
  return parts


def _head_sum(x, bd_ones):
  rows = x.shape[0]
  out = []
  for j in range(x.shape[1] // LANES):
    terms = jnp.concatenate(_split_bf16(x[:, j * LANES:(j + 1) * LANES], 2), axis=0)
    s = jnp.dot(terms, bd_ones, preferred_element_type=F32)
    out.append(s[:rows] + s[rows:])
  return out[0] if len(out) == 1 else jnp.concatenate(out, axis=1)


def _shift_rows(cur, prev_tail, n):
  rolled = pltpu.roll(cur, n, axis=0)
  row = lax.broadcasted_iota(I32, cur.shape, 0)
  out = rolled
  for j in range(n):
    out = jnp.where(row == j, prev_tail[SUBLANES - n + j:SUBLANES - n + j + 1, :], out)
  return out


def _prep_kernel(cb_ref, cc_ref, ch_ref, ga_ref, gb_ref, r_ref, k_ref, v_ref, lr_ref,
                 mu_ref, mulr_ref, convw_ref, vec_ref, dup_ref, aup_ref, gup_ref, bd_ref,
                 *rest, has_vmix):
  if has_vmix:
    hv_ref, vfirst_ref, vup_ref, v0_ref = rest[:4]
    rest = rest[4:]
  (ro_ref, ko_ref, vo_ref, lw_ref, kk_ref, beta_ref, g_ref, mconv_ref, sgb_ref,
   pr_scr, pk_scr, pv_scr, plr_scr, pu_scr) = rest

  @pl.when(pl.program_id(1) == 0)
  def _():
    pr_scr[...] = jnp.zeros_like(pr_scr)
    pk_scr[...] = jnp.zeros_like(pk_scr)
    pv_scr[...] = jnp.zeros_like(pv_scr)
    plr_scr[...] = jnp.zeros_like(plr_scr)
    pu_scr[...] = jnp.zeros_like(pu_scr)

  rows = r_ref.shape[1]
  tail = slice(rows - SUBLANES, rows)

  def token_shift(ref, scr, mu):
    cur = ref[0]
    prev = _shift_rows(cur, scr[...], 1)
    scr[...] = cur[tail, :]
    return cur + (prev - cur) * mu

  r = token_shift(r_ref, pr_scr, mu_ref[0:1, :])
  k = token_shift(k_ref, pk_scr, mu_ref[1:2, :])
  v = token_shift(v_ref, pv_scr, mu_ref[2:3, :])
  lr = token_shift(lr_ref, plr_scr, mulr_ref[...])

  w0 = vec_ref[0:1, :]
  a0 = vec_ref[1:2, :]
  k_k = vec_ref[2:3, :]
  k_a = vec_ref[3:4, :]

  w_pre = w0 + _dot_f32(jnp.tanh(lr), dup_ref[...])
  z = -w_pre
  softplus = jnp.maximum(z, 0.0) + jnp.log(1.0 + jnp.exp(-jnp.abs(z)))
  lw_ref[0] = -jnp.exp(-softplus - 0.5)
  a = _sigmoid(a0 + _dot_f32(lr, aup_ref[...]))
  g_ref[0] = _dot_f32(_sigmoid(lr), gup_ref[...])
  if has_vmix:
    mix = _sigmoid(v0_ref[...] + _dot_f32(hv_ref[0], vup_ref[...]))
    v = v + (vfirst_ref[0] - v) * mix
  kk = k * k_k
  norm = jnp.sqrt(_head_sum(kk * kk, bd_ref[...]))
  kk = kk / jnp.maximum(norm, 1e-12)
  ro_ref[0] = r
  ko_ref[0] = k * (1.0 + (a - 1.0) * k_a)
  vo_ref[0] = v
  kk_ref[0] = kk
  beta_ref[0] = a * kk

  u = cc_ref[0] * ch_ref[0]
  prev_u = pu_scr[...]
  conv = (convw_ref[0:1, :] * _shift_rows(u, prev_u, 2)
          + convw_ref[1:2, :] * _shift_rows(u, prev_u, 1)
          + convw_ref[2:3, :] * u)
  pu_scr[...] = u[tail, :]
  mconv_ref[0] = _sigmoid(ga_ref[0]) * (cb_ref[0] * conv)
  sgb_ref[0] = _sigmoid(gb_ref[0])


def _prep(p, mu3, mu_lr, conv_w, vecs, decay_up_pad, aaa_up_pad, gate_up_pad, bd_ones,
          hv, v_first, vmix_up_pad, v0):
  bsz, s, _ = p.shape
  d = conv_w.shape[1]
  ts = min(PREP_ROWS, s)
  has_vmix = hv is not None
  col = lambda c: pl.BlockSpec((1, ts, d), lambda b, i, c=c: (b, i, c))
  full = lambda a: pl.BlockSpec(a.shape, lambda b, i: (0,) * a.ndim)
  in_specs = [col(c) for c in range(8)]
  in_specs.append(pl.BlockSpec((1, ts, LOW_RANK), lambda b, i: (b, i, 8 * d // LOW_RANK)))
  consts = [mu3, mu_lr, conv_w, vecs, decay_up_pad, aaa_up_pad, gate_up_pad, bd_ones]
  in_specs += [full(a) for a in consts]
  args = [p] * 9 + consts
  if has_vmix:
    in_specs += [pl.BlockSpec((1, ts, LANES), lambda b, i: (b, i, 0)),
                 pl.BlockSpec((1, ts, d), lambda b, i: (b, i, 0)),
                 full(vmix_up_pad), full(v0)]
    args += [hv, v_first, vmix_up_pad, v0]
  out_spec = pl.BlockSpec((1, ts, d), lambda b, i: (b, i, 0))
  out_sds = jax.ShapeDtypeStruct((bsz, s, d), F32)
  return pl.pallas_call(
      functools.partial(_prep_kernel, has_vmix=has_vmix),
      grid=(bsz, s // ts),
      in_specs=in_specs,
      out_specs=[out_spec] * 9,
      out_shape=[out_sds] * 9,
      scratch_shapes=[pltpu.VMEM((SUBLANES, d), F32)] * 3
      + [pltpu.VMEM((SUBLANES, LOW_RANK), F32), pltpu.VMEM((SUBLANES, d), F32)],
      compiler_params=_params(("arbitrary", "arbitrary")),
      name="prep",
  )(*args)


def _wkv_kernel(r_ref, k_ref, v_ref, lw_ref, kk_ref, beta_ref, g_ref, mconv_ref, sgb_ref,
                tri_ref, bd_ref, lnw_ref, lnb_ref, rk_ref, o_ref, s_scr):
  c = WKV_CHUNK
  n_chunks = r_ref.shape[1] // c
  n_pairs = r_ref.shape[2] // LANES

  @pl.when(pl.program_id(2) == 0)
  def _():
    s_scr[...] = jnp.zeros_like(s_scr)

  lane = lax.broadcasted_iota(I32, (c, LANES), 1)
  head_masks = [(lane // HEAD_DIM == h).astype(BF16) for h in range(HEADS_PER_VREG)]
  n_stack = HEADS_PER_VREG * c
  t_row = lax.broadcasted_iota(I32, (n_stack, n_stack), 0) % c
  t_col = lax.broadcasted_iota(I32, (n_stack, n_stack), 1) % c
  strict = t_row > t_col
  incl = t_row >= t_col
  eye = (lax.broadcasted_iota(I32, (n_stack, n_stack), 0)
         == lax.broadcasted_iota(I32, (n_stack, n_stack), 1)).astype(F32)

  def stack(x):
    xb = x.astype(BF16)
    return jnp.concatenate([xb * m for m in head_masks], axis=0)

  def unstack(x):
    out = x[0:c]
    for h in range(1, HEADS_PER_VREG):
      out = out + x[h * c:(h + 1) * c]
    return out

  def mm(a, b, dims=(((1,), (0,)), ((), ()))):
    return lax.dot_general(a, b, dims, preferred_element_type=F32)

  tri = tri_ref[...]
  bd_ones = bd_ref[...]

  def pair_chunk(p, sl, cum):
    ln = slice(p * LANES, (p + 1) * LANES)
    r = r_ref[0, sl, ln]
    k = k_ref[0, sl, ln]
    v = v_ref[0, sl, ln]
    lw = lw_ref[0, sl, ln]
    kk = kk_ref[0, sl, ln]
    beta = beta_ref[0, sl, ln]

    cum_last = cum[c - 1:c, :]
    inv_p = jnp.exp(-cum)
    rem_p = jnp.exp(cum_last - cum)
    r_s = stack(r * jnp.exp(cum))
    a_s = stack(kk * jnp.exp(cum - lw))
    kh_s = stack(k * inv_p)
    bh_s = stack(beta * inv_p)
    kb_s = stack(k * rem_p)
    bb_s = stack(-beta * rem_p)
    v_s = stack(v)
    state = s_scr[p]
    yield

    sc = mm(jnp.concatenate([a_s, r_s], axis=0), jnp.concatenate([kh_s, bh_s], axis=0), _NT)
    m_k = jnp.where(strict, sc[:n_stack, :n_stack], 0.0).astype(BF16)
    m_b = jnp.where(strict, sc[:n_stack, n_stack:], 0.0)
    q_k = jnp.where(incl, sc[n_stack:, :n_stack], 0.0).astype(BF16)
    q_b = jnp.where(incl, sc[n_stack:, n_stack:], 0.0).astype(BF16)
    yield

    w_inv = eye - m_b
    power = m_b.astype(BF16)
    kv = mm(jnp.concatenate([m_k, q_k], axis=0), v_s)
    for _ in range(c.bit_length() - 2):
      power = mm(power, power).astype(BF16)
      yield
      w_inv = w_inv + mm(w_inv.astype(BF16), power)

    yield
    wa = mm(w_inv.astype(BF16), jnp.concatenate([a_s, kv[:n_stack].astype(BF16)], axis=1))
    a2_s = wa[:, :LANES].astype(BF16)
    v2_s = wa[:, LANES:]
    yield

    res = mm(jnp.concatenate([a2_s, r_s], axis=0), state.astype(BF16), _NT)
    u_s = res[:n_stack] + v2_s
    u_b = u_s.astype(BF16)
    yield
    y_s = res[n_stack:] + kv[n_stack:] - mm(q_b, u_b)
    s_scr[p] = state * jnp.exp(cum_last) + mm(
        jnp.concatenate([v_s, u_b], axis=0), jnp.concatenate([kb_s, bb_s], axis=0), _TN)
    y = unstack(y_s)
    yield

    inv_n = 1.0 / HEAD_DIM
    sums = _head_sum(jnp.concatenate([y, r * k * rk_ref[:, ln]], axis=0), bd_ones)
    yc = y - sums[:c] * inv_n
    var = _head_sum(yc * yc, bd_ones) * inv_n
    yn = yc * lax.rsqrt(var + LN_X_EPS) * lnw_ref[:, ln] + lnb_ref[:, ln]
    y_rwkv = (yn + sums[c:] * v) * g_ref[0, sl, ln]
    o_ref[0, sl, ln] = mconv_ref[0, sl, ln] + sgb_ref[0, sl, ln] * y_rwkv

  def chunk_body(ci, carry):
    sl = pl.ds(pl.multiple_of(ci * c, c), c)
    lw_terms = jnp.concatenate(_split_bf16(lw_ref[0, sl, :], 3), axis=1)
    cs = jnp.dot(tri, lw_terms, preferred_element_type=F32)
    width = n_pairs * LANES
    cum = cs[:, :width] + cs[:, width:2 * width] + cs[:, 2 * width:]
    chains = [pair_chunk(p, sl, cum[:, p * LANES:(p + 1) * LANES]) for p in range(n_pairs)]
    for _ in itertools.zip_longest(*chains):
      pass
    return carry

  lax.fori_loop(0, n_chunks, chunk_body, 0)


def _wkv(r, k, v, lw, kk, beta, g, mconv, sgb, tri, bd_ones, ln_w, ln_b, r_k):
  bsz, s, d = r.shape
  sb = min(WKV_SEQ_BLOCK, s)
  width = WKV_PAIRS * LANES
  seq = pl.BlockSpec((1, sb, width), lambda b, hp, i: (b, i, hp))
  vec = pl.BlockSpec((1, width), lambda b, hp, i: (0, hp))
  full = lambda a: pl.BlockSpec(a.shape, lambda b, hp, i: (0,) * a.ndim)
  return pl.pallas_call(
      _wkv_kernel,
      grid=(bsz, d // width, s // sb),
      in_specs=[seq] * 9 + [full(tri), full(bd_ones), vec, vec, vec],
      out_specs=seq,
      out_shape=jax.ShapeDtypeStruct((bsz, s, d), F32),
      scratch_shapes=[pltpu.VMEM((WKV_PAIRS, LANES, LANES), F32)],
      compiler_params=_params(("arbitrary", "arbitrary", "arbitrary")),
      name="wkv",
  )(r, k, v, lw, kk, beta, g, mconv, sgb, tri, bd_ones,
    ln_w.reshape(1, d), ln_b.reshape(1, d), r_k.reshape(1, d))


def _outproj_kernel(m_ref, x_ref, w_ref, g1_ref, gain_ref, sc_ref, sh_ref, wr_ref, rb_ref,
                    xo_ref, h2_ref, eid_ref, wrow_ref):
  mix = jnp.dot(m_ref[0].astype(BF16), w_ref[...], preferred_element_type=F32)
  x = x_ref[0] + g1_ref[0] * mix
  xo_ref[0] = x
  h2 = _modulated_norm(x, gain_ref[...], sc_ref[0], sh_ref[0])
  h2_ref[0] = h2

  logits = _dot_f32(wr_ref[...], h2, _NT)
  tm = logits.shape[1]
  epg, ng = EXPERTS_PER_GROUP, N_GROUPS
  aff = [_sigmoid(logits[j * ng:(j + 1) * ng, :]) for j in range(epg)]
  sel = [aff[j] + rb_ref[j * ng:(j + 1) * ng, :] for j in range(epg)]
  hi01, lo01 = jnp.maximum(sel[0], sel[1]), jnp.minimum(sel[0], sel[1])
  hi23, lo23 = jnp.maximum(sel[2], sel[3]), jnp.minimum(sel[2], sel[3])
  gscore = jnp.maximum(hi01, hi23) + jnp.maximum(jnp.minimum(hi01, hi23),
                                                 jnp.maximum(lo01, lo23))
  gid = lax.broadcasted_iota(I32, (ng, tm), 0)
  gmax = jnp.max(gscore, axis=0, keepdims=True)
  gsel = jnp.min(jnp.where(gscore == gmax, gid, ng), axis=0, keepdims=True)
  in_g = gid == gsel
  cand = [jnp.sum(jnp.where(in_g, sel[j], 0.0), axis=0, keepdims=True) for j in range(epg)]
  affc = [jnp.sum(jnp.where(in_g, aff[j], 0.0), axis=0, keepdims=True) for j in range(epg)]

  def first_argmax(vals):
    best = vals[0]
    for val in vals[1:]:
      best = jnp.maximum(best, val)
    idx = jnp.full(best.shape, epg, I32)
    for j in reversed(range(epg)):
      idx = jnp.where(vals[j] == best, j, idx)
    return idx

  i1 = first_argmax(cand)
  i2 = first_argmax([jnp.where(i1 == j, -jnp.inf, cand[j]) for j in range(epg)])
  pick = lambda idx: sum(jnp.where(idx == j, affc[j], 0.0) for j in range(epg))
  a1, a2 = pick(i1), pick(i2)
  denom = a1 + a2
  eid_ref[0] = jnp.concatenate([gsel * epg + i1, gsel * epg + i2], axis=0)
  wpad = jnp.concatenate([a1 / denom, a2 / denom, jnp.zeros((LANES - TOP_K, tm), F32)], axis=0)
  wrow_ref[0] = wpad.T


def _outproj(merged, x, w_out_bf16, g1, gain2, sc2, sh2, w_router_t, router_bias_col):
  bsz, s, d = x.shape
  tm = min(OUTPROJ_ROWS, s)
  row = pl.BlockSpec((1, tm, d), lambda b, i: (b, i, 0))
  mod = pl.BlockSpec((1, 1, d), lambda b, i: (b, 0, 0))
  full = lambda a: pl.BlockSpec(a.shape, lambda b, i: (0,) * a.ndim)
  gain2 = gain2.reshape(1, d)
  return pl.pallas_call(
      _outproj_kernel,
      grid=(bsz, s // tm),
      in_specs=[row, row, full(w_out_bf16), mod, full(gain2), mod, mod,
                full(w_router_t), full(router_bias_col)],
      out_specs=[row, row,
                 pl.BlockSpec((1, TOP_K, tm), lambda b, i: (b, 0, i)),
                 pl.BlockSpec((1, tm, LANES), lambda b, i: (b, i, 0))],
      out_shape=[jax.ShapeDtypeStruct((bsz, s, d), F32),
                 jax.ShapeDtypeStruct((bsz, s, d), F32),
                 jax.ShapeDtypeStruct((bsz, TOP_K, s), I32),
                 jax.ShapeDtypeStruct((bsz, s, LANES), F32)],
      compiler_params=_params(("arbitrary", "arbitrary")),
      name="outproj_router",
  )(merged, x, w_out_bf16, g1, gain2, sc2, sh2, w_router_t, router_bias_col)


def _rank_kernel(eid_ref, triu_ref, rank_ref, size_ref, carry_scr):
  @pl.when((pl.program_id(0) == 0) & (pl.program_id(1) == 0))
  def _():
    carry_scr[...] = jnp.zeros_like(carry_scr)

  tn = eid_ref.shape[2]
  eid = eid_ref[0]
  expert = lax.broadcasted_iota(I32, (N_EXPERTS, tn), 0)
  hot = [(expert == eid[slot:slot + 1, :]).astype(F32) for slot in range(TOP_K)]
  both = hot[0] + hot[1]
  before = jnp.dot(both.astype(BF16), triu_ref[...], preferred_element_type=F32)
  carry = carry_scr[...]
  count = before + jnp.concatenate([carry] * (tn // LANES), axis=1)
  rank_ref[0] = jnp.concatenate(
      [jnp.sum(hot[slot] * count, axis=0, keepdims=True) for slot in range(TOP_K)],
      axis=0).astype(I32)
  total = carry + jnp.dot(both.astype(BF16), jnp.ones((tn, LANES), BF16),
                          preferred_element_type=F32)
  carry_scr[...] = total
  size_ref[...] = total.astype(I32)


def _rank(eid, triu_bf16):
  bsz, _, s = eid.shape
  tn = min(RANK_COLS, s)
  return pl.pallas_call(
      _rank_kernel,
      grid=(bsz, s // tn),
      in_specs=[pl.BlockSpec((1, TOP_K, tn), lambda b, i: (b, 0, i)),
                pl.BlockSpec((tn, tn), lambda b, i: (0, 0))],
      out_specs=[pl.BlockSpec((1, TOP_K, tn), lambda b, i: (b, 0, i)),
                 pl.BlockSpec((N_EXPERTS, LANES), lambda b, i: (0, 0))],
      out_shape=[jax.ShapeDtypeStruct((bsz, TOP_K, s), I32),
                 jax.ShapeDtypeStruct((N_EXPERTS, LANES), I32)],
      scratch_shapes=[pltpu.VMEM((N_EXPERTS, LANES), F32)],
      compiler_params=_params(("arbitrary", "arbitrary")),
      name="rank",
  )(eid, triu_bf16)


def _row_copy(src_ref, src_row, dst_ref, dst_row, sem):
  return pltpu.make_async_copy(src_ref.at[pl.ds(src_row, 1), :],
                               dst_ref.at[pl.ds(dst_row, 1), :], sem)


def _scatter_kernel(start_ref, eid_ref, rank_ref, h_ref, buf_in_ref, buf_ref, sem):
  del buf_in_ref
  tn = h_ref.shape[1]
  src = h_ref.at[0]

  def issue(t, carry):
    for slot in range(TOP_K):
      dst = start_ref[eid_ref[0, slot, t]] + rank_ref[0, slot, t]
      _row_copy(src, t, buf_ref, dst, sem).start()
    return carry

  lax.fori_loop(0, tn, issue, 0)

  def drain(t, carry):
    for slot in range(TOP_K):
      _row_copy(src, 0, buf_ref, 0, sem).wait()
    return carry

  lax.fori_loop(0, tn, drain, 0)


def _scatter(pad_start, eid, rank, h2, n_rows):
  bsz, s, d = h2.shape
  tn = min(SCATTER_ROWS, s)
  smem = pl.BlockSpec((1, TOP_K, tn), lambda b, i, st: (b, 0, i), memory_space=pltpu.SMEM)
  grid_spec = pltpu.PrefetchScalarGridSpec(
      num_scalar_prefetch=1,
      grid=(bsz, s // tn),
      in_specs=[smem, smem,
                pl.BlockSpec((1, tn, d), lambda b, i, st: (b, i, 0)),
                pl.BlockSpec(memory_space=pl.ANY)],
      out_specs=pl.BlockSpec(memory_space=pl.ANY),
      scratch_shapes=[pltpu.SemaphoreType.DMA(())],
  )
  return pl.pallas_call(
      _scatter_kernel,
      grid_spec=grid_spec,
      out_shape=jax.ShapeDtypeStruct((n_rows, d), F32),
      input_output_aliases={4: 0},
      compiler_params=_params(("arbitrary", "arbitrary")),
      name="scatter_rows",
  )(pad_start, eid, rank, h2, jnp.zeros((n_rows, d), F32))


def _expert_kernel(blk_e_ref, x_ref, win_ref, wout_ref, o_ref, win_scr, wout_scr):
  i = pl.program_id(0)
  changed = (i == 0) | (blk_e_ref[i] != blk_e_ref[jnp.maximum(i - 1, 0)])

  @pl.when(changed)
  def _():
    win_scr[...] = win_ref[0, 0].astype(BF16)
    wout_scr[...] = wout_ref[0, 0].astype(BF16)

  hidden = jnp.dot(x_ref[...].astype(BF16), win_scr[...], preferred_element_type=F32)
  half = hidden.shape[1] // 2
  gate, up = hidden[:, :half], hidden[:, half:]
  act = (gate * _sigmoid(gate)) * up
  o_ref[...] = jnp.dot(act.astype(BF16), wout_scr[...], preferred_element_type=F32)


def _experts(blk_e, buf, w_e_in, w_e_out, layer):
  n_rows, d = buf.shape
  d_hidden2 = w_e_in.shape[3]
  grid_spec = pltpu.PrefetchScalarGridSpec(
      num_scalar_prefetch=1,
      grid=(n_rows // M_BLK,),
      in_specs=[pl.BlockSpec((M_BLK, d), lambda i, be: (i, 0)),
                pl.BlockSpec((1, 1, d, d_hidden2), lambda i, be: (layer, be[i], 0, 0)),
                pl.BlockSpec((1, 1, d_hidden2 // 2, d), lambda i, be: (layer, be[i], 0, 0))],
      out_specs=pl.BlockSpec((M_BLK, d), lambda i, be: (i, 0)),
      scratch_shapes=[pltpu.VMEM((d, d_hidden2), BF16),
                      pltpu.VMEM((d_hidden2 // 2, d), BF16)],
  )
  return pl.pallas_call(
      _expert_kernel,
      grid_spec=grid_spec,
      out_shape=jax.ShapeDtypeStruct((n_rows, d), F32),
      compiler_params=_params(("arbitrary",)),
      name="experts",
  )(blk_e, buf, w_e_in, w_e_out)


def _combine_kernel(start_ref, eid_ref, rank_ref, x_ref, wrow_ref, g2_ref, gain_ref, yb_ref,
                    o_ref, y_scr, sem, final_norm):
  tn = x_ref.shape[1]

  def issue(t, carry):
    for slot in range(TOP_K):
      src = start_ref[eid_ref[0, slot, t]] + rank_ref[0, slot, t]
      _row_copy(yb_ref, src, y_scr.at[slot], t, sem).start()
    return carry

  lax.fori_loop(0, tn, issue, 0)

  def drain(t, carry):
    for slot in range(TOP_K):
      _row_copy(yb_ref, 0, y_scr.at[slot], 0, sem).wait()
    return carry

  lax.fori_loop(0, tn, drain, 0)

  wrow = wrow_ref[0]
  moe = wrow[:, 0:1] * y_scr[0] + wrow[:, 1:2] * y_scr[1]
  x = x_ref[0] + g2_ref[0] * moe
  if final_norm:
    x = (x * lax.rsqrt(jnp.mean(x * x, axis=-1, keepdims=True) + NORM_EPS)) * gain_ref[...]
  o_ref[0] = x


def _combine(pad_start, eid, rank, x, wrow, g2, final_gain, yb, final_norm):
  bsz, s, d = x.shape
  tn = min(COMBINE_ROWS, s)
  smem = pl.BlockSpec((1, TOP_K, tn), lambda b, i, st: (b, 0, i), memory_space=pltpu.SMEM)
  row = pl.BlockSpec((1, tn, d), lambda b, i, st: (b, i, 0))
  grid_spec = pltpu.PrefetchScalarGridSpec(
      num_scalar_prefetch=1,
      grid=(bsz, s // tn),
      in_specs=[smem, smem, row,
                pl.BlockSpec((1, tn, LANES), lambda b, i, st: (b, i, 0)),
                pl.BlockSpec((1, 1, d), lambda b, i, st: (b, 0, 0)),
                pl.BlockSpec((1, d), lambda b, i, st: (0, 0)),
                pl.BlockSpec(memory_space=pl.ANY)],
      out_specs=row,
      scratch_shapes=[pltpu.VMEM((TOP_K, tn, d), F32), pltpu.SemaphoreType.DMA(())],
  )
  return pl.pallas_call(
      functools.partial(_combine_kernel, final_norm=final_norm),
      grid_spec=grid_spec,
      out_shape=jax.ShapeDtypeStruct((bsz, s, d), F32),
      compiler_params=_params(("arbitrary", "arbitrary")),
      name="combine",
  )(pad_start, eid, rank, x, wrow, g2, final_gain.reshape(1, d), yb)


def _pad_rows(w, offset, total):
  return jnp.pad(w, ((offset, total - offset - w.shape[0]), (0, 0)))


def kernel(x, c, w_ada, b_ada, norm1_gain, norm2_gain, w_in, token_mu, conv_w, w0, decay_up,
           a0, aaa_up, gate_up, k_k, k_a, r_k, ln_x_w, ln_x_b, vmix_down, vmix_up, v0, w_out,
           w_router, router_bias, w_expert_in, w_expert_out, final_gain):
  bsz, s, d = x.shape
  depth = w_in.shape[0]
  n_tok = bsz * s
  n_rows = n_tok * TOP_K + N_EXPERTS * M_BLK

  ada = _ada(c, w_ada, b_ada)
  mods = ada.reshape(depth, bsz, 6, 1, d)

  lane = jnp.arange(LANES)
  bd_ones = (lane[:, None] // HEAD_DIM == lane[None, :] // HEAD_DIM).astype(BF16)
  t = jnp.arange(WKV_CHUNK)
  tri = (t[:, None] >= t[None, :]).astype(BF16)
  tc = jnp.arange(min(RANK_COLS, s))
  triu = (tc[:, None] < tc[None, :]).astype(BF16)

  perm = (jnp.arange(N_EXPERTS) % N_GROUPS) * EXPERTS_PER_GROUP + jnp.arange(N_EXPERTS) // N_GROUPS
  w_router_t = w_router.T[perm]
  router_bias_col = router_bias[perm].reshape(N_EXPERTS, 1)

  v_first = None
  for l in range(depth):
    sh1, sc1, g1, sh2, sc2, g2 = (mods[l, :, j] for j in range(6))
    has_vmix = l > 0
    vmix_down_pad = None
    if has_vmix:
      vmix_down_pad = jnp.zeros((d, LANES), BF16).at[:, :VMIX_RANK].set(
          vmix_down[l - 1].astype(BF16))
    p, hv = _inproj(x, norm1_gain[l], sc1, sh1, w_in[l].astype(BF16), vmix_down_pad)

    n_shift = token_mu.shape[1]
    mu3 = token_mu[l, :3 * d].reshape(3, d)
    mu_lr = token_mu[l, 3 * d:].reshape(1, n_shift - 3 * d)
    vecs = jnp.stack([w0[l], a0[l], k_k[l], k_a[l]])
    prep_out = _prep(
        p, mu3, mu_lr, conv_w[l], vecs,
        _pad_rows(decay_up[l], 0, LOW_RANK),
        _pad_rows(aaa_up[l], DECAY_RANK, LOW_RANK),
        _pad_rows(gate_up[l], DECAY_RANK + AAA_RANK, LOW_RANK),
        bd_ones,
        hv, v_first,
        _pad_rows(vmix_up[l - 1], 0, LANES) if has_vmix else None,
        v0[l - 1].reshape(1, d) if has_vmix else None)
    r_, k_, v_, lw, kk, beta, g, mconv, sgb = prep_out
    if not has_vmix:
      v_first = v_

    merged = _wkv(r_, k_, v_, lw, kk, beta, g, mconv, sgb, tri, bd_ones,
                  ln_x_w[l], ln_x_b[l], r_k[l])
    x, h2, eid, wrow = _outproj(merged, x, w_out[l].astype(BF16), g1, norm2_gain[l], sc2, sh2,
                                w_router_t, router_bias_col)

    rank, sizes = _rank(eid, triu)
    sizes = sizes[:, 0]
    padded = (sizes + M_BLK - 1) // M_BLK * M_BLK
    pad_end = jnp.cumsum(padded)
    pad_start = (pad_end - padded).astype(I32)
    blk_first_row = jnp.arange(n_rows // M_BLK, dtype=I32) * M_BLK
    blk_e = jnp.minimum(
        jnp.sum((pad_end[None, :] <= blk_first_row[:, None]).astype(I32), axis=1),
        N_EXPERTS - 1)

    buf = _scatter(pad_start, eid, rank, h2, n_rows)
    yb = _experts(blk_e, buf, w_expert_in, w_expert_out, l)
    x = _combine(pad_start, eid, rank, x, wrow, g2, final_gain, yb, final_norm=(l == depth - 1))
  return x
```

```python
import functools
import itertools

import jax
import jax.numpy as jnp
from jax import lax
from jax.experimental import pallas as pl
from jax.experimental.pallas import tpu as pltpu

F32 = jnp.float32
BF16 = jnp.bfloat16
I32 = jnp.int32
HIGHEST = lax.Precision.HIGHEST

HEAD_DIM = 64
CONV_WIDTH = 3
DECAY_RANK = 64
AAA_RANK = 64
GATE_RANK = 128
LOW_RANK = DECAY_RANK + AAA_RANK + GATE_RANK
VMIX_RANK = 32
N_EXPERTS = 32
N_GROUPS = 8
EXPERTS_PER_GROUP = N_EXPERTS // N_GROUPS
TOP_K = 2
M_BLK = 256
NORM_EPS = 1e-6
LN_X_EPS = 64e-5

LANES = 128
SUBLANES = 8
HEADS_PER_VREG = LANES // HEAD_DIM
VMEM_LIMIT_BYTES = 56 * 1024 * 1024

WKV_CHUNK = 64
WKV_SEQ_BLOCK = 256
WKV_PAIRS = 8
INPROJ_ROWS = 1024
INPROJ_COLS = 1408
PREP_ROWS = 128
OUTPROJ_ROWS = 256
RANK_COLS = 512
SCATTER_ROWS = 512
COMBINE_ROWS = 512


def _dot_f32(a, b, dims=(((1,), (0,)), ((), ()))):
  return lax.dot_general(a, b, dims, precision=HIGHEST, preferred_element_type=F32)


def _dot_bf16(a, b, dims=(((1,), (0,)), ((), ()))):
  return lax.dot_general(a.astype(BF16), b.astype(BF16), dims,
                         preferred_element_type=F32)


_NT = (((1,), (1,)), ((), ()))
_TN = (((0,), (0,)), ((), ()))


def _sigmoid(x):
  return 1.0 / (1.0 + jnp.exp(-x))


def _params(semantics):
  return pltpu.CompilerParams(dimension_semantics=semantics,
                              vmem_limit_bytes=VMEM_LIMIT_BYTES)


def _ada_kernel(c_ref, w_ref, b_ref, o_ref):
  c = c_ref[...]
  cond = c * _sigmoid(c)
  o_ref[0] = _dot_f32(cond, w_ref[0]) + b_ref[0]


def _ada(c, w_ada, b_ada):
  depth, d, n = w_ada.shape
  bsz = c.shape[0]
  tn = d
  return pl.pallas_call(
      _ada_kernel,
      grid=(depth, n // tn),
      in_specs=[
          pl.BlockSpec((bsz, d), lambda l, j: (0, 0)),
          pl.BlockSpec((1, d, tn), lambda l, j: (l, 0, j)),
          pl.BlockSpec((1, 1, tn), lambda l, j: (l, 0, j)),
      ],
      out_specs=pl.BlockSpec((1, bsz, tn), lambda l, j: (l, 0, j)),
      out_shape=jax.ShapeDtypeStruct((depth, bsz, n), F32),
      compiler_params=_params(("arbitrary", "arbitrary")),
      name="ada",
  )(c, w_ada, b_ada.reshape(depth, 1, n))


def _modulated_norm(x, gain, scale, shift):
  y = x * lax.rsqrt(jnp.mean(x * x, axis=-1, keepdims=True) + NORM_EPS)
  return (y * gain) * (1.0 + scale) + shift


def _inproj_kernel(x_ref, gain_ref, sc_ref, sh_ref, w_ref, *rest, has_vmix):
  if has_vmix:
    wv_ref, p_ref, hv_ref, h_scr = rest
  else:
    p_ref, h_scr = rest

  @pl.when(pl.program_id(2) == 0)
  def _():
    h = _modulated_norm(x_ref[0], gain_ref[...], sc_ref[0], sh_ref[0])
    hb = h.astype(BF16)
    h_scr[...] = hb
    if has_vmix:
      hv_ref[0] = jnp.dot(hb, wv_ref[...], preferred_element_type=F32)

  p_ref[0] = jnp.dot(h_scr[...], w_ref[...], preferred_element_type=F32)


def _inproj(x, gain, sc, sh, w_bf16, vmix_down_bf16):
  bsz, s, d = x.shape
  n = w_bf16.shape[1]
  tm = min(INPROJ_ROWS, s)
  tn = INPROJ_COLS
  has_vmix = vmix_down_bf16 is not None
  in_specs = [
      pl.BlockSpec((1, tm, d), lambda b, i, j: (b, i, 0)),
      pl.BlockSpec((1, d), lambda b, i, j: (0, 0)),
      pl.BlockSpec((1, 1, d), lambda b, i, j: (b, 0, 0)),
      pl.BlockSpec((1, 1, d), lambda b, i, j: (b, 0, 0)),
      pl.BlockSpec((d, tn), lambda b, i, j: (0, j)),
  ]
  args = [x, gain.reshape(1, d), sc, sh, w_bf16]
  out_specs = [pl.BlockSpec((1, tm, tn), lambda b, i, j: (b, i, j))]
  out_shape = [jax.ShapeDtypeStruct((bsz, s, n), F32)]
  if has_vmix:
    in_specs.append(pl.BlockSpec((d, LANES), lambda b, i, j: (0, 0)))
    args.append(vmix_down_bf16)
    out_specs.append(pl.BlockSpec((1, tm, LANES), lambda b, i, j: (b, i, 0)))
    out_shape.append(jax.ShapeDtypeStruct((bsz, s, LANES), F32))
  outs = pl.pallas_call(
      functools.partial(_inproj_kernel, has_vmix=has_vmix),
      grid=(bsz, s // tm, n // tn),
      in_specs=in_specs,
      out_specs=out_specs,
      out_shape=out_shape,
      scratch_shapes=[pltpu.VMEM((tm, d), BF16)],
      compiler_params=_params(("arbitrary", "arbitrary", "arbitrary")),
      name="inproj",
  )(*args)
  return outs if has_vmix else (outs[0], None)


def _split_bf16(x, terms):
  parts = []
  for _ in range(terms - 1):
    part = x.astype(BF16)
    parts.append(part)
    x = x - part.astype(F32)
  parts.append(x.astype(BF16))
  return parts


def _head_sum(x, bd_ones):
  rows = x.shape[0]
  out = []
  for j in range(x.shape[1] // LANES):
    terms = jnp.concatenate(_split_bf16(x[:, j * LANES:(j + 1) * LANES], 2), axis=0)
    s = jnp.dot(terms, bd_ones, preferred_element_type=F32)
    out.append(s[:rows] + s[rows:])
  return out[0] if len(out) == 1 else jnp.concatenate(out, axis=1)


def _shift_rows(cur, prev_tail, n):
  rolled = pltpu.roll(cur, n, axis=0)
  row = lax.broadcasted_iota(I32, cur.shape, 0)
  out = rolled
  for j in range(n):
    out = jnp.where(row == j, prev_tail[SUBLANES - n + j:SUBLANES - n + j + 1, :], out)
  return out


def _prep_kernel(cb_ref, cc_ref, ch_ref, ga_ref, gb_ref, r_ref, k_ref, v_ref, lr_ref,
                 mu_ref, mulr_ref, convw_ref, vec_ref, dup_ref, aup_ref, gup_ref, bd_ref,
                 *rest, has_vmix):
  if has_vmix:
    hv_ref, vfirst_ref, vup_ref, v0_ref = rest[:4]
    rest = rest[4:]
  (ro_ref, ko_ref, vo_ref, lw_ref, kk_ref, beta_ref, g_ref, mconv_ref, sgb_ref,
   pr_scr, pk_scr, pv_scr, plr_scr, pu_scr) = rest

  @pl.when(pl.program_id(1) == 0)
  def _():
    pr_scr[...] = jnp.zeros_like(pr_scr)
    pk_scr[...] = jnp.zeros_like(pk_scr)
    pv_scr[...] = jnp.zeros_like(pv_scr)
    plr_scr[...] = jnp.zeros_like(plr_scr)
    pu_scr[...] = jnp.zeros_like(pu_scr)

  rows = r_ref.shape[1]
  tail = slice(rows - SUBLANES, rows)

  def token_shift(ref, scr, mu):
    cur = ref[0]
    prev = _shift_rows(cur, scr[...], 1)
    scr[...] = cur[tail, :]
    return cur + (prev - cur) * mu

  r = token_shift(r_ref, pr_scr, mu_ref[0:1, :])
  k = token_shift(k_ref, pk_scr, mu_ref[1:2, :])
  v = token_shift(v_ref, pv_scr, mu_ref[2:3, :])
  lr = token_shift(lr_ref, plr_scr, mulr_ref[...])

  w0 = vec_ref[0:1, :]
  a0 = vec_ref[1:2, :]
  k_k = vec_ref[2:3, :]
  k_a = vec_ref[3:4, :]

  w_pre = w0 + _dot_f32(jnp.tanh(lr), dup_ref[...])
  z = -w_pre
  softplus = jnp.maximum(z, 0.0) + jnp.log(1.0 + jnp.exp(-jnp.abs(z)))
  lw_ref[0] = -jnp.exp(-softplus - 0.5)
  a = _sigmoid(a0 + _dot_f32(lr, aup_ref[...]))
  g_ref[0] = _dot_f32(_sigmoid(lr), gup_ref[...])
  if has_vmix:
    mix = _sigmoid(v0_ref[...] + _dot_f32(hv_ref[0], vup_ref[...]))
    v = v + (vfirst_ref[0] - v) * mix
  kk = k * k_k
  norm = jnp.sqrt(_head_sum(kk * kk, bd_ref[...]))
  kk = kk / jnp.maximum(norm, 1e-12)
  ro_ref[0] = r
  ko_ref[0] = k * (1.0 + (a - 1.0) * k_a)
  vo_ref[0] = v
  kk_ref[0] = kk
  beta_ref[0] = a * kk

  u = cc_ref[0] * ch_ref[0]
  prev_u = pu_scr[...]
  conv = (convw_ref[0:1, :] * _shift_rows(u, prev_u, 2)
          + convw_ref[1:2, :] * _shift_rows(u, prev_u, 1)
          + convw_ref[2:3, :] * u)
  pu_scr[...] = u[tail, :]
  mconv_ref[0] = _sigmoid(ga_ref[0]) * (cb_ref[0] * conv)
  sgb_ref[0] = _sigmoid(gb_ref[0])


def _prep(p, mu3, mu_lr, conv_w, vecs, decay_up_pad, aaa_up_pad, gate_up_pad, bd_ones,
          hv, v_first, vmix_up_pad, v0):
  bsz, s, _ = p.shape
  d = conv_w.shape[1]
  ts = min(PREP_ROWS, s)
  has_vmix = hv is not None
  col = lambda c: pl.BlockSpec((1, ts, d), lambda b, i, c=c: (b, i, c))
  full = lambda a: pl.BlockSpec(a.shape, lambda b, i: (0,) * a.ndim)
  in_specs = [col(c) for c in range(8)]
  in_specs.append(pl.BlockSpec((1, ts, LOW_RANK), lambda b, i: (b, i, 8 * d // LOW_RANK)))
  consts = [mu3, mu_lr, conv_w, vecs, decay_up_pad, aaa_up_pad, gate_up_pad, bd_ones]
  in_specs += [full(a) for a in consts]
  args = [p] * 9 + consts
  if has_vmix:
    in_specs += [pl.BlockSpec((1, ts, LANES), lambda b, i: (b, i, 0)),
                 pl.BlockSpec((1, ts, d), lambda b, i: (b, i, 0)),
                 full(vmix_up_pad), full(v0)]
    args += [hv, v_first, vmix_up_pad, v0]
  out_spec = pl.BlockSpec((1, ts, d), lambda b, i: (b, i, 0))
  out_sds = jax.ShapeDtypeStruct((bsz, s, d), F32)
  return pl.pallas_call(
      functools.partial(_prep_kernel, has_vmix=has_vmix),
      grid=(bsz, s // ts),
      in_specs=in_specs,
      out_specs=[out_spec] * 9,
      out_shape=[out_sds] * 9,
      scratch_shapes=[pltpu.VMEM((SUBLANES, d), F32)] * 3
      + [pltpu.VMEM((SUBLANES, LOW_RANK), F32), pltpu.VMEM((SUBLANES, d), F32)],
      compiler_params=_params(("arbitrary", "arbitrary")),
      name="prep",
  )(*args)


def _wkv_kernel(r_ref, k_ref, v_ref, lw_ref, kk_ref, beta_ref, g_ref, mconv_ref, sgb_ref,
                tri_ref, bd_ref, lnw_ref, lnb_ref, rk_ref, o_ref, s_scr):
  c = WKV_CHUNK
  n_chunks = r_ref.shape[1] // c
  n_pairs = r_ref.shape[2] // LANES

  @pl.when(pl.program_id(2) == 0)
  def _():
    s_scr[...] = jnp.zeros_like(s_scr)

  lane = lax.broadcasted_iota(I32, (c, LANES), 1)
  head_masks = [(lane // HEAD_DIM == h).astype(BF16) for h in range(HEADS_PER_VREG)]
  n_stack = HEADS_PER_VREG * c
  t_row = lax.broadcasted_iota(I32, (n_stack, n_stack), 0) % c
  t_col = lax.broadcasted_iota(I32, (n_stack, n_stack), 1) % c
  strict = t_row > t_col
  incl = t_row >= t_col
  eye = (lax.broadcasted_iota(I32, (n_stack, n_stack), 0)
         == lax.broadcasted_iota(I32, (n_stack, n_stack), 1)).astype(F32)

  def stack(x):
    xb = x.astype(BF16)
    return jnp.concatenate([xb * m for m in head_masks], axis=0)

  def unstack(x):
    out = x[0:c]
    for h in range(1, HEADS_PER_VREG):
      out = out + x[h * c:(h + 1) * c]
    return out

  def mm(a, b, dims=(((1,), (0,)), ((), ()))):
    return lax.dot_general(a, b, dims, preferred_element_type=F32)

  tri = tri_ref[...]
  bd_ones = bd_ref[...]

  def pair_chunk(p, sl, cum):
    ln = slice(p * LANES, (p + 1) * LANES)
    r = r_ref[0, sl, ln]
    k = k_ref[0, sl, ln]
    v = v_ref[0, sl, ln]
    lw = lw_ref[0, sl, ln]
    kk = kk_ref[0, sl, ln]
    beta = beta_ref[0, sl, ln]

    cum_last = cum[c - 1:c, :]
    inv_p = jnp.exp(-cum)
    rem_p = jnp.exp(cum_last - cum)
    r_s = stack(r * jnp.exp(cum))
    a_s = stack(kk * jnp.exp(cum - lw))
    kh_s = stack(k * inv_p)
    bh_s = stack(beta * inv_p)
    kb_s = stack(k * rem_p)
    bb_s = stack(-beta * rem_p)
    v_s = stack(v)
    state = s_scr[p]
    yield

    sc = mm(jnp.concatenate([a_s, r_s], axis=0), jnp.concatenate([kh_s, bh_s], axis=0), _NT)
    m_k = jnp.where(strict, sc[:n_stack, :n_stack], 0.0).astype(BF16)
    m_b = jnp.where(strict, sc[:n_stack, n_stack:], 0.0)
    q_k = jnp.where(incl, sc[n_stack:, :n_stack], 0.0).astype(BF16)
    q_b = jnp.where(incl, sc[n_stack:, n_stack:], 0.0).astype(BF16)
    yield

    w_inv = eye - m_b
    power = m_b.astype(BF16)
    kv = mm(jnp.concatenate([m_k, q_k], axis=0), v_s)
    for _ in range(c.bit_length() - 2):
      power = mm(power, power).astype(BF16)
      yield
      w_inv = w_inv + mm(w_inv.astype(BF16), power)

    yield
    wa = mm(w_inv.astype(BF16), jnp.concatenate([a_s, kv[:n_stack].astype(BF16)], axis=1))
    a2_s = wa[:, :LANES].astype(BF16)
    v2_s = wa[:, LANES:]
    yield

    res = mm(jnp.concatenate([a2_s, r_s], axis=0), state.astype(BF16), _NT)
    u_s = res[:n_stack] + v2_s
    u_b = u_s.astype(BF16)
    yield
    y_s = res[n_stack:] + kv[n_stack:] - mm(q_b, u_b)
    s_scr[p] = state * jnp.exp(cum_last) + mm(
        jnp.concatenate([v_s, u_b], axis=0), jnp.concatenate([kb_s, bb_s], axis=0), _TN)
    y = unstack(y_s)
    yield

    inv_n = 1.0 / HEAD_DIM
    sums = _head_sum(jnp.concatenate([y, r * k * rk_ref[:, ln]], axis=0), bd_ones)
    yc = y - sums[:c] * inv_n
    var = _head_sum(yc * yc, bd_ones) * inv_n
    yn = yc * lax.rsqrt(var + LN_X_EPS) * lnw_ref[:, ln] + lnb_ref[:, ln]
    y_rwkv = (yn + sums[c:] * v) * g_ref[0, sl, ln]
    o_ref[0, sl, ln] = mconv_ref[0, sl, ln] + sgb_ref[0, sl, ln] * y_rwkv

  def chunk_body(ci, carry):
    sl = pl.ds(pl.multiple_of(ci * c, c), c)
    lw_terms = jnp.concatenate(_split_bf16(lw_ref[0, sl, :], 3), axis=1)
    cs = jnp.dot(tri, lw_terms, preferred_element_type=F32)
    width = n_pairs * LANES
    cum = cs[:, :width] + cs[:, width:2 * width] + cs[:, 2 * width:]
    chains = [pair_chunk(p, sl, cum[:, p * LANES:(p + 1) * LANES]) for p in range(n_pairs)]
    for _ in itertools.zip_longest(*chains):
      pass
    return carry

  lax.fori_loop(0, n_chunks, chunk_body, 0)


def _wkv(r, k, v, lw, kk, beta, g, mconv, sgb, tri, bd_ones, ln_w, ln_b, r_k):
  bsz, s, d = r.shape
  sb = min(WKV_SEQ_BLOCK, s)
  width = WKV_PAIRS * LANES
  seq = pl.BlockSpec((1, sb, width), lambda b, hp, i: (b, i, hp))
  vec = pl.BlockSpec((1, width), lambda b, hp, i: (0, hp))
  full = lambda a: pl.BlockSpec(a.shape, lambda b, hp, i: (0,) * a.ndim)
  return pl.pallas_call(
      _wkv_kernel,
      grid=(bsz, d // width, s // sb),
      in_specs=[seq] * 9 + [full(tri), full(bd_ones), vec, vec, vec],
      out_specs=seq,
      out_shape=jax.ShapeDtypeStruct((bsz, s, d), F32),
      scratch_shapes=[pltpu.VMEM((WKV_PAIRS, LANES, LANES), F32)],
      compiler_params=_params(("arbitrary", "arbitrary", "arbitrary")),
      name="wkv",
  )(r, k, v, lw, kk, beta, g, mconv, sgb, tri, bd_ones,
    ln_w.reshape(1, d), ln_b.reshape(1, d), r_k.reshape(1, d))


def _outproj_kernel(m_ref, x_ref, w_ref, g1_ref, gain_ref, sc_ref, sh_ref, wr_ref, rb_ref,
                    xo_ref, h2_ref, eid_ref, wrow_ref):
  mix = jnp.dot(m_ref[0].astype(BF16), w_ref[...], preferred_element_type=F32)
  x = x_ref[0] + g1_ref[0] * mix
  xo_ref[0] = x
  h2 = _modulated_norm(x, gain_ref[...], sc_ref[0], sh_ref[0])
  h2_ref[0] = h2

  logits = _dot_f32(wr_ref[...], h2, _NT)
  tm = logits.shape[1]
  epg, ng = EXPERTS_PER_GROUP, N_GROUPS
  aff = [_sigmoid(logits[j * ng:(j + 1) * ng, :]) for j in range(epg)]
  sel = [aff[j] + rb_ref[j * ng:(j + 1) * ng, :] for j in range(epg)]
  hi01, lo01 = jnp.maximum(sel[0], sel[1]), jnp.minimum(sel[0], sel[1])
  hi23, lo23 = jnp.maximum(sel[2], sel[3]), jnp.minimum(sel[2], sel[3])
  gscore = jnp.maximum(hi01, hi23) + jnp.maximum(jnp.minimum(hi01, hi23),
                                                 jnp.maximum(lo01, lo23))
  gid = lax.broadcasted_iota(I32, (ng, tm), 0)
  gmax = jnp.max(gscore, axis=0, keepdims=True)
  gsel = jnp.min(jnp.where(gscore == gmax, gid, ng), axis=0, keepdims=True)
  in_g = gid == gsel
  cand = [jnp.sum(jnp.where(in_g, sel[j], 0.0), axis=0, keepdims=True) for j in range(epg)]
  affc = [jnp.sum(jnp.where(in_g, aff[j], 0.0), axis=0, keepdims=True) for j in range(epg)]

  def first_argmax(vals):
    best = vals[0]
    for val in vals[1:]:
      best = jnp.maximum(best, val)
    idx = jnp.full(best.shape, epg, I32)
    for j in reversed(range(epg)):
      idx = jnp.where(vals[j] == best, j, idx)
    return idx

  i1 = first_argmax(cand)
  i2 = first_argmax([jnp.where(i1 == j, -jnp.inf, cand[j]) for j in range(epg)])
  pick = lambda idx: sum(jnp.where(idx == j, affc[j], 0.0) for j in range(epg))
  a1, a2 = pick(i1), pick(i2)
  denom = a1 + a2
  eid_ref[0] = jnp.concatenate([gsel * epg + i1, gsel * epg + i2], axis=0)
  wpad = jnp.concatenate([a1 / denom, a2 / denom, jnp.zeros((LANES - TOP_K, tm), F32)], axis=0)
  wrow_ref[0] = wpad.T


def _outproj(merged, x, w_out_bf16, g1, gain2, sc2, sh2, w_router_t, router_bias_col):
  bsz, s, d = x.shape
  tm = min(OUTPROJ_ROWS, s)
  row = pl.BlockSpec((1, tm, d), lambda b, i: (b, i, 0))
  mod = pl.BlockSpec((1, 1, d), lambda b, i: (b, 0, 0))
  full = lambda a: pl.BlockSpec(a.shape, lambda b, i: (0,) * a.ndim)
  gain2 = gain2.reshape(1, d)
  return pl.pallas_call(
      _outproj_kernel,
      grid=(bsz, s // tm),
      in_specs=[row, row, full(w_out_bf16), mod, full(gain2), mod, mod,
                full(w_router_t), full(router_bias_col)],
      out_specs=[row, row,
                 pl.BlockSpec((1, TOP_K, tm), lambda b, i: (b, 0, i)),
                 pl.BlockSpec((1, tm, LANES), lambda b, i: (b, i, 0))],
      out_shape=[jax.ShapeDtypeStruct((bsz, s, d), F32),
                 jax.ShapeDtypeStruct((bsz, s, d), F32),
                 jax.ShapeDtypeStruct((bsz, TOP_K, s), I32),
                 jax.ShapeDtypeStruct((bsz, s, LANES), F32)],
      compiler_params=_params(("arbitrary", "arbitrary")),
      name="outproj_router",
  )(merged, x, w_out_bf16, g1, gain2, sc2, sh2, w_router_t, router_bias_col)


def _rank_kernel(eid_ref, triu_ref, rank_ref, size_ref, carry_scr):
  @pl.when((pl.program_id(0) == 0) & (pl.program_id(1) == 0))
  def _():
    carry_scr[...] = jnp.zeros_like(carry_scr)

  tn = eid_ref.shape[2]
  eid = eid_ref[0]
  expert = lax.broadcasted_iota(I32, (N_EXPERTS, tn), 0)
  hot = [(expert == eid[slot:slot + 1, :]).astype(F32) for slot in range(TOP_K)]
  both = hot[0] + hot[1]
  before = jnp.dot(both.astype(BF16), triu_ref[...], preferred_element_type=F32)
  carry = carry_scr[...]
  count = before + jnp.concatenate([carry] * (tn // LANES), axis=1)
  rank_ref[0] = jnp.concatenate(
      [jnp.sum(hot[slot] * count, axis=0, keepdims=True) for slot in range(TOP_K)],
      axis=0).astype(I32)
  total = carry + jnp.dot(both.astype(BF16), jnp.ones((tn, LANES), BF16),
                          preferred_element_type=F32)
  carry_scr[...] = total
  size_ref[...] = total.astype(I32)


def _rank(eid, triu_bf16):
  bsz, _, s = eid.shape
  tn = min(RANK_COLS, s)
  return pl.pallas_call(
      _rank_kernel,
      grid=(bsz, s // tn),
      in_specs=[pl.BlockSpec((1, TOP_K, tn), lambda b, i: (b, 0, i)),
                pl.BlockSpec((tn, tn), lambda b, i: (0, 0))],
      out_specs=[pl.BlockSpec((1, TOP_K, tn), lambda b, i: (b, 0, i)),
                 pl.BlockSpec((N_EXPERTS, LANES), lambda b, i: (0, 0))],
      out_shape=[jax.ShapeDtypeStruct((bsz, TOP_K, s), I32),
                 jax.ShapeDtypeStruct((N_EXPERTS, LANES), I32)],
      scratch_shapes=[pltpu.VMEM((N_EXPERTS, LANES), F32)],
      compiler_params=_params(("arbitrary", "arbitrary")),
      name="rank",
  )(eid, triu_bf16)


def _token_row(ref, group, sub):
  return ref.at[group, pl.ds(sub, 1), :]


def _table_row(ref, row):
  return ref.at[pl.ds(row, 1), :]


def _dest_kernel(eid_ref, rank_ref, start_ref, o_ref):
  tn = eid_ref.shape[2]
  expert = lax.broadcasted_iota(I32, (N_EXPERTS, tn), 0)
  rows = []
  for slot in range(TOP_K):
    hot = expert == eid_ref[0, slot:slot + 1, :]
    start = jnp.sum(jnp.where(hot, start_ref[...], 0.0), axis=0, keepdims=True)
    rows.append(start.astype(I32) + rank_ref[0, slot:slot + 1, :])
  o_ref[0] = jnp.concatenate(rows, axis=0)


def _dest(eid, rank, pad_start):
  bsz, _, s = eid.shape
  tn = min(RANK_COLS, s)
  blk = pl.BlockSpec((1, TOP_K, tn), lambda b, i: (b, 0, i))
  dest = pl.pallas_call(
      _dest_kernel,
      grid=(bsz, s // tn),
      in_specs=[blk, blk, pl.BlockSpec((N_EXPERTS, 1), lambda b, i: (0, 0))],
      out_specs=blk,
      out_shape=jax.ShapeDtypeStruct((bsz, TOP_K, s), I32),
      compiler_params=_params(("arbitrary", "arbitrary")),
      name="dest",
  )(eid, rank, pad_start.astype(F32).reshape(N_EXPERTS, 1))
  return jnp.transpose(dest, (0, 2, 1)).reshape(-1)


def _for_each_assignment(n_groups, fn):
  def body(j, carry):
    for u in range(SUBLANES):
      for slot in range(TOP_K):
        fn(j, u, slot, (j * SUBLANES + u) * TOP_K + slot)
    return carry

  lax.fori_loop(0, n_groups, body, 0)


def _scatter_kernel(dest_ref, h_ref, buf_in_ref, buf_ref, sem):
  del buf_in_ref
  src = h_ref.at[0]
  n_groups = src.shape[0]

  def copy(group, sub, row):
    return pltpu.make_async_copy(_token_row(src, group, sub), _table_row(buf_ref, row), sem)

  _for_each_assignment(n_groups, lambda j, u, slot, idx: copy(j, u, dest_ref[idx]).start())
  _for_each_assignment(n_groups, lambda j, u, slot, idx: copy(0, 0, 0).wait())


def _scatter(dest_flat, h2, n_rows):
  bsz, s, d = h2.shape
  tn = min(SCATTER_ROWS, s)
  n_tiles = s // tn
  return pl.pallas_call(
      _scatter_kernel,
      grid=(bsz, n_tiles),
      in_specs=[pl.BlockSpec((TOP_K * tn,), lambda b, i: (b * n_tiles + i,),
                             memory_space=pltpu.SMEM),
                pl.BlockSpec((1, tn // SUBLANES, SUBLANES, d), lambda b, i: (b, i, 0, 0)),
                pl.BlockSpec(memory_space=pl.ANY)],
      out_specs=pl.BlockSpec(memory_space=pl.ANY),
      scratch_shapes=[pltpu.SemaphoreType.DMA(())],
      out_shape=jax.ShapeDtypeStruct((n_rows, d), F32),
      input_output_aliases={2: 0},
      compiler_params=_params(("arbitrary", "arbitrary")),
      name="scatter_rows",
  )(dest_flat, h2.reshape(bsz, s // SUBLANES, SUBLANES, d), jnp.zeros((n_rows, d), F32))


def _expert_kernel(blk_e_ref, x_ref, win_ref, wout_ref, o_ref, win_scr, wout_scr):
  i = pl.program_id(0)
  changed = (i == 0) | (blk_e_ref[i] != blk_e_ref[jnp.maximum(i - 1, 0)])

  @pl.when(changed)
  def _():
    win_scr[...] = win_ref[0, 0].astype(BF16)
    wout_scr[...] = wout_ref[0, 0].astype(BF16)

  hidden = jnp.dot(x_ref[...].astype(BF16), win_scr[...], preferred_element_type=F32)
  half = hidden.shape[1] // 2
  gate, up = hidden[:, :half], hidden[:, half:]
  act = (gate * _sigmoid(gate)) * up
  o_ref[...] = jnp.dot(act.astype(BF16), wout_scr[...], preferred_element_type=F32)


def _experts(blk_e, buf, w_e_in, w_e_out, layer):
  n_rows, d = buf.shape
  d_hidden2 = w_e_in.shape[3]
  grid_spec = pltpu.PrefetchScalarGridSpec(
      num_scalar_prefetch=1,
      grid=(n_rows // M_BLK,),
      in_specs=[pl.BlockSpec((M_BLK, d), lambda i, be: (i, 0)),
                pl.BlockSpec((1, 1, d, d_hidden2), lambda i, be: (layer, be[i], 0, 0)),
                pl.BlockSpec((1, 1, d_hidden2 // 2, d), lambda i, be: (layer, be[i], 0, 0))],
      out_specs=pl.BlockSpec((M_BLK, d), lambda i, be: (i, 0)),
      scratch_shapes=[pltpu.VMEM((d, d_hidden2), BF16),
                      pltpu.VMEM((d_hidden2 // 2, d), BF16)],
  )
  return pl.pallas_call(
      _expert_kernel,
      grid_spec=grid_spec,
      out_shape=jax.ShapeDtypeStruct((n_rows, d), F32),
      compiler_params=_params(("arbitrary",)),
      name="experts",
  )(blk_e, buf, w_e_in, w_e_out)


def _combine_kernel(dest_ref, x_ref, wrow_ref, g2_ref, gain_ref, yb_ref,
                    o_ref, y_scr, sem, final_norm):
  tn, d = x_ref.shape[1], x_ref.shape[2]
  n_groups = y_scr.shape[1]

  def copy(row, slot, group, sub):
    return pltpu.make_async_copy(_table_row(yb_ref, row), _token_row(y_scr.at[slot], group, sub),
                                 sem)

  _for_each_assignment(n_groups, lambda j, u, slot, idx: copy(dest_ref[idx], slot, j, u).start())
  _for_each_assignment(n_groups, lambda j, u, slot, idx: copy(0, slot, 0, 0).wait())

  wrow = wrow_ref[0]
  moe = (wrow[:, 0:1] * y_scr[0].reshape(tn, d) + wrow[:, 1:2] * y_scr[1].reshape(tn, d))
  x = x_ref[0] + g2_ref[0] * moe
  if final_norm:
    x = (x * lax.rsqrt(jnp.mean(x * x, axis=-1, keepdims=True) + NORM_EPS)) * gain_ref[...]
  o_ref[0] = x


def _combine(dest_flat, x, wrow, g2, final_gain, yb, final_norm):
  bsz, s, d = x.shape
  tn = min(COMBINE_ROWS, s)
  n_tiles = s // tn
  row = pl.BlockSpec((1, tn, d), lambda b, i: (b, i, 0))
  return pl.pallas_call(
      functools.partial(_combine_kernel, final_norm=final_norm),
      grid=(bsz, n_tiles),
      in_specs=[pl.BlockSpec((TOP_K * tn,), lambda b, i: (b * n_tiles + i,),
                             memory_space=pltpu.SMEM),
                row,
                pl.BlockSpec((1, tn, LANES), lambda b, i: (b, i, 0)),
                pl.BlockSpec((1, 1, d), lambda b, i: (b, 0, 0)),
                pl.BlockSpec((1, d), lambda b, i: (0, 0)),
                pl.BlockSpec(memory_space=pl.ANY)],
      out_specs=row,
      scratch_shapes=[pltpu.VMEM((TOP_K, tn // SUBLANES, SUBLANES, d), F32),
                      pltpu.SemaphoreType.DMA(())],
      out_shape=jax.ShapeDtypeStruct((bsz, s, d), F32),
      compiler_params=_params(("arbitrary", "arbitrary")),
      name="combine",
  )(dest_flat, x, wrow, g2, final_gain.reshape(1, d), yb)


def _pad_rows(w, offset, total):
  return jnp.pad(w, ((offset, total - offset - w.shape[0]), (0, 0)))


def kernel(x, c, w_ada, b_ada, norm1_gain, norm2_gain, w_in, token_mu, conv_w, w0, decay_up,
           a0, aaa_up, gate_up, k_k, k_a, r_k, ln_x_w, ln_x_b, vmix_down, vmix_up, v0, w_out,
           w_router, router_bias, w_expert_in, w_expert_out, final_gain):
  bsz, s, d = x.shape
  depth = w_in.shape[0]
  n_tok = bsz * s
  n_rows = n_tok * TOP_K + N_EXPERTS * M_BLK

  ada = _ada(c, w_ada, b_ada)
  mods = ada.reshape(depth, bsz, 6, 1, d)

  lane = jnp.arange(LANES)
  bd_ones = (lane[:, None] // HEAD_DIM == lane[None, :] // HEAD_DIM).astype(BF16)
  t = jnp.arange(WKV_CHUNK)
  tri = (t[:, None] >= t[None, :]).astype(BF16)
  tc = jnp.arange(min(RANK_COLS, s))
  triu = (tc[:, None] < tc[None, :]).astype(BF16)

  perm = (jnp.arange(N_EXPERTS) % N_GROUPS) * EXPERTS_PER_GROUP + jnp.arange(N_EXPERTS) // N_GROUPS
  w_router_t = w_router.T[perm]
  router_bias_col = router_bias[perm].reshape(N_EXPERTS, 1)

  v_first = None
  for l in range(depth):
    sh1, sc1, g1, sh2, sc2, g2 = (mods[l, :, j] for j in range(6))
    has_vmix = l > 0
    vmix_down_pad = None
    if has_vmix:
      vmix_down_pad = jnp.zeros((d, LANES), BF16).at[:, :VMIX_RANK].set(
          vmix_down[l - 1].astype(BF16))
    p, hv = _inproj(x, norm1_gain[l], sc1, sh1, w_in[l].astype(BF16), vmix_down_pad)

    n_shift = token_mu.shape[1]
    mu3 = token_mu[l, :3 * d].reshape(3, d)
    mu_lr = token_mu[l, 3 * d:].reshape(1, n_shift - 3 * d)
    vecs = jnp.stack([w0[l], a0[l], k_k[l], k_a[l]])
    prep_out = _prep(
        p, mu3, mu_lr, conv_w[l], vecs,
        _pad_rows(decay_up[l], 0, LOW_RANK),
        _pad_rows(aaa_up[l], DECAY_RANK, LOW_RANK),
        _pad_rows(gate_up[l], DECAY_RANK + AAA_RANK, LOW_RANK),
        bd_ones,
        hv, v_first,
        _pad_rows(vmix_up[l - 1], 0, LANES) if has_vmix else None,
        v0[l - 1].reshape(1, d) if has_vmix else None)
    r_, k_, v_, lw, kk, beta, g, mconv, sgb = prep_out
    if not has_vmix:
      v_first = v_

    merged = _wkv(r_, k_, v_, lw, kk, beta, g, mconv, sgb, tri, bd_ones,
                  ln_x_w[l], ln_x_b[l], r_k[l])
    x, h2, eid, wrow = _outproj(merged, x, w_out[l].astype(BF16), g1, norm2_gain[l], sc2, sh2,
                                w_router_t, router_bias_col)

    rank, sizes = _rank(eid, triu)
    sizes = sizes[:, 0]
    padded = (sizes + M_BLK - 1) // M_BLK * M_BLK
    pad_end = jnp.cumsum(padded)
    pad_start = (pad_end - padded).astype(I32)
    blk_first_row = jnp.arange(n_rows // M_BLK, dtype=I32) * M_BLK
    blk_e = jnp.minimum(
        jnp.sum((pad_end[None, :] <= blk_first_row[:, None]).astype(I32), axis=1),
        N_EXPERTS - 1)

    dest_flat = _dest(eid, rank, pad_start)
    buf = _scatter(dest_flat, h2, n_rows)
    yb = _experts(blk_e, buf, w_expert_in, w_expert_out, l)
    x = _combine(dest_flat, x, wrow, g2, final_gain, yb, final_norm=(l == depth - 1))
  return x
```

```python
import functools
import itertools

import jax
import jax.numpy as jnp
from jax import lax
from jax.experimental import pallas as pl
from jax.experimental.pallas import tpu as pltpu

F32 = jnp.float32
BF16 = jnp.bfloat16
I32 = jnp.int32

HEAD_DIM = 64
CONV_WIDTH = 3
DECAY_RANK = 64
AAA_RANK = 64
GATE_RANK = 128
LOW_RANK = DECAY_RANK + AAA_RANK + GATE_RANK
VMIX_RANK = 32
N_EXPERTS = 32
N_GROUPS = 8
EXPERTS_PER_GROUP = N_EXPERTS // N_GROUPS
TOP_K = 2
M_BLK = 256
NORM_EPS = 1e-6
LN_X_EPS = 64e-5

LANES = 128
SUBLANES = 8
HEADS_PER_VREG = LANES // HEAD_DIM
VMEM_LIMIT_BYTES = 56 * 1024 * 1024

WKV_CHUNK = 64
WKV_SEQ_BLOCK = 256
WKV_PAIRS = 8
WKV_UNROLL = 2
WKV_STAGGER = 2
INPROJ_ROWS = 1024
INPROJ_COLS = 1408
PREP_ROWS = 128
OUTPROJ_ROWS = 256
RANK_COLS = 512
SCATTER_ROWS = 512
COMBINE_ROWS = 512


def _dot_bf16(a, b, dims=(((1,), (0,)), ((), ()))):
  return lax.dot_general(a.astype(BF16), b.astype(BF16), dims,
                         preferred_element_type=F32)


_NT = (((1,), (1,)), ((), ()))
_TN = (((0,), (0,)), ((), ()))


def _sigmoid(x):
  return 1.0 / (1.0 + jnp.exp(-x))


def _params(semantics):
  return pltpu.CompilerParams(dimension_semantics=semantics,
                              vmem_limit_bytes=VMEM_LIMIT_BYTES)


def _ada_kernel(c_ref, w_ref, b_ref, o_ref):
  c = c_ref[...]
  cond = c * _sigmoid(c)
  o_ref[0] = _dot_bf16(cond, w_ref[0]) + b_ref[0]


def _ada(c, w_ada, b_ada):
  depth, d, n = w_ada.shape
  bsz = c.shape[0]
  tn = d
  return pl.pallas_call(
      _ada_kernel,
      grid=(depth, n // tn),
      in_specs=[
          pl.BlockSpec((bsz, d), lambda l, j: (0, 0)),
          pl.BlockSpec((1, d, tn), lambda l, j: (l, 0, j)),
          pl.BlockSpec((1, 1, tn), lambda l, j: (l, 0, j)),
      ],
      out_specs=pl.BlockSpec((1, bsz, tn), lambda l, j: (l, 0, j)),
      out_shape=jax.ShapeDtypeStruct((depth, bsz, n), F32),
      compiler_params=_params(("arbitrary", "arbitrary")),
      name="ada",
  )(c, w_ada, b_ada.reshape(depth, 1, n))


def _modulated_norm(x, gain, scale, shift):
  y = x * lax.rsqrt(jnp.mean(x * x, axis=-1, keepdims=True) + NORM_EPS)
  return (y * gain) * (1.0 + scale) + shift


def _inproj_kernel(x_ref, gain_ref, sc_ref, sh_ref, w_ref, *rest, has_vmix):
  if has_vmix:
    wv_ref, p_ref, hv_ref, h_scr = rest
  else:
    p_ref, h_scr = rest

  @pl.when(pl.program_id(2) == 0)
  def _():
    h = _modulated_norm(x_ref[0], gain_ref[...], sc_ref[0], sh_ref[0])
    hb = h.astype(BF16)
    h_scr[...] = hb
    if has_vmix:
      hv_ref[0] = jnp.dot(hb, wv_ref[...], preferred_element_type=F32)

  p_ref[0] = jnp.dot(h_scr[...], w_ref[...], preferred_element_type=F32)


def _inproj(x, gain, sc, sh, w_bf16, vmix_down_bf16):
  bsz, s, d = x.shape
  n = w_bf16.shape[1]
  tm = min(INPROJ_ROWS, s)
  tn = INPROJ_COLS
  has_vmix = vmix_down_bf16 is not None
  in_specs = [
      pl.BlockSpec((1, tm, d), lambda b, i, j: (b, i, 0)),
      pl.BlockSpec((1, d), lambda b, i, j: (0, 0)),
      pl.BlockSpec((1, 1, d), lambda b, i, j: (b, 0, 0)),
      pl.BlockSpec((1, 1, d), lambda b, i, j: (b, 0, 0)),
      pl.BlockSpec((d, tn), lambda b, i, j: (0, j)),
  ]
  args = [x, gain.reshape(1, d), sc, sh, w_bf16]
  out_specs = [pl.BlockSpec((1, tm, tn), lambda b, i, j: (b, i, j))]
  out_shape = [jax.ShapeDtypeStruct((bsz, s, n), F32)]
  if has_vmix:
    in_specs.append(pl.BlockSpec((d, LANES), lambda b, i, j: (0, 0)))
    args.append(vmix_down_bf16)
    out_specs.append(pl.BlockSpec((1, tm, LANES), lambda b, i, j: (b, i, 0)))
    out_shape.append(jax.ShapeDtypeStruct((bsz, s, LANES), F32))
  outs = pl.pallas_call(
      functools.partial(_inproj_kernel, has_vmix=has_vmix),
      grid=(bsz, s // tm, n // tn),
      in_specs=in_specs,
      out_specs=out_specs,
      out_shape=out_shape,
      scratch_shapes=[pltpu.VMEM((tm, d), BF16)],
      compiler_params=_params(("arbitrary", "arbitrary", "arbitrary")),
      name="inproj",
  )(*args)
  return outs if has_vmix else (outs[0], None)


def _split_bf16(x, terms):
  parts = []
  for _ in range(terms - 1):
    part = x.astype(BF16)
    parts.append(part)
    x = x - part.astype(F32)
  parts.append(x.astype(BF16))
  return parts


def _head_sum(x, bd_ones, terms=2):
  rows = x.shape[0]
  out = []
  for j in range(x.shape[1] // LANES):
    parts = jnp.concatenate(_split_bf16(x[:, j * LANES:(j + 1) * LANES], terms), axis=0)
    s = jnp.dot(parts, bd_ones, preferred_element_type=F32)
    out.append(sum(s[i * rows:(i + 1) * rows] for i in range(terms)))
  return out[0] if len(out) == 1 else jnp.concatenate(out, axis=1)


def _shift_rows(cur, prev_tail, n):
  rolled = pltpu.roll(cur, n, axis=0)
  row = lax.broadcasted_iota(I32, cur.shape, 0)
  out = rolled
  for j in range(n):
    out = jnp.where(row == j, prev_tail[SUBLANES - n + j:SUBLANES - n + j + 1, :], out)
  return out


def _prep_kernel(cb_ref, cc_ref, ch_ref, ga_ref, gb_ref, r_ref, k_ref, v_ref, lr_ref,
                 mu_ref, mulr_ref, convw_ref, vec_ref, dup_ref, aup_ref, gup_ref, bd_ref,
                 *rest, has_vmix):
  if has_vmix:
    hv_ref, vfirst_ref, vup_ref, v0_ref = rest[:4]
    rest = rest[4:]
  (ro_ref, ko_ref, vo_ref, lw_ref, kk_ref, beta_ref, g_ref, mconv_ref, sgb_ref,
   pr_scr, pk_scr, pv_scr, plr_scr, pu_scr) = rest

  @pl.when(pl.program_id(1) == 0)
  def _():
    pr_scr[...] = jnp.zeros_like(pr_scr)
    pk_scr[...] = jnp.zeros_like(pk_scr)
    pv_scr[...] = jnp.zeros_like(pv_scr)
    plr_scr[...] = jnp.zeros_like(plr_scr)
    pu_scr[...] = jnp.zeros_like(pu_scr)

  rows = r_ref.shape[1]
  tail = slice(rows - SUBLANES, rows)

  def token_shift(ref, scr, mu):
    cur = ref[0]
    prev = _shift_rows(cur, scr[...], 1)
    scr[...] = cur[tail, :]
    return cur + (prev - cur) * mu

  r = token_shift(r_ref, pr_scr, mu_ref[0:1, :])
  k = token_shift(k_ref, pk_scr, mu_ref[1:2, :])
  v = token_shift(v_ref, pv_scr, mu_ref[2:3, :])
  lr = token_shift(lr_ref, plr_scr, mulr_ref[...])

  w0 = vec_ref[0:1, :]
  a0 = vec_ref[1:2, :]
  k_k = vec_ref[2:3, :]
  k_a = vec_ref[3:4, :]

  w_pre = w0 + _dot_bf16(jnp.tanh(lr), dup_ref[...])
  z = -w_pre
  softplus = jnp.maximum(z, 0.0) + jnp.log(1.0 + jnp.exp(-jnp.abs(z)))
  lw_ref[0] = -jnp.exp(-softplus - 0.5)
  a = _sigmoid(a0 + _dot_bf16(lr, aup_ref[...]))
  g_ref[0] = _dot_bf16(_sigmoid(lr), gup_ref[...])
  if has_vmix:
    mix = _sigmoid(v0_ref[...] + _dot_bf16(hv_ref[0], vup_ref[...]))
    v = v + (vfirst_ref[0] - v) * mix
  kk = k * k_k
  norm = jnp.sqrt(_head_sum(kk * kk, bd_ref[...]))
  kk = kk / jnp.maximum(norm, 1e-12)
  ro_ref[0] = r
  ko_ref[0] = k * (1.0 + (a - 1.0) * k_a)
  vo_ref[0] = v
  kk_ref[0] = kk
  beta_ref[0] = a * kk

  u = cc_ref[0] * ch_ref[0]
  prev_u = pu_scr[...]
  conv = (convw_ref[0:1, :] * _shift_rows(u, prev_u, 2)
          + convw_ref[1:2, :] * _shift_rows(u, prev_u, 1)
          + convw_ref[2:3, :] * u)
  pu_scr[...] = u[tail, :]
  mconv_ref[0] = _sigmoid(ga_ref[0]) * (cb_ref[0] * conv)
  sgb_ref[0] = _sigmoid(gb_ref[0])


def _prep(p, mu3, mu_lr, conv_w, vecs, decay_up_pad, aaa_up_pad, gate_up_pad, bd_ones,
          hv, v_first, vmix_up_pad, v0):
  bsz, s, _ = p.shape
  d = conv_w.shape[1]
  ts = min(PREP_ROWS, s)
  has_vmix = hv is not None
  col = lambda c: pl.BlockSpec((1, ts, d), lambda b, i, c=c: (b, i, c))
  full = lambda a: pl.BlockSpec(a.shape, lambda b, i: (0,) * a.ndim)
  in_specs = [col(c) for c in range(8)]
  in_specs.append(pl.BlockSpec((1, ts, LOW_RANK), lambda b, i: (b, i, 8 * d // LOW_RANK)))
  consts = [mu3, mu_lr, conv_w, vecs, decay_up_pad, aaa_up_pad, gate_up_pad, bd_ones]
  in_specs += [full(a) for a in consts]
  args = [p] * 9 + consts
  if has_vmix:
    in_specs += [pl.BlockSpec((1, ts, LANES), lambda b, i: (b, i, 0)),
                 pl.BlockSpec((1, ts, d), lambda b, i: (b, i, 0)),
                 full(vmix_up_pad), full(v0)]
    args += [hv, v_first, vmix_up_pad, v0]
  out_spec = pl.BlockSpec((1, ts, d), lambda b, i: (b, i, 0))
  out_sds = jax.ShapeDtypeStruct((bsz, s, d), F32)
  return pl.pallas_call(
      functools.partial(_prep_kernel, has_vmix=has_vmix),
      grid=(bsz, s // ts),
      in_specs=in_specs,
      out_specs=[out_spec] * 9,
      out_shape=[out_sds] * 9,
      scratch_shapes=[pltpu.VMEM((SUBLANES, d), F32)] * 3
      + [pltpu.VMEM((SUBLANES, LOW_RANK), F32), pltpu.VMEM((SUBLANES, d), F32)],
      compiler_params=_params(("arbitrary", "arbitrary")),
      name="prep",
  )(*args)


def _wkv_kernel(r_ref, k_ref, v_ref, lw_ref, kk_ref, beta_ref, g_ref, mconv_ref, sgb_ref,
                tri_ref, bd_ref, lnw_ref, lnb_ref, rk_ref, o_ref, s_scr):
  assert WKV_CHUNK == HEAD_DIM and HEADS_PER_VREG * WKV_CHUNK == LANES
  c = WKV_CHUNK
  n_chunks = r_ref.shape[1] // c
  n_pairs = r_ref.shape[2] // LANES

  @pl.when(pl.program_id(2) == 0)
  def _():
    s_scr[...] = jnp.zeros_like(s_scr)

  lane = lax.broadcasted_iota(I32, (c, LANES), 1)
  head_masks = [(lane // HEAD_DIM == h).astype(BF16) for h in range(HEADS_PER_VREG)]
  t_row = lax.broadcasted_iota(I32, (c, LANES), 0)
  t_col = lane % c
  strict = t_row > t_col
  incl = t_row >= t_col
  eye = (t_row == t_col).astype(F32)
  same_head = (lax.broadcasted_iota(I32, (LANES, LANES), 0) // HEAD_DIM
               == lax.broadcasted_iota(I32, (LANES, LANES), 1) // HEAD_DIM)

  def stack(xb):
    return jnp.concatenate([xb * m for m in head_masks], axis=0)

  def mm(a, b, dims=(((1,), (0,)), ((), ()))):
    return lax.dot_general(a, b, dims, preferred_element_type=F32)

  tri = tri_ref[...]
  bd_ones = bd_ref[...]
  doubling_steps = c.bit_length() - 2

  def pair_chunk(p, sl, cum, delay):
    for _ in range(delay):
      yield
    ln = slice(p * LANES, (p + 1) * LANES)
    r = r_ref[0, sl, ln]
    k = k_ref[0, sl, ln]
    v = v_ref[0, sl, ln]
    lw = lw_ref[0, sl, ln]
    kk = kk_ref[0, sl, ln]
    beta = beta_ref[0, sl, ln]

    cum_last = cum[c - 1:c, :]
    inv_p = jnp.exp(-cum)
    rem_p = jnp.exp(cum_last - cum)
    rt = (r * jnp.exp(cum)).astype(BF16)
    at = (kk * jnp.exp(cum - lw)).astype(BF16)
    kh_s = stack((k * inv_p).astype(BF16))
    bh_s = stack((beta * inv_p).astype(BF16))
    kb = (k * rem_p).astype(BF16)
    bb_neg = (-(beta * rem_p)).astype(BF16)
    vb = v.astype(BF16)
    v_s = stack(vb)
    yield

    sc = mm(jnp.concatenate([at, rt], axis=0), jnp.concatenate([kh_s, bh_s], axis=0), _NT)
    m_k = jnp.where(strict, sc[:c, :LANES], 0.0).astype(BF16)
    q_pow = jnp.where(strict, -sc[:c, LANES:], 0.0)
    q_k = jnp.where(incl, sc[c:, :LANES], 0.0).astype(BF16)
    q_b = jnp.where(incl, sc[c:, LANES:], 0.0).astype(BF16)
    yield

    w_inv = eye + q_pow
    q_pow = q_pow.astype(BF16)
    kv = mm(jnp.concatenate([m_k, q_k], axis=0), v_s)
    q_pow = mm(q_pow, stack(q_pow)).astype(BF16)
    yield
    for i in range(doubling_steps):
      w_s = stack(w_inv.astype(BF16))
      if i < doubling_steps - 1:
        prod = mm(q_pow, jnp.concatenate([w_s, stack(q_pow)], axis=1))
        q_pow = prod[:, LANES:].astype(BF16)
      else:
        prod = mm(q_pow, w_s)
      w_inv = w_inv + prod[:, :LANES]
      yield

    wa = mm(w_inv.astype(BF16),
            jnp.concatenate([stack(at), stack(kv[:c].astype(BF16))], axis=1))
    a2 = wa[:, :LANES].astype(BF16)
    v2 = wa[:, LANES:]
    yield

    state = s_scr[p]
    res = mm(jnp.concatenate([a2, rt], axis=0), state.astype(BF16), _NT)
    u = res[:c] + v2
    ub = u.astype(BF16)
    yield
    y = res[c:] + kv[c:] - mm(q_b, stack(ub))
    update = mm(jnp.concatenate([vb, ub], axis=0), jnp.concatenate([kb, bb_neg], axis=0), _TN)
    s_scr[p] = state * jnp.exp(cum_last) + jnp.where(same_head, update, 0.0)
    yield

    inv_n = 1.0 / HEAD_DIM
    sums = _head_sum(jnp.concatenate([y, r * k * rk_ref[:, ln]], axis=0), bd_ones, terms=1)
    yield
    yc = y - sums[:c] * inv_n
    var = _head_sum(yc * yc, bd_ones, terms=1) * inv_n
    yield
    yn = yc * lax.rsqrt(var + LN_X_EPS) * lnw_ref[:, ln] + lnb_ref[:, ln]
    y_rwkv = (yn + sums[c:] * v) * g_ref[0, sl, ln]
    o_ref[0, sl, ln] = mconv_ref[0, sl, ln] + sgb_ref[0, sl, ln] * y_rwkv

  def cumulative_log_decay(sl):
    lw_terms = jnp.concatenate(_split_bf16(lw_ref[0, sl, :], 3), axis=1)
    cs = jnp.dot(tri, lw_terms, preferred_element_type=F32)
    width = n_pairs * LANES
    return cs[:, :width] + cs[:, width:2 * width] + cs[:, 2 * width:]

  def chunk_body(ci, carry):
    chains = []
    for j in range(WKV_UNROLL):
      sl = pl.ds(pl.multiple_of((ci * WKV_UNROLL + j) * c, c), c)
      cum = cumulative_log_decay(sl)
      chains += [pair_chunk(p, sl, cum[:, p * LANES:(p + 1) * LANES], j * WKV_STAGGER)
                 for p in range(n_pairs)]
    for _ in itertools.zip_longest(*chains):
      pass
    return carry

  lax.fori_loop(0, n_chunks // WKV_UNROLL, chunk_body, 0)


def _wkv(r, k, v, lw, kk, beta, g, mconv, sgb, tri, bd_ones, ln_w, ln_b, r_k):
  bsz, s, d = r.shape
  sb = min(WKV_SEQ_BLOCK, s)
  width = WKV_PAIRS * LANES
  seq = pl.BlockSpec((1, sb, width), lambda b, hp, i: (b, i, hp))
  vec = pl.BlockSpec((1, width), lambda b, hp, i: (0, hp))
  full = lambda a: pl.BlockSpec(a.shape, lambda b, hp, i: (0,) * a.ndim)
  return pl.pallas_call(
      _wkv_kernel,
      grid=(bsz, d // width, s // sb),
      in_specs=[seq] * 9 + [full(tri), full(bd_ones), vec, vec, vec],
      out_specs=seq,
      out_shape=jax.ShapeDtypeStruct((bsz, s, d), F32),
      scratch_shapes=[pltpu.VMEM((WKV_PAIRS, LANES, LANES), F32)],
      compiler_params=_params(("arbitrary", "arbitrary", "arbitrary")),
      name="wkv",
  )(r, k, v, lw, kk, beta, g, mconv, sgb, tri, bd_ones,
    ln_w.reshape(1, d), ln_b.reshape(1, d), r_k.reshape(1, d))


def _outproj_kernel(m_ref, x_ref, w_ref, g1_ref, gain_ref, sc_ref, sh_ref, wr_ref, rb_ref,
                    xo_ref, h2_ref, eid_ref, wrow_ref):
  mix = jnp.dot(m_ref[0].astype(BF16), w_ref[...], preferred_element_type=F32)
  x = x_ref[0] + g1_ref[0] * mix
  xo_ref[0] = x
  h2 = _modulated_norm(x, gain_ref[...], sc_ref[0], sh_ref[0])
  h2_ref[0] = h2

  logits = _dot_bf16(wr_ref[...], h2, _NT)
  tm = logits.shape[1]
  epg, ng = EXPERTS_PER_GROUP, N_GROUPS
  aff = [_sigmoid(logits[j * ng:(j + 1) * ng, :]) for j in range(epg)]
  sel = [aff[j] + rb_ref[j * ng:(j + 1) * ng, :] for j in range(epg)]
  hi01, lo01 = jnp.maximum(sel[0], sel[1]), jnp.minimum(sel[0], sel[1])
  hi23, lo23 = jnp.maximum(sel[2], sel[3]), jnp.minimum(sel[2], sel[3])
  gscore = jnp.maximum(hi01, hi23) + jnp.maximum(jnp.minimum(hi01, hi23),
                                                 jnp.maximum(lo01, lo23))
  gid = lax.broadcasted_iota(I32, (ng, tm), 0)
  gmax = jnp.max(gscore, axis=0, keepdims=True)
  gsel = jnp.min(jnp.where(gscore == gmax, gid, ng), axis=0, keepdims=True)
  in_g = gid == gsel
  cand = [jnp.sum(jnp.where(in_g, sel[j], 0.0), axis=0, keepdims=True) for j in range(epg)]
  affc = [jnp.sum(jnp.where(in_g, aff[j], 0.0), axis=0, keepdims=True) for j in range(epg)]

  def first_argmax(vals):
    best = vals[0]
    for val in vals[1:]:
      best = jnp.maximum(best, val)
    idx = jnp.full(best.shape, epg, I32)
    for j in reversed(range(epg)):
      idx = jnp.where(vals[j] == best, j, idx)
    return idx

  i1 = first_argmax(cand)
  i2 = first_argmax([jnp.where(i1 == j, -jnp.inf, cand[j]) for j in range(epg)])
  pick = lambda idx: sum(jnp.where(idx == j, affc[j], 0.0) for j in range(epg))
  a1, a2 = pick(i1), pick(i2)
  denom = a1 + a2
  eid_ref[0] = jnp.concatenate([gsel * epg + i1, gsel * epg + i2], axis=0)
  wpad = jnp.concatenate([a1 / denom, a2 / denom, jnp.zeros((LANES - TOP_K, tm), F32)], axis=0)
  wrow_ref[0] = wpad.T


def _outproj(merged, x, w_out_bf16, g1, gain2, sc2, sh2, w_router_t, router_bias_col):
  bsz, s, d = x.shape
  tm = min(OUTPROJ_ROWS, s)
  row = pl.BlockSpec((1, tm, d), lambda b, i: (b, i, 0))
  mod = pl.BlockSpec((1, 1, d), lambda b, i: (b, 0, 0))
  full = lambda a: pl.BlockSpec(a.shape, lambda b, i: (0,) * a.ndim)
  gain2 = gain2.reshape(1, d)
  return pl.pallas_call(
      _outproj_kernel,
      grid=(bsz, s // tm),
      in_specs=[row, row, full(w_out_bf16), mod, full(gain2), mod, mod,
                full(w_router_t), full(router_bias_col)],
      out_specs=[row, row,
                 pl.BlockSpec((1, TOP_K, tm), lambda b, i: (b, 0, i)),
                 pl.BlockSpec((1, tm, LANES), lambda b, i: (b, i, 0))],
      out_shape=[jax.ShapeDtypeStruct((bsz, s, d), F32),
                 jax.ShapeDtypeStruct((bsz, s, d), F32),
                 jax.ShapeDtypeStruct((bsz, TOP_K, s), I32),
                 jax.ShapeDtypeStruct((bsz, s, LANES), F32)],
      compiler_params=_params(("arbitrary", "arbitrary")),
      name="outproj_router",
  )(merged, x, w_out_bf16, g1, gain2, sc2, sh2, w_router_t, router_bias_col)


def _rank_kernel(eid_ref, triu_ref, rank_ref, size_ref, carry_scr):
  @pl.when((pl.program_id(0) == 0) & (pl.program_id(1) == 0))
  def _():
    carry_scr[...] = jnp.zeros_like(carry_scr)

  tn = eid_ref.shape[2]
  eid = eid_ref[0]
  expert = lax.broadcasted_iota(I32, (N_EXPERTS, tn), 0)
  hot = [(expert == eid[slot:slot + 1, :]).astype(F32) for slot in range(TOP_K)]
  both = hot[0] + hot[1]
  before = jnp.dot(both.astype(BF16), triu_ref[...], preferred_element_type=F32)
  carry = carry_scr[...]
  count = before + jnp.concatenate([carry] * (tn // LANES), axis=1)
  rank_ref[0] = jnp.concatenate(
      [jnp.sum(hot[slot] * count, axis=0, keepdims=True) for slot in range(TOP_K)],
      axis=0).astype(I32)
  total = carry + jnp.dot(both.astype(BF16), jnp.ones((tn, LANES), BF16),
                          preferred_element_type=F32)
  carry_scr[...] = total
  size_ref[...] = total.astype(I32)


def _rank(eid, triu_bf16):
  bsz, _, s = eid.shape
  tn = min(RANK_COLS, s)
  return pl.pallas_call(
      _rank_kernel,
      grid=(bsz, s // tn),
      in_specs=[pl.BlockSpec((1, TOP_K, tn), lambda b, i: (b, 0, i)),
                pl.BlockSpec((tn, tn), lambda b, i: (0, 0))],
      out_specs=[pl.BlockSpec((1, TOP_K, tn), lambda b, i: (b, 0, i)),
                 pl.BlockSpec((N_EXPERTS, LANES), lambda b, i: (0, 0))],
      out_shape=[jax.ShapeDtypeStruct((bsz, TOP_K, s), I32),
                 jax.ShapeDtypeStruct((N_EXPERTS, LANES), I32)],
      scratch_shapes=[pltpu.VMEM((N_EXPERTS, LANES), F32)],
      compiler_params=_params(("arbitrary", "arbitrary")),
      name="rank",
  )(eid, triu_bf16)


def _token_row(ref, group, sub):
  return ref.at[group, pl.ds(sub, 1), :]


def _table_row(ref, row):
  return ref.at[pl.ds(row, 1), :]


def _dest_kernel(eid_ref, rank_ref, start_ref, o_ref):
  tn = eid_ref.shape[2]
  expert = lax.broadcasted_iota(I32, (N_EXPERTS, tn), 0)
  rows = []
  for slot in range(TOP_K):
    hot = expert == eid_ref[0, slot:slot + 1, :]
    start = jnp.sum(jnp.where(hot, start_ref[...], 0.0), axis=0, keepdims=True)
    rows.append(start.astype(I32) + rank_ref[0, slot:slot + 1, :])
  o_ref[0] = jnp.concatenate(rows, axis=0)


def _dest(eid, rank, pad_start):
  bsz, _, s = eid.shape
  tn = min(RANK_COLS, s)
  blk = pl.BlockSpec((1, TOP_K, tn), lambda b, i: (b, 0, i))
  dest = pl.pallas_call(
      _dest_kernel,
      grid=(bsz, s // tn),
      in_specs=[blk, blk, pl.BlockSpec((N_EXPERTS, 1), lambda b, i: (0, 0))],
      out_specs=blk,
      out_shape=jax.ShapeDtypeStruct((bsz, TOP_K, s), I32),
      compiler_params=_params(("arbitrary", "arbitrary")),
      name="dest",
  )(eid, rank, pad_start.astype(F32).reshape(N_EXPERTS, 1))
  return jnp.transpose(dest, (0, 2, 1)).reshape(-1)


def _for_each_assignment(n_groups, fn):
  def body(j, carry):
    for u in range(SUBLANES):
      for slot in range(TOP_K):
        fn(j, u, slot, (j * SUBLANES + u) * TOP_K + slot)
    return carry

  lax.fori_loop(0, n_groups, body, 0)


def _scatter_kernel(dest_ref, h_ref, buf_in_ref, buf_ref, sem):
  del buf_in_ref
  src = h_ref.at[0]
  n_groups = src.shape[0]

  def copy(group, sub, row):
    return pltpu.make_async_copy(_token_row(src, group, sub), _table_row(buf_ref, row), sem)

  _for_each_assignment(n_groups, lambda j, u, slot, idx: copy(j, u, dest_ref[idx]).start())
  _for_each_assignment(n_groups, lambda j, u, slot, idx: copy(0, 0, 0).wait())


def _scatter(dest_flat, h2, n_rows):
  bsz, s, d = h2.shape
  tn = min(SCATTER_ROWS, s)
  n_tiles = s // tn
  return pl.pallas_call(
      _scatter_kernel,
      grid=(bsz, n_tiles),
      in_specs=[pl.BlockSpec((TOP_K * tn,), lambda b, i: (b * n_tiles + i,),
                             memory_space=pltpu.SMEM),
                pl.BlockSpec((1, tn // SUBLANES, SUBLANES, d), lambda b, i: (b, i, 0, 0)),
                pl.BlockSpec(memory_space=pl.ANY)],
      out_specs=pl.BlockSpec(memory_space=pl.ANY),
      scratch_shapes=[pltpu.SemaphoreType.DMA(())],
      out_shape=jax.ShapeDtypeStruct((n_rows, d), F32),
      input_output_aliases={2: 0},
      compiler_params=_params(("arbitrary", "arbitrary")),
      name="scatter_rows",
  )(dest_flat, h2.reshape(bsz, s // SUBLANES, SUBLANES, d), jnp.zeros((n_rows, d), F32))


def _expert_kernel(blk_e_ref, x_ref, win_ref, wout_ref, o_ref, win_scr, wout_scr):
  i = pl.program_id(0)
  changed = (i == 0) | (blk_e_ref[i] != blk_e_ref[jnp.maximum(i - 1, 0)])

  @pl.when(changed)
  def _():
    win_scr[...] = win_ref[0, 0].astype(BF16)
    wout_scr[...] = wout_ref[0, 0].astype(BF16)

  hidden = jnp.dot(x_ref[...].astype(BF16), win_scr[...], preferred_element_type=F32)
  half = hidden.shape[1] // 2
  gate, up = hidden[:, :half], hidden[:, half:]
  act = (gate * _sigmoid(gate)) * up
  o_ref[...] = jnp.dot(act.astype(BF16), wout_scr[...], preferred_element_type=F32)


def _experts(blk_e, buf, w_e_in, w_e_out, layer):
  n_rows, d = buf.shape
  d_hidden2 = w_e_in.shape[3]
  grid_spec = pltpu.PrefetchScalarGridSpec(
      num_scalar_prefetch=1,
      grid=(n_rows // M_BLK,),
      in_specs=[pl.BlockSpec((M_BLK, d), lambda i, be: (i, 0)),
                pl.BlockSpec((1, 1, d, d_hidden2), lambda i, be: (layer, be[i], 0, 0)),
                pl.BlockSpec((1, 1, d_hidden2 // 2, d), lambda i, be: (layer, be[i], 0, 0))],
      out_specs=pl.BlockSpec((M_BLK, d), lambda i, be: (i, 0)),
      scratch_shapes=[pltpu.VMEM((d, d_hidden2), BF16),
                      pltpu.VMEM((d_hidden2 // 2, d), BF16)],
  )
  return pl.pallas_call(
      _expert_kernel,
      grid_spec=grid_spec,
      out_shape=jax.ShapeDtypeStruct((n_rows, d), F32),
      compiler_params=_params(("arbitrary",)),
      name="experts",
  )(blk_e, buf, w_e_in, w_e_out)


def _combine_kernel(dest_ref, x_ref, wrow_ref, g2_ref, gain_ref, yb_ref,
                    o_ref, y_scr, sem, final_norm):
  tn, d = x_ref.shape[1], x_ref.shape[2]
  n_groups = y_scr.shape[1]

  def copy(row, slot, group, sub):
    return pltpu.make_async_copy(_table_row(yb_ref, row), _token_row(y_scr.at[slot], group, sub),
                                 sem)

  _for_each_assignment(n_groups, lambda j, u, slot, idx: copy(dest_ref[idx], slot, j, u).start())
  _for_each_assignment(n_groups, lambda j, u, slot, idx: copy(0, slot, 0, 0).wait())

  wrow = wrow_ref[0]
  moe = (wrow[:, 0:1] * y_scr[0].reshape(tn, d) + wrow[:, 1:2] * y_scr[1].reshape(tn, d))
  x = x_ref[0] + g2_ref[0] * moe
  if final_norm:
    x = (x * lax.rsqrt(jnp.mean(x * x, axis=-1, keepdims=True) + NORM_EPS)) * gain_ref[...]
  o_ref[0] = x


def _combine(dest_flat, x, wrow, g2, final_gain, yb, final_norm):
  bsz, s, d = x.shape
  tn = min(COMBINE_ROWS, s)
  n_tiles = s // tn
  row = pl.BlockSpec((1, tn, d), lambda b, i: (b, i, 0))
  return pl.pallas_call(
      functools.partial(_combine_kernel, final_norm=final_norm),
      grid=(bsz, n_tiles),
      in_specs=[pl.BlockSpec((TOP_K * tn,), lambda b, i: (b * n_tiles + i,),
                             memory_space=pltpu.SMEM),
                row,
                pl.BlockSpec((1, tn, LANES), lambda b, i: (b, i, 0)),
                pl.BlockSpec((1, 1, d), lambda b, i: (b, 0, 0)),
                pl.BlockSpec((1, d), lambda b, i: (0, 0)),
                pl.BlockSpec(memory_space=pl.ANY)],
      out_specs=row,
      scratch_shapes=[pltpu.VMEM((TOP_K, tn // SUBLANES, SUBLANES, d), F32),
                      pltpu.SemaphoreType.DMA(())],
      out_shape=jax.ShapeDtypeStruct((bsz, s, d), F32),
      compiler_params=_params(("arbitrary", "arbitrary")),
      name="combine",
  )(dest_flat, x, wrow, g2, final_gain.reshape(1, d), yb)


def _pad_rows(w, offset, total):
  return jnp.pad(w.astype(BF16), ((offset, total - offset - w.shape[0]), (0, 0)))


def kernel(x, c, w_ada, b_ada, norm1_gain, norm2_gain, w_in, token_mu, conv_w, w0, decay_up,
           a0, aaa_up, gate_up, k_k, k_a, r_k, ln_x_w, ln_x_b, vmix_down, vmix_up, v0, w_out,
           w_router, router_bias, w_expert_in, w_expert_out, final_gain):
  bsz, s, d = x.shape
  depth = w_in.shape[0]
  n_tok = bsz * s
  n_rows = n_tok * TOP_K + N_EXPERTS * M_BLK

  ada = _ada(c, w_ada, b_ada)
  mods = ada.reshape(depth, bsz, 6, 1, d)

  lane = jnp.arange(LANES)
  bd_ones = (lane[:, None] // HEAD_DIM == lane[None, :] // HEAD_DIM).astype(BF16)
  t = jnp.arange(WKV_CHUNK)
  tri = (t[:, None] >= t[None, :]).astype(BF16)
  tc = jnp.arange(min(RANK_COLS, s))
  triu = (tc[:, None] < tc[None, :]).astype(BF16)

  perm = (jnp.arange(N_EXPERTS) % N_GROUPS) * EXPERTS_PER_GROUP + jnp.arange(N_EXPERTS) // N_GROUPS
  w_router_t = w_router.T[perm].astype(BF16)
  router_bias_col = router_bias[perm].reshape(N_EXPERTS, 1)

  v_first = None
  for l in range(depth):
    sh1, sc1, g1, sh2, sc2, g2 = (mods[l, :, j] for j in range(6))
    has_vmix = l > 0
    vmix_down_pad = None
    if has_vmix:
      vmix_down_pad = jnp.zeros((d, LANES), BF16).at[:, :VMIX_RANK].set(
          vmix_down[l - 1].astype(BF16))
    p, hv = _inproj(x, norm1_gain[l], sc1, sh1, w_in[l].astype(BF16), vmix_down_pad)

    n_shift = token_mu.shape[1]
    mu3 = token_mu[l, :3 * d].reshape(3, d)
    mu_lr = token_mu[l, 3 * d:].reshape(1, n_shift - 3 * d)
    vecs = jnp.stack([w0[l], a0[l], k_k[l], k_a[l]])
    prep_out = _prep(
        p, mu3, mu_lr, conv_w[l], vecs,
        _pad_rows(decay_up[l], 0, LOW_RANK),
        _pad_rows(aaa_up[l], DECAY_RANK, LOW_RANK),
        _pad_rows(gate_up[l], DECAY_RANK + AAA_RANK, LOW_RANK),
        bd_ones,
        hv, v_first,
        _pad_rows(vmix_up[l - 1], 0, LANES) if has_vmix else None,
        v0[l - 1].reshape(1, d) if has_vmix else None)
    r_, k_, v_, lw, kk, beta, g, mconv, sgb = prep_out
    if not has_vmix:
      v_first = v_

    merged = _wkv(r_, k_, v_, lw, kk, beta, g, mconv, sgb, tri, bd_ones,
                  ln_x_w[l], ln_x_b[l], r_k[l])
    x, h2, eid, wrow = _outproj(merged, x, w_out[l].astype(BF16), g1, norm2_gain[l], sc2, sh2,
                                w_router_t, router_bias_col)

    rank, sizes = _rank(eid, triu)
    sizes = sizes[:, 0]
    padded = (sizes + M_BLK - 1) // M_BLK * M_BLK
    pad_end = jnp.cumsum(padded)
    pad_start = (pad_end - padded).astype(I32)
    blk_first_row = jnp.arange(n_rows // M_BLK, dtype=I32) * M_BLK
    blk_e = jnp.minimum(
        jnp.sum((pad_end[None, :] <= blk_first_row[:, None]).astype(I32), axis=1),
        N_EXPERTS - 1)

    dest_flat = _dest(eid, rank, pad_start)
    buf = _scatter(dest_flat, h2, n_rows)
    yb = _experts(blk_e, buf, w_expert_in, w_expert_out, l)
    x = _combine(dest_flat, x, wrow, g2, final_gain, yb, final_norm=(l == depth - 1))
  return x
```

```python
import functools
import itertools

import jax
import jax.numpy as jnp
from jax import lax
from jax.experimental import pallas as pl
from jax.experimental.pallas import tpu as pltpu

F32 = jnp.float32
BF16 = jnp.bfloat16
I32 = jnp.int32

HEAD_DIM = 64
CONV_WIDTH = 3
DECAY_RANK = 64
AAA_RANK = 64
GATE_RANK = 128
LOW_RANK = DECAY_RANK + AAA_RANK + GATE_RANK
VMIX_RANK = 32
N_EXPERTS = 32
N_GROUPS = 8
EXPERTS_PER_GROUP = N_EXPERTS // N_GROUPS
TOP_K = 2
M_BLK = 256
NORM_EPS = 1e-6
LN_X_EPS = 64e-5

LANES = 128
SUBLANES = 8
HEADS_PER_VREG = LANES // HEAD_DIM
VMEM_LIMIT_BYTES = 56 * 1024 * 1024

WKV_CHUNK = 64
WKV_STAGGER = 2
MIXER_TILE = 128
INPROJ_ROWS = 1024
INPROJ_COLS = 1408
OUTPROJ_ROWS = 256
RANK_COLS = 512
SCATTER_ROWS = 512
COMBINE_ROWS = 512


def _dot_bf16(a, b, dims=(((1,), (0,)), ((), ()))):
  return lax.dot_general(a.astype(BF16), b.astype(BF16), dims,
                         preferred_element_type=F32)


_NT = (((1,), (1,)), ((), ()))
_TN = (((0,), (0,)), ((), ()))


def _sigmoid(x):
  return 1.0 / (1.0 + jnp.exp(-x))


def _params(semantics):
  return pltpu.CompilerParams(dimension_semantics=semantics,
                              vmem_limit_bytes=VMEM_LIMIT_BYTES)


def _ada_kernel(c_ref, w_ref, b_ref, o_ref):
  c = c_ref[...]
  cond = c * _sigmoid(c)
  o_ref[0] = _dot_bf16(cond, w_ref[0]) + b_ref[0]


def _ada(c, w_ada, b_ada):
  depth, d, n = w_ada.shape
  bsz = c.shape[0]
  tn = d
  return pl.pallas_call(
      _ada_kernel,
      grid=(depth, n // tn),
      in_specs=[
          pl.BlockSpec((bsz, d), lambda l, j: (0, 0)),
          pl.BlockSpec((1, d, tn), lambda l, j: (l, 0, j)),
          pl.BlockSpec((1, 1, tn), lambda l, j: (l, 0, j)),
      ],
      out_specs=pl.BlockSpec((1, bsz, tn), lambda l, j: (l, 0, j)),
      out_shape=jax.ShapeDtypeStruct((depth, bsz, n), F32),
      compiler_params=_params(("arbitrary", "arbitrary")),
      name="ada",
  )(c, w_ada, b_ada.reshape(depth, 1, n))


def _modulated_norm(x, gain, scale, shift):
  y = x * lax.rsqrt(jnp.mean(x * x, axis=-1, keepdims=True) + NORM_EPS)
  return (y * gain) * (1.0 + scale) + shift


def _inproj_kernel(x_ref, gain_ref, sc_ref, sh_ref, w_ref, *rest, has_vmix):
  if has_vmix:
    wv_ref, p_ref, hv_ref, h_scr = rest
  else:
    p_ref, h_scr = rest

  @pl.when(pl.program_id(2) == 0)
  def _():
    h = _modulated_norm(x_ref[0], gain_ref[...], sc_ref[0], sh_ref[0])
    hb = h.astype(BF16)
    h_scr[...] = hb
    if has_vmix:
      hv_ref[0] = jnp.dot(hb, wv_ref[...], preferred_element_type=F32)

  p_ref[0] = jnp.dot(h_scr[...], w_ref[...], preferred_element_type=F32)


def _inproj(x, gain, sc, sh, w_bf16, vmix_down_bf16):
  bsz, s, d = x.shape
  n = w_bf16.shape[1]
  tm = min(INPROJ_ROWS, s)
  tn = INPROJ_COLS
  has_vmix = vmix_down_bf16 is not None
  in_specs = [
      pl.BlockSpec((1, tm, d), lambda b, i, j: (b, i, 0)),
      pl.BlockSpec((1, d), lambda b, i, j: (0, 0)),
      pl.BlockSpec((1, 1, d), lambda b, i, j: (b, 0, 0)),
      pl.BlockSpec((1, 1, d), lambda b, i, j: (b, 0, 0)),
      pl.BlockSpec((d, tn), lambda b, i, j: (0, j)),
  ]
  args = [x, gain.reshape(1, d), sc, sh, w_bf16]
  out_specs = [pl.BlockSpec((1, tm, tn), lambda b, i, j: (b, i, j))]
  out_shape = [jax.ShapeDtypeStruct((bsz, s, n), F32)]
  if has_vmix:
    in_specs.append(pl.BlockSpec((d, LANES), lambda b, i, j: (0, 0)))
    args.append(vmix_down_bf16)
    out_specs.append(pl.BlockSpec((1, tm, LANES), lambda b, i, j: (b, i, 0)))
    out_shape.append(jax.ShapeDtypeStruct((bsz, s, LANES), F32))
  outs = pl.pallas_call(
      functools.partial(_inproj_kernel, has_vmix=has_vmix),
      grid=(bsz, s // tm, n // tn),
      in_specs=in_specs,
      out_specs=out_specs,
      out_shape=out_shape,
      scratch_shapes=[pltpu.VMEM((tm, d), BF16)],
      compiler_params=_params(("arbitrary", "arbitrary", "arbitrary")),
      name="inproj",
  )(*args)
  return outs if has_vmix else (outs[0], None)


def _split_bf16(x, terms):
  parts = []
  for _ in range(terms - 1):
    part = x.astype(BF16)
    parts.append(part)
    x = x - part.astype(F32)
  parts.append(x.astype(BF16))
  return parts


def _head_sum(x, bd_ones, terms=2):
  rows = x.shape[0]
  out = []
  for j in range(x.shape[1] // LANES):
    parts = jnp.concatenate(_split_bf16(x[:, j * LANES:(j + 1) * LANES], terms), axis=0)
    s = jnp.dot(parts, bd_ones, preferred_element_type=F32)
    out.append(sum(s[i * rows:(i + 1) * rows] for i in range(terms)))
  return out[0] if len(out) == 1 else jnp.concatenate(out, axis=1)


def _shift_rows(cur, prev_tail, n):
  rolled = pltpu.roll(cur, n, axis=0)
  row = lax.broadcasted_iota(I32, cur.shape, 0)
  out = rolled
  for j in range(n):
    out = jnp.where(row == j, prev_tail[SUBLANES - n + j:SUBLANES - n + j + 1, :], out)
  return out


_R, _K, _V, _LW, _KK, _BETA, _G, _MCONV, _SGB = range(9)
_N_PREPARED = 9


def _mixer_kernel(*refs, has_vmix):
  assert WKV_CHUNK == HEAD_DIM and HEADS_PER_VREG * WKV_CHUNK == LANES
  refs = list(refs)
  p_tiles = refs[:3]
  refs = refs[3:]
  if has_vmix:
    hv_tiles, pv_tiles = refs[:3], refs[3:6]
    refs = refs[6:]
  else:
    hv_tiles = pv_tiles = (None,) * 3
  mu_ref, mulr_ref, convw_ref, vec_ref, dup_ref, aup_ref, gup_ref, bd_ref = refs[:8]
  refs = refs[8:]
  if has_vmix:
    vup_ref, v0_ref, muvf_ref = refs[:3]
    refs = refs[3:]
  tri_ref, lnw_ref, lnb_ref, rk_ref, o_ref = refs[:5]
  buf_a, buf_b, pr_scr, pk_scr, pv_scr, plr_scr, pu_scr, pvf_scr, s_scr = refs[5:]

  c = WKV_CHUNK
  d = o_ref.shape[2]
  tile = buf_a.shape[1]
  n_pairs = d // LANES
  step = pl.program_id(1)
  bd_ones = bd_ref[...]
  tri = tri_ref[...]

  tail = slice(tile - SUBLANES, tile)

  def token_shift(cur, scr, mu):
    prev = _shift_rows(cur, scr[...], 1)
    scr[...] = cur[tail, :]
    return cur + (prev - cur) * mu

  def prepare(p_ref, hv_ref, pv_ref, dst):
    col = lambda j: p_ref[0, :, j * d:(j + 1) * d]
    w0, a0, k_k, k_a = (vec_ref[j:j + 1, :] for j in range(4))

    lr = token_shift(p_ref[0, :, 8 * d:8 * d + LOW_RANK], plr_scr, mulr_ref[...])
    z = -(w0 + _dot_bf16(jnp.tanh(lr), dup_ref[...]))
    softplus = jnp.maximum(z, 0.0) + jnp.log(1.0 + jnp.exp(-jnp.abs(z)))
    dst[_LW] = -jnp.exp(-softplus - 0.5)
    a = _sigmoid(a0 + _dot_bf16(lr, aup_ref[...]))
    dst[_G] = _dot_bf16(_sigmoid(lr), gup_ref[...])
    yield

    dst[_R] = token_shift(col(5), pr_scr, mu_ref[0:1, :])
    yield

    k = token_shift(col(6), pk_scr, mu_ref[1:2, :])
    kk = k * k_k
    kk = kk / jnp.maximum(jnp.sqrt(_head_sum(kk * kk, bd_ones)), 1e-12)
    dst[_K] = k * (1.0 + (a - 1.0) * k_a)
    dst[_KK] = kk
    dst[_BETA] = a * kk
    yield

    v = token_shift(col(7), pv_scr, mu_ref[2:3, :])
    if has_vmix:
      v_first = token_shift(pv_ref[0], pvf_scr, muvf_ref[...])
      mix = _sigmoid(v0_ref[...] + _dot_bf16(hv_ref[0], vup_ref[...]))
      v = v + (v_first - v) * mix
    dst[_V] = v
    yield

    u = col(1) * col(2)
    prev_u = pu_scr[...]
    conv = (convw_ref[0:1, :] * _shift_rows(u, prev_u, 2)
            + convw_ref[1:2, :] * _shift_rows(u, prev_u, 1)
            + convw_ref[2:3, :] * u)
    pu_scr[...] = u[tail, :]
    dst[_MCONV] = _sigmoid(col(3)) * (col(0) * conv)
    yield
    dst[_SGB] = _sigmoid(col(4))

  lane = lax.broadcasted_iota(I32, (c, LANES), 1)
  head_masks = [(lane // HEAD_DIM == h).astype(BF16) for h in range(HEADS_PER_VREG)]
  t_row = lax.broadcasted_iota(I32, (c, LANES), 0)
  t_col = lane % c
  strict = t_row > t_col
  incl = t_row >= t_col
  eye = (t_row == t_col).astype(F32)
  same_head = (lax.broadcasted_iota(I32, (LANES, LANES), 0) // HEAD_DIM
               == lax.broadcasted_iota(I32, (LANES, LANES), 1) // HEAD_DIM)
  doubling_steps = c.bit_length() - 2

  def stack(xb):
    return jnp.concatenate([xb * m for m in head_masks], axis=0)

  def mm(a, b, dims=(((1,), (0,)), ((), ()))):
    return lax.dot_general(a, b, dims, preferred_element_type=F32)

  def cumulative_log_decay(src, rows):
    lw_terms = jnp.concatenate(_split_bf16(src[_LW, rows, :], 3), axis=1)
    cs = jnp.dot(tri, lw_terms, preferred_element_type=F32)
    return cs[:, :d] + cs[:, d:2 * d] + cs[:, 2 * d:]

  def pair_chunk(p, src, rows, out_rows, cum, delay):
    for _ in range(delay):
      yield
    ln = slice(p * LANES, (p + 1) * LANES)
    r = src[_R, rows, ln]
    k = src[_K, rows, ln]
    v = src[_V, rows, ln]
    lw = src[_LW, rows, ln]
    kk = src[_KK, rows, ln]
    beta = src[_BETA, rows, ln]

    cum_last = cum[c - 1:c, :]
    inv_p = jnp.exp(-cum)
    rem_p = jnp.exp(cum_last - cum)
    rt = (r * jnp.exp(cum)).astype(BF16)
    at = (kk * jnp.exp(cum - lw)).astype(BF16)
    kh_s = stack((k * inv_p).astype(BF16))
    bh_s = stack((beta * inv_p).astype(BF16))
    kb = (k * rem_p).astype(BF16)
    bb_neg = (-(beta * rem_p)).astype(BF16)
    vb = v.astype(BF16)
    v_s = stack(vb)
    yield

    sc = mm(jnp.concatenate([at, rt], axis=0), jnp.concatenate([kh_s, bh_s], axis=0), _NT)
    m_k = jnp.where(strict, sc[:c, :LANES], 0.0).astype(BF16)
    q_pow = jnp.where(strict, -sc[:c, LANES:], 0.0)
    q_k = jnp.where(incl, sc[c:, :LANES], 0.0).astype(BF16)
    q_b = jnp.where(incl, sc[c:, LANES:], 0.0).astype(BF16)
    yield

    w_inv = eye + q_pow
    q_pow = q_pow.astype(BF16)
    kv = mm(jnp.concatenate([m_k, q_k], axis=0), v_s)
    q_pow = mm(q_pow, stack(q_pow)).astype(BF16)
    yield
    for i in range(doubling_steps):
      w_s = stack(w_inv.astype(BF16))
      if i < doubling_steps - 1:
        prod = mm(q_pow, jnp.concatenate([w_s, stack(q_pow)], axis=1))
        q_pow = prod[:, LANES:].astype(BF16)
      else:
        prod = mm(q_pow, w_s)
      w_inv = w_inv + prod[:, :LANES]
      yield

    wa = mm(w_inv.astype(BF16),
            jnp.concatenate([stack(at), stack(kv[:c].astype(BF16))], axis=1))
    a2 = wa[:, :LANES].astype(BF16)
    v2 = wa[:, LANES:]
    yield

    state = s_scr[p]
    res = mm(jnp.concatenate([a2, rt], axis=0), state.astype(BF16), _NT)
    u = res[:c] + v2
    ub = u.astype(BF16)
    yield
    y = res[c:] + kv[c:] - mm(q_b, stack(ub))
    update = mm(jnp.concatenate([vb, ub], axis=0), jnp.concatenate([kb, bb_neg], axis=0), _TN)
    s_scr[p] = state * jnp.exp(cum_last) + jnp.where(same_head, update, 0.0)
    yield

    inv_n = 1.0 / HEAD_DIM
    sums = _head_sum(jnp.concatenate([y, r * k * rk_ref[:, ln]], axis=0), bd_ones, terms=1)
    yield
    yc = y - sums[:c] * inv_n
    var = _head_sum(yc * yc, bd_ones, terms=1) * inv_n
    yield
    yn = yc * lax.rsqrt(var + LN_X_EPS) * lnw_ref[:, ln] + lnb_ref[:, ln]
    y_rwkv = (yn + sums[c:] * v) * src[_G, rows, ln]
    o_ref[0, out_rows, ln] = src[_MCONV, rows, ln] + src[_SGB, rows, ln] * y_rwkv

  def run_round_robin(chains):
    for _ in itertools.zip_longest(*chains):
      pass

  def slot(src, out_row0, p_ref, hv_ref, pv_ref, dst):
    chains = [prepare(p_ref, hv_ref, pv_ref, dst)]
    for j in range(tile // c):
      rows = slice(j * c, (j + 1) * c)
      out_rows = slice(out_row0 + j * c, out_row0 + (j + 1) * c)
      cum = cumulative_log_decay(src, rows)
      chains += [pair_chunk(p, src, rows, out_rows, cum[:, p * LANES:(p + 1) * LANES],
                            j * WKV_STAGGER) for p in range(n_pairs)]
    run_round_robin(chains)

  @pl.when(step == 0)
  def _():
    for scr in (pr_scr, pk_scr, pv_scr, plr_scr, pu_scr, pvf_scr, s_scr):
      scr[...] = jnp.zeros_like(scr)
    run_round_robin([prepare(p_tiles[0], hv_tiles[0], pv_tiles[0], buf_a)])

  slot(buf_a, 0, p_tiles[1], hv_tiles[1], pv_tiles[1], buf_b)
  slot(buf_b, tile, p_tiles[2], hv_tiles[2], pv_tiles[2], buf_a)


def _mixer(p, p_first_layer, hv, mu3, mu_lr, conv_w, vecs, decay_up_pad, aaa_up_pad, gate_up_pad,
           bd_ones, vmix_up_pad, v0, mu_v_first, tri, ln_w, ln_b, r_k):
  bsz, s, n_in = p.shape
  d = conv_w.shape[1]
  tile = MIXER_TILE
  n_tiles = s // tile
  has_vmix = hv is not None
  tile_maps = [lambda b, i: (b, 0, 0),
               lambda b, i: (b, 2 * i + 1, 0),
               lambda b, i: (b, jnp.minimum(2 * i + 2, n_tiles - 1), 0)]
  full = lambda a: pl.BlockSpec(a.shape, lambda b, i: (0,) * a.ndim)
  vec = pl.BlockSpec((1, d), lambda b, i: (0, 0))
  in_specs = [pl.BlockSpec((1, tile, n_in), m) for m in tile_maps]
  args = [p] * 3
  if has_vmix:
    in_specs += [pl.BlockSpec((1, tile, LANES), m) for m in tile_maps]
    v_col = 7
    in_specs += [pl.BlockSpec((1, tile, d), lambda b, i, m=m: m(b, i)[:2] + (v_col,))
                 for m in tile_maps]
    args += [hv] * 3 + [p_first_layer] * 3
  consts = [mu3, mu_lr, conv_w, vecs, decay_up_pad, aaa_up_pad, gate_up_pad, bd_ones]
  if has_vmix:
    consts += [vmix_up_pad, v0, mu_v_first]
  consts += [tri]
  in_specs += [full(a) for a in consts] + [vec, vec, vec]
  args += consts + [ln_w.reshape(1, d), ln_b.reshape(1, d), r_k.reshape(1, d)]
  return pl.pallas_call(
      functools.partial(_mixer_kernel, has_vmix=has_vmix),
      grid=(bsz, n_tiles // 2),
      in_specs=in_specs,
      out_specs=pl.BlockSpec((1, 2 * tile, d), lambda b, i: (b, i, 0)),
      out_shape=jax.ShapeDtypeStruct((bsz, s, d), F32),
      scratch_shapes=[pltpu.VMEM((_N_PREPARED, tile, d), F32)] * 2
      + [pltpu.VMEM((SUBLANES, d), F32)] * 3
      + [pltpu.VMEM((SUBLANES, LOW_RANK), F32)]
      + [pltpu.VMEM((SUBLANES, d), F32)] * 2
      + [pltpu.VMEM((d // LANES, LANES, LANES), F32)],
      compiler_params=_params(("arbitrary", "arbitrary")),
      name="mixer",
  )(*args)


def _outproj_kernel(m_ref, x_ref, w_ref, g1_ref, gain_ref, sc_ref, sh_ref, wr_ref, rb_ref,
                    xo_ref, h2_ref, eid_ref, wrow_ref):
  mix = jnp.dot(m_ref[0].astype(BF16), w_ref[...], preferred_element_type=F32)
  x = x_ref[0] + g1_ref[0] * mix
  xo_ref[0] = x
  h2 = _modulated_norm(x, gain_ref[...], sc_ref[0], sh_ref[0])
  h2_ref[0] = h2

  logits = _dot_bf16(wr_ref[...], h2, _NT)
  tm = logits.shape[1]
  epg, ng = EXPERTS_PER_GROUP, N_GROUPS
  aff = [_sigmoid(logits[j * ng:(j + 1) * ng, :]) for j in range(epg)]
  sel = [aff[j] + rb_ref[j * ng:(j + 1) * ng, :] for j in range(epg)]
  hi01, lo01 = jnp.maximum(sel[0], sel[1]), jnp.minimum(sel[0], sel[1])
  hi23, lo23 = jnp.maximum(sel[2], sel[3]), jnp.minimum(sel[2], sel[3])
  gscore = jnp.maximum(hi01, hi23) + jnp.maximum(jnp.minimum(hi01, hi23),
                                                 jnp.maximum(lo01, lo23))
  gid = lax.broadcasted_iota(I32, (ng, tm), 0)
  gmax = jnp.max(gscore, axis=0, keepdims=True)
  gsel = jnp.min(jnp.where(gscore == gmax, gid, ng), axis=0, keepdims=True)
  in_g = gid == gsel
  cand = [jnp.sum(jnp.where(in_g, sel[j], 0.0), axis=0, keepdims=True) for j in range(epg)]
  affc = [jnp.sum(jnp.where(in_g, aff[j], 0.0), axis=0, keepdims=True) for j in range(epg)]

  def first_argmax(vals):
    best = vals[0]
    for val in vals[1:]:
      best = jnp.maximum(best, val)
    idx = jnp.full(best.shape, epg, I32)
    for j in reversed(range(epg)):
      idx = jnp.where(vals[j] == best, j, idx)
    return idx

  i1 = first_argmax(cand)
  i2 = first_argmax([jnp.where(i1 == j, -jnp.inf, cand[j]) for j in range(epg)])
  pick = lambda idx: sum(jnp.where(idx == j, affc[j], 0.0) for j in range(epg))
  a1, a2 = pick(i1), pick(i2)
  denom = a1 + a2
  eid_ref[0] = jnp.concatenate([gsel * epg + i1, gsel * epg + i2], axis=0)
  wpad = jnp.concatenate([a1 / denom, a2 / denom, jnp.zeros((LANES - TOP_K, tm), F32)], axis=0)
  wrow_ref[0] = wpad.T


def _outproj(merged, x, w_out_bf16, g1, gain2, sc2, sh2, w_router_t, router_bias_col):
  bsz, s, d = x.shape
  tm = min(OUTPROJ_ROWS, s)
  row = pl.BlockSpec((1, tm, d), lambda b, i: (b, i, 0))
  mod = pl.BlockSpec((1, 1, d), lambda b, i: (b, 0, 0))
  full = lambda a: pl.BlockSpec(a.shape, lambda b, i: (0,) * a.ndim)
  gain2 = gain2.reshape(1, d)
  return pl.pallas_call(
      _outproj_kernel,
      grid=(bsz, s // tm),
      in_specs=[row, row, full(w_out_bf16), mod, full(gain2), mod, mod,
                full(w_router_t), full(router_bias_col)],
      out_specs=[row, row,
                 pl.BlockSpec((1, TOP_K, tm), lambda b, i: (b, 0, i)),
                 pl.BlockSpec((1, tm, LANES), lambda b, i: (b, i, 0))],
      out_shape=[jax.ShapeDtypeStruct((bsz, s, d), F32),
                 jax.ShapeDtypeStruct((bsz, s, d), F32),
                 jax.ShapeDtypeStruct((bsz, TOP_K, s), I32),
                 jax.ShapeDtypeStruct((bsz, s, LANES), F32)],
      compiler_params=_params(("arbitrary", "arbitrary")),
      name="outproj_router",
  )(merged, x, w_out_bf16, g1, gain2, sc2, sh2, w_router_t, router_bias_col)


def _rank_kernel(eid_ref, triu_ref, rank_ref, size_ref, carry_scr):
  @pl.when((pl.program_id(0) == 0) & (pl.program_id(1) == 0))
  def _():
    carry_scr[...] = jnp.zeros_like(carry_scr)

  tn = eid_ref.shape[2]
  eid = eid_ref[0]
  expert = lax.broadcasted_iota(I32, (N_EXPERTS, tn), 0)
  hot = [(expert == eid[slot:slot + 1, :]).astype(F32) for slot in range(TOP_K)]
  both = hot[0] + hot[1]
  before = jnp.dot(both.astype(BF16), triu_ref[...], preferred_element_type=F32)
  carry = carry_scr[...]
  count = before + jnp.concatenate([carry] * (tn // LANES), axis=1)
  rank_ref[0] = jnp.concatenate(
      [jnp.sum(hot[slot] * count, axis=0, keepdims=True) for slot in range(TOP_K)],
      axis=0).astype(I32)
  total = carry + jnp.dot(both.astype(BF16), jnp.ones((tn, LANES), BF16),
                          preferred_element_type=F32)
  carry_scr[...] = total
  size_ref[...] = total.astype(I32)


def _rank(eid, triu_bf16):
  bsz, _, s = eid.shape
  tn = min(RANK_COLS, s)
  return pl.pallas_call(
      _rank_kernel,
      grid=(bsz, s // tn),
      in_specs=[pl.BlockSpec((1, TOP_K, tn), lambda b, i: (b, 0, i)),
                pl.BlockSpec((tn, tn), lambda b, i: (0, 0))],
      out_specs=[pl.BlockSpec((1, TOP_K, tn), lambda b, i: (b, 0, i)),
                 pl.BlockSpec((N_EXPERTS, LANES), lambda b, i: (0, 0))],
      out_shape=[jax.ShapeDtypeStruct((bsz, TOP_K, s), I32),
                 jax.ShapeDtypeStruct((N_EXPERTS, LANES), I32)],
      scratch_shapes=[pltpu.VMEM((N_EXPERTS, LANES), F32)],
      compiler_params=_params(("arbitrary", "arbitrary")),
      name="rank",
  )(eid, triu_bf16)


def _token_row(ref, group, sub):
  return ref.at[group, pl.ds(sub, 1), :]


def _table_row(ref, row):
  return ref.at[pl.ds(row, 1), :]


def _dest_kernel(eid_ref, rank_ref, start_ref, o_ref):
  tn = eid_ref.shape[2]
  expert = lax.broadcasted_iota(I32, (N_EXPERTS, tn), 0)
  rows = []
  for slot in range(TOP_K):
    hot = expert == eid_ref[0, slot:slot + 1, :]
    start = jnp.sum(jnp.where(hot, start_ref[...], 0.0), axis=0, keepdims=True)
    rows.append(start.astype(I32) + rank_ref[0, slot:slot + 1, :])
  o_ref[0] = jnp.concatenate(rows, axis=0)


def _dest(eid, rank, pad_start):
  bsz, _, s = eid.shape
  tn = min(RANK_COLS, s)
  blk = pl.BlockSpec((1, TOP_K, tn), lambda b, i: (b, 0, i))
  dest = pl.pallas_call(
      _dest_kernel,
      grid=(bsz, s // tn),
      in_specs=[blk, blk, pl.BlockSpec((N_EXPERTS, 1), lambda b, i: (0, 0))],
      out_specs=blk,
      out_shape=jax.ShapeDtypeStruct((bsz, TOP_K, s), I32),
      compiler_params=_params(("arbitrary", "arbitrary")),
      name="dest",
  )(eid, rank, pad_start.astype(F32).reshape(N_EXPERTS, 1))
  return jnp.transpose(dest, (0, 2, 1)).reshape(-1)


def _for_each_assignment(n_groups, fn):
  def body(j, carry):
    for u in range(SUBLANES):
      for slot in range(TOP_K):
        fn(j, u, slot, (j * SUBLANES + u) * TOP_K + slot)
    return carry

  lax.fori_loop(0, n_groups, body, 0)


def _scatter_kernel(dest_ref, h_ref, buf_in_ref, buf_ref, sem):
  del buf_in_ref
  src = h_ref.at[0]
  n_groups = src.shape[0]

  def copy(group, sub, row):
    return pltpu.make_async_copy(_token_row(src, group, sub), _table_row(buf_ref, row), sem)

  _for_each_assignment(n_groups, lambda j, u, slot, idx: copy(j, u, dest_ref[idx]).start())
  _for_each_assignment(n_groups, lambda j, u, slot, idx: copy(0, 0, 0).wait())


def _scatter(dest_flat, h2, n_rows):
  bsz, s, d = h2.shape
  tn = min(SCATTER_ROWS, s)
  n_tiles = s // tn
  return pl.pallas_call(
      _scatter_kernel,
      grid=(bsz, n_tiles),
      in_specs=[pl.BlockSpec((TOP_K * tn,), lambda b, i: (b * n_tiles + i,),
                             memory_space=pltpu.SMEM),
                pl.BlockSpec((1, tn // SUBLANES, SUBLANES, d), lambda b, i: (b, i, 0, 0)),
                pl.BlockSpec(memory_space=pl.ANY)],
      out_specs=pl.BlockSpec(memory_space=pl.ANY),
      scratch_shapes=[pltpu.SemaphoreType.DMA(())],
      out_shape=jax.ShapeDtypeStruct((n_rows, d), F32),
      input_output_aliases={2: 0},
      compiler_params=_params(("arbitrary", "arbitrary")),
      name="scatter_rows",
  )(dest_flat, h2.reshape(bsz, s // SUBLANES, SUBLANES, d), jnp.zeros((n_rows, d), F32))


def _expert_kernel(blk_e_ref, x_ref, win_ref, wout_ref, o_ref, win_scr, wout_scr):
  i = pl.program_id(0)
  changed = (i == 0) | (blk_e_ref[i] != blk_e_ref[jnp.maximum(i - 1, 0)])

  @pl.when(changed)
  def _():
    win_scr[...] = win_ref[0, 0].astype(BF16)
    wout_scr[...] = wout_ref[0, 0].astype(BF16)

  hidden = jnp.dot(x_ref[...].astype(BF16), win_scr[...], preferred_element_type=F32)
  half = hidden.shape[1] // 2
  gate, up = hidden[:, :half], hidden[:, half:]
  act = (gate * _sigmoid(gate)) * up
  o_ref[...] = jnp.dot(act.astype(BF16), wout_scr[...], preferred_element_type=F32)


def _experts(blk_e, buf, w_e_in, w_e_out, layer):
  n_rows, d = buf.shape
  d_hidden2 = w_e_in.shape[3]
  grid_spec = pltpu.PrefetchScalarGridSpec(
      num_scalar_prefetch=1,
      grid=(n_rows // M_BLK,),
      in_specs=[pl.BlockSpec((M_BLK, d), lambda i, be: (i, 0)),
                pl.BlockSpec((1, 1, d, d_hidden2), lambda i, be: (layer, be[i], 0, 0)),
                pl.BlockSpec((1, 1, d_hidden2 // 2, d), lambda i, be: (layer, be[i], 0, 0))],
      out_specs=pl.BlockSpec((M_BLK, d), lambda i, be: (i, 0)),
      scratch_shapes=[pltpu.VMEM((d, d_hidden2), BF16),
                      pltpu.VMEM((d_hidden2 // 2, d), BF16)],
  )
  return pl.pallas_call(
      _expert_kernel,
      grid_spec=grid_spec,
      out_shape=jax.ShapeDtypeStruct((n_rows, d), F32),
      compiler_params=_params(("arbitrary",)),
      name="experts",
  )(blk_e, buf, w_e_in, w_e_out)


def _combine_kernel(dest_ref, x_ref, wrow_ref, g2_ref, gain_ref, yb_ref,
                    o_ref, y_scr, sem, final_norm):
  tn, d = x_ref.shape[1], x_ref.shape[2]
  n_groups = y_scr.shape[1]

  def copy(row, slot, group, sub):
    return pltpu.make_async_copy(_table_row(yb_ref, row), _token_row(y_scr.at[slot], group, sub),
                                 sem)

  _for_each_assignment(n_groups, lambda j, u, slot, idx: copy(dest_ref[idx], slot, j, u).start())
  _for_each_assignment(n_groups, lambda j, u, slot, idx: copy(0, slot, 0, 0).wait())

  wrow = wrow_ref[0]
  moe = (wrow[:, 0:1] * y_scr[0].reshape(tn, d) + wrow[:, 1:2] * y_scr[1].reshape(tn, d))
  x = x_ref[0] + g2_ref[0] * moe
  if final_norm:
    x = (x * lax.rsqrt(jnp.mean(x * x, axis=-1, keepdims=True) + NORM_EPS)) * gain_ref[...]
  o_ref[0] = x


def _combine(dest_flat, x, wrow, g2, final_gain, yb, final_norm):
  bsz, s, d = x.shape
  tn = min(COMBINE_ROWS, s)
  n_tiles = s // tn
  row = pl.BlockSpec((1, tn, d), lambda b, i: (b, i, 0))
  return pl.pallas_call(
      functools.partial(_combine_kernel, final_norm=final_norm),
      grid=(bsz, n_tiles),
      in_specs=[pl.BlockSpec((TOP_K * tn,), lambda b, i: (b * n_tiles + i,),
                             memory_space=pltpu.SMEM),
                row,
                pl.BlockSpec((1, tn, LANES), lambda b, i: (b, i, 0)),
                pl.BlockSpec((1, 1, d), lambda b, i: (b, 0, 0)),
                pl.BlockSpec((1, d), lambda b, i: (0, 0)),
                pl.BlockSpec(memory_space=pl.ANY)],
      out_specs=row,
      scratch_shapes=[pltpu.VMEM((TOP_K, tn // SUBLANES, SUBLANES, d), F32),
                      pltpu.SemaphoreType.DMA(())],
      out_shape=jax.ShapeDtypeStruct((bsz, s, d), F32),
      compiler_params=_params(("arbitrary", "arbitrary")),
      name="combine",
  )(dest_flat, x, wrow, g2, final_gain.reshape(1, d), yb)


def _pad_rows(w, offset, total):
  return jnp.pad(w.astype(BF16), ((offset, total - offset - w.shape[0]), (0, 0)))


def kernel(x, c, w_ada, b_ada, norm1_gain, norm2_gain, w_in, token_mu, conv_w, w0, decay_up,
           a0, aaa_up, gate_up, k_k, k_a, r_k, ln_x_w, ln_x_b, vmix_down, vmix_up, v0, w_out,
           w_router, router_bias, w_expert_in, w_expert_out, final_gain):
  bsz, s, d = x.shape
  depth = w_in.shape[0]
  n_tok = bsz * s
  n_rows = n_tok * TOP_K + N_EXPERTS * M_BLK

  ada = _ada(c, w_ada, b_ada)
  mods = ada.reshape(depth, bsz, 6, 1, d)

  lane = jnp.arange(LANES)
  bd_ones = (lane[:, None] // HEAD_DIM == lane[None, :] // HEAD_DIM).astype(BF16)
  t = jnp.arange(WKV_CHUNK)
  tri = (t[:, None] >= t[None, :]).astype(BF16)
  tc = jnp.arange(min(RANK_COLS, s))
  triu = (tc[:, None] < tc[None, :]).astype(BF16)

  perm = (jnp.arange(N_EXPERTS) % N_GROUPS) * EXPERTS_PER_GROUP + jnp.arange(N_EXPERTS) // N_GROUPS
  w_router_t = w_router.T[perm].astype(BF16)
  router_bias_col = router_bias[perm].reshape(N_EXPERTS, 1)

  p_first_layer = None
  for l in range(depth):
    sh1, sc1, g1, sh2, sc2, g2 = (mods[l, :, j] for j in range(6))
    has_vmix = l > 0
    vmix_down_pad = None
    if has_vmix:
      vmix_down_pad = jnp.zeros((d, LANES), BF16).at[:, :VMIX_RANK].set(
          vmix_down[l - 1].astype(BF16))
    p, hv = _inproj(x, norm1_gain[l], sc1, sh1, w_in[l].astype(BF16), vmix_down_pad)

    n_shift = token_mu.shape[1]
    mu3 = token_mu[l, :3 * d].reshape(3, d)
    mu_lr = token_mu[l, 3 * d:].reshape(1, n_shift - 3 * d)
    vecs = jnp.stack([w0[l], a0[l], k_k[l], k_a[l]])
    if not has_vmix:
      p_first_layer = p
    merged = _mixer(
        p, p_first_layer, hv, mu3, mu_lr, conv_w[l], vecs,
        _pad_rows(decay_up[l], 0, LOW_RANK),
        _pad_rows(aaa_up[l], DECAY_RANK, LOW_RANK),
        _pad_rows(gate_up[l], DECAY_RANK + AAA_RANK, LOW_RANK),
        bd_ones,
        _pad_rows(vmix_up[l - 1], 0, LANES) if has_vmix else None,
        v0[l - 1].reshape(1, d) if has_vmix else None,
        token_mu[0, 2 * d:3 * d].reshape(1, d) if has_vmix else None,
        tri, ln_x_w[l], ln_x_b[l], r_k[l])
    x, h2, eid, wrow = _outproj(merged, x, w_out[l].astype(BF16), g1, norm2_gain[l], sc2, sh2,
                                w_router_t, router_bias_col)

    rank, sizes = _rank(eid, triu)
    sizes = sizes[:, 0]
    padded = (sizes + M_BLK - 1) // M_BLK * M_BLK
    pad_end = jnp.cumsum(padded)
    pad_start = (pad_end - padded).astype(I32)
    blk_first_row = jnp.arange(n_rows // M_BLK, dtype=I32) * M_BLK
    blk_e = jnp.minimum(
        jnp.sum((pad_end[None, :] <= blk_first_row[:, None]).astype(I32), axis=1),
        N_EXPERTS - 1)

    dest_flat = _dest(eid, rank, pad_start)
    buf = _scatter(dest_flat, h2, n_rows)
    yb = _experts(blk_e, buf, w_expert_in, w_expert_out, l)
    x = _combine(dest_flat, x, wrow, g2, final_gain, yb, final_norm=(l == depth - 1))
  return x
```

```python
import functools
import itertools

import jax
import jax.numpy as jnp
from jax import lax
from jax.experimental import pallas as pl
from jax.experimental.pallas import tpu as pltpu

F32 = jnp.float32
BF16 = jnp.bfloat16
I32 = jnp.int32

HEAD_DIM = 64
CONV_WIDTH = 3
DECAY_RANK = 64
AAA_RANK = 64
GATE_RANK = 128
LOW_RANK = DECAY_RANK + AAA_RANK + GATE_RANK
VMIX_RANK = 32
N_EXPERTS = 32
N_GROUPS = 8
EXPERTS_PER_GROUP = N_EXPERTS // N_GROUPS
TOP_K = 2
M_BLK = 256
NORM_EPS = 1e-6
LN_X_EPS = 64e-5
LOG2_E = 1.4426950408889634

LANES = 128
SUBLANES = 8
HEADS_PER_VREG = LANES // HEAD_DIM
VMEM_LIMIT_BYTES = 56 * 1024 * 1024

WKV_CHUNK = 64
WKV_STAGGER = 2
MIXER_TILE = 128
INPROJ_ROWS = 1024
INPROJ_COLS = 1408
OUTPROJ_ROWS = 256
RANK_COLS = 512
SCATTER_ROWS = 512
COMBINE_ROWS = 512


def _dot_bf16(a, b, dims=(((1,), (0,)), ((), ()))):
  return lax.dot_general(a.astype(BF16), b.astype(BF16), dims,
                         preferred_element_type=F32)


_NT = (((1,), (1,)), ((), ()))
_TN = (((0,), (0,)), ((), ()))


def _sigmoid(x):
  return 1.0 / (1.0 + jnp.exp(-x))


def _params(semantics):
  return pltpu.CompilerParams(dimension_semantics=semantics,
                              vmem_limit_bytes=VMEM_LIMIT_BYTES)


def _ada_kernel(c_ref, w_ref, b_ref, o_ref):
  c = c_ref[...]
  cond = c * _sigmoid(c)
  o_ref[0] = _dot_bf16(cond, w_ref[0]) + b_ref[0]


def _ada(c, w_ada, b_ada):
  depth, d, n = w_ada.shape
  bsz = c.shape[0]
  tn = d
  return pl.pallas_call(
      _ada_kernel,
      grid=(depth, n // tn),
      in_specs=[
          pl.BlockSpec((bsz, d), lambda l, j: (0, 0)),
          pl.BlockSpec((1, d, tn), lambda l, j: (l, 0, j)),
          pl.BlockSpec((1, 1, tn), lambda l, j: (l, 0, j)),
      ],
      out_specs=pl.BlockSpec((1, bsz, tn), lambda l, j: (l, 0, j)),
      out_shape=jax.ShapeDtypeStruct((depth, bsz, n), F32),
      compiler_params=_params(("arbitrary", "arbitrary")),
      name="ada",
  )(c, w_ada, b_ada.reshape(depth, 1, n))


def _modulated_norm(x, gain, scale, shift):
  y = x * lax.rsqrt(jnp.mean(x * x, axis=-1, keepdims=True) + NORM_EPS)
  return (y * gain) * (1.0 + scale) + shift


def _inproj_kernel(x_ref, gain_ref, sc_ref, sh_ref, w_ref, *rest, has_vmix):
  if has_vmix:
    wv_ref, p_ref, hv_ref, h_scr = rest
  else:
    p_ref, h_scr = rest

  @pl.when(pl.program_id(2) == 0)
  def _():
    h = _modulated_norm(x_ref[0], gain_ref[...], sc_ref[0], sh_ref[0])
    hb = h.astype(BF16)
    h_scr[...] = hb
    if has_vmix:
      hv_ref[0] = jnp.dot(hb, wv_ref[...], preferred_element_type=F32)

  p_ref[0] = jnp.dot(h_scr[...], w_ref[...], preferred_element_type=F32)


def _inproj(x, gain, sc, sh, w_bf16, vmix_down_bf16):
  bsz, s, d = x.shape
  n = w_bf16.shape[1]
  tm = min(INPROJ_ROWS, s)
  tn = INPROJ_COLS
  has_vmix = vmix_down_bf16 is not None
  in_specs = [
      pl.BlockSpec((1, tm, d), lambda b, i, j: (b, i, 0)),
      pl.BlockSpec((1, d), lambda b, i, j: (0, 0)),
      pl.BlockSpec((1, 1, d), lambda b, i, j: (b, 0, 0)),
      pl.BlockSpec((1, 1, d), lambda b, i, j: (b, 0, 0)),
      pl.BlockSpec((d, tn), lambda b, i, j: (0, j)),
  ]
  args = [x, gain.reshape(1, d), sc, sh, w_bf16]
  out_specs = [pl.BlockSpec((1, tm, tn), lambda b, i, j: (b, i, j))]
  out_shape = [jax.ShapeDtypeStruct((bsz, s, n), F32)]
  if has_vmix:
    in_specs.append(pl.BlockSpec((d, LANES), lambda b, i, j: (0, 0)))
    args.append(vmix_down_bf16)
    out_specs.append(pl.BlockSpec((1, tm, LANES), lambda b, i, j: (b, i, 0)))
    out_shape.append(jax.ShapeDtypeStruct((bsz, s, LANES), F32))
  outs = pl.pallas_call(
      functools.partial(_inproj_kernel, has_vmix=has_vmix),
      grid=(bsz, s // tm, n // tn),
      in_specs=in_specs,
      out_specs=out_specs,
      out_shape=out_shape,
      scratch_shapes=[pltpu.VMEM((tm, d), BF16)],
      compiler_params=_params(("arbitrary", "arbitrary", "arbitrary")),
      name="inproj",
  )(*args)
  return outs if has_vmix else (outs[0], None)


def _split_bf16(x, terms):
  parts = []
  for _ in range(terms - 1):
    part = x.astype(BF16)
    parts.append(part)
    x = x - part.astype(F32)
  parts.append(x.astype(BF16))
  return parts


def _head_sum(x, bd_ones, terms=2):
  rows = x.shape[0]
  out = []
  for j in range(x.shape[1] // LANES):
    parts = jnp.concatenate(_split_bf16(x[:, j * LANES:(j + 1) * LANES], terms), axis=0)
    s = jnp.dot(parts, bd_ones, preferred_element_type=F32)
    out.append(sum(s[i * rows:(i + 1) * rows] for i in range(terms)))
  return out[0] if len(out) == 1 else jnp.concatenate(out, axis=1)


def _shift_rows(cur, prev_tail, n):
  rolled = pltpu.roll(cur, n, axis=0)
  row = lax.broadcasted_iota(I32, cur.shape, 0)
  out = rolled
  for j in range(n):
    out = jnp.where(row == j, prev_tail[SUBLANES - n + j:SUBLANES - n + j + 1, :], out)
  return out


_R, _K, _V, _LW, _KK, _BETA, _G, _MCONV, _SGB = range(9)
_N_PREPARED = 9


def _mixer_kernel(*refs, has_vmix):
  assert WKV_CHUNK == HEAD_DIM and HEADS_PER_VREG * WKV_CHUNK == LANES
  refs = list(refs)
  p_tiles = refs[:3]
  refs = refs[3:]
  if has_vmix:
    hv_tiles, pv_tiles = refs[:3], refs[3:6]
    refs = refs[6:]
  else:
    hv_tiles = pv_tiles = (None,) * 3
  mu_ref, mulr_ref, convw_ref, vec_ref, dup_ref, aup_ref, gup_ref, bd_ref = refs[:8]
  refs = refs[8:]
  if has_vmix:
    vup_ref, v0_ref, muvf_ref = refs[:3]
    refs = refs[3:]
  tri_ref, lnw_ref, lnb_ref, rk_ref, o_ref = refs[:5]
  buf_a, buf_b, pr_scr, pk_scr, pv_scr, plr_scr, pu_scr, pvf_scr, s_scr = refs[5:]

  c = WKV_CHUNK
  d = o_ref.shape[2]
  tile = buf_a.shape[1]
  n_pairs = d // LANES
  step = pl.program_id(1)
  bd_ones = bd_ref[...]
  tri = tri_ref[...]

  tail = slice(tile - SUBLANES, tile)

  def token_shift(cur, scr, mu):
    prev = _shift_rows(cur, scr[...], 1)
    scr[...] = cur[tail, :]
    return cur + (prev - cur) * mu

  def prepare(p_ref, hv_ref, pv_ref, dst):
    col = lambda j: p_ref[0, :, j * d:(j + 1) * d]
    w0, a0, k_k, k_a = (vec_ref[j:j + 1, :] for j in range(4))

    lr = token_shift(p_ref[0, :, 8 * d:8 * d + LOW_RANK], plr_scr, mulr_ref[...])
    z = -(w0 + _dot_bf16(jnp.tanh(lr), dup_ref[...]))
    softplus = jnp.maximum(z, 0.0) + jnp.log(1.0 + jnp.exp(-jnp.abs(z)))
    dst[_LW] = -LOG2_E * jnp.exp(-softplus - 0.5)
    a = _sigmoid(a0 + _dot_bf16(lr, aup_ref[...]))
    dst[_G] = _dot_bf16(_sigmoid(lr), gup_ref[...])
    yield

    dst[_R] = token_shift(col(5), pr_scr, mu_ref[0:1, :])
    yield

    k = token_shift(col(6), pk_scr, mu_ref[1:2, :])
    kk = k * k_k
    kk = kk / jnp.maximum(jnp.sqrt(_head_sum(kk * kk, bd_ones)), 1e-12)
    dst[_K] = k * (1.0 + (a - 1.0) * k_a)
    dst[_KK] = kk
    dst[_BETA] = a * kk
    yield

    v = token_shift(col(7), pv_scr, mu_ref[2:3, :])
    if has_vmix:
      v_first = token_shift(pv_ref[0], pvf_scr, muvf_ref[...])
      mix = _sigmoid(v0_ref[...] + _dot_bf16(hv_ref[0], vup_ref[...]))
      v = v + (v_first - v) * mix
    dst[_V] = v
    yield

    u = col(1) * col(2)
    prev_u = pu_scr[...]
    conv = (convw_ref[0:1, :] * _shift_rows(u, prev_u, 2)
            + convw_ref[1:2, :] * _shift_rows(u, prev_u, 1)
            + convw_ref[2:3, :] * u)
    pu_scr[...] = u[tail, :]
    dst[_MCONV] = _sigmoid(col(3)) * (col(0) * conv)
    yield
    dst[_SGB] = _sigmoid(col(4))

  lane = lax.broadcasted_iota(I32, (c, LANES), 1)
  head_masks = [(lane // HEAD_DIM == h).astype(BF16) for h in range(HEADS_PER_VREG)]
  t_row = lax.broadcasted_iota(I32, (c, LANES), 0)
  t_col = lane % c
  strict = t_row > t_col
  incl = t_row >= t_col
  eye = (t_row == t_col).astype(F32)
  same_head = (lax.broadcasted_iota(I32, (LANES, LANES), 0) // HEAD_DIM
               == lax.broadcasted_iota(I32, (LANES, LANES), 1) // HEAD_DIM)
  doubling_steps = c.bit_length() - 2

  def stack(xb):
    return jnp.concatenate([xb * m for m in head_masks], axis=0)

  def mm(a, b, dims=(((1,), (0,)), ((), ()))):
    return lax.dot_general(a, b, dims, preferred_element_type=F32)

  def cumulative_log_decay(src, rows):
    lw_terms = jnp.concatenate(_split_bf16(src[_LW, rows, :], 3), axis=1)
    cs = jnp.dot(tri, lw_terms, preferred_element_type=F32)
    return cs[:, :d] + cs[:, d:2 * d] + cs[:, 2 * d:]

  def pair_chunk(p, src, rows, out_rows, cum, delay):
    for _ in range(delay):
      yield
    ln = slice(p * LANES, (p + 1) * LANES)
    r = src[_R, rows, ln]
    k = src[_K, rows, ln]
    v = src[_V, rows, ln]
    lw = src[_LW, rows, ln]
    kk = src[_KK, rows, ln]
    beta = src[_BETA, rows, ln]

    cum_last = cum[c - 1:c, :]
    inv_p = jnp.exp2(-cum)
    rem_p = jnp.exp2(cum_last - cum)
    rt = (r * jnp.exp2(cum)).astype(BF16)
    at = (kk * jnp.exp2(cum - lw)).astype(BF16)
    kh_s = stack((k * inv_p).astype(BF16))
    bh_s = stack((beta * inv_p).astype(BF16))
    kb = (k * rem_p).astype(BF16)
    bb_neg = (-(beta * rem_p)).astype(BF16)
    vb = v.astype(BF16)
    v_s = stack(vb)
    yield

    sc = mm(jnp.concatenate([at, rt], axis=0), jnp.concatenate([kh_s, bh_s], axis=0), _NT)
    m_k = jnp.where(strict, sc[:c, :LANES], 0.0).astype(BF16)
    q_pow = jnp.where(strict, -sc[:c, LANES:], 0.0)
    q_k = jnp.where(incl, sc[c:, :LANES], 0.0).astype(BF16)
    q_b = jnp.where(incl, sc[c:, LANES:], 0.0).astype(BF16)
    yield

    w_inv = eye + q_pow
    q_pow = q_pow.astype(BF16)
    kv = mm(jnp.concatenate([m_k, q_k], axis=0), v_s)
    q_pow = mm(q_pow, stack(q_pow)).astype(BF16)
    yield
    for i in range(doubling_steps):
      w_s = stack(w_inv.astype(BF16))
      if i < doubling_steps - 1:
        prod = mm(q_pow, jnp.concatenate([w_s, stack(q_pow)], axis=1))
        q_pow = prod[:, LANES:].astype(BF16)
      else:
        prod = mm(q_pow, w_s)
      w_inv = w_inv + prod[:, :LANES]
      yield

    wa = mm(w_inv.astype(BF16),
            jnp.concatenate([stack(at), stack(kv[:c].astype(BF16))], axis=1))
    a2 = wa[:, :LANES].astype(BF16)
    v2 = wa[:, LANES:]
    yield

    state = s_scr[p]
    res = mm(jnp.concatenate([a2, rt], axis=0), state.astype(BF16), _NT)
    u = res[:c] + v2
    ub = u.astype(BF16)
    yield
    y = res[c:] + kv[c:] - mm(q_b, stack(ub))
    update = mm(jnp.concatenate([vb, ub], axis=0), jnp.concatenate([kb, bb_neg], axis=0), _TN)
    s_scr[p] = state * jnp.exp2(cum_last) + jnp.where(same_head, update, 0.0)
    yield

    inv_n = 1.0 / HEAD_DIM
    sums = _head_sum(jnp.concatenate([y, r * k * rk_ref[:, ln]], axis=0), bd_ones, terms=1)
    yield
    yc = y - sums[:c] * inv_n
    var = _head_sum(yc * yc, bd_ones, terms=1) * inv_n
    yield
    yn = yc * lax.rsqrt(var + LN_X_EPS) * lnw_ref[:, ln] + lnb_ref[:, ln]
    y_rwkv = (yn + sums[c:] * v) * src[_G, rows, ln]
    o_ref[0, out_rows, ln] = src[_MCONV, rows, ln] + src[_SGB, rows, ln] * y_rwkv

  def run_round_robin(chains):
    for _ in itertools.zip_longest(*chains):
      pass

  def slot(src, out_row0, p_ref, hv_ref, pv_ref, dst):
    chains = [prepare(p_ref, hv_ref, pv_ref, dst)]
    for j in range(tile // c):
      rows = slice(j * c, (j + 1) * c)
      out_rows = slice(out_row0 + j * c, out_row0 + (j + 1) * c)
      cum = cumulative_log_decay(src, rows)
      chains += [pair_chunk(p, src, rows, out_rows, cum[:, p * LANES:(p + 1) * LANES],
                            j * WKV_STAGGER) for p in range(n_pairs)]
    run_round_robin(chains)

  @pl.when(step == 0)
  def _():
    for scr in (pr_scr, pk_scr, pv_scr, plr_scr, pu_scr, pvf_scr, s_scr):
      scr[...] = jnp.zeros_like(scr)
    run_round_robin([prepare(p_tiles[0], hv_tiles[0], pv_tiles[0], buf_a)])

  slot(buf_a, 0, p_tiles[1], hv_tiles[1], pv_tiles[1], buf_b)
  slot(buf_b, tile, p_tiles[2], hv_tiles[2], pv_tiles[2], buf_a)


def _mixer(p, p_first_layer, hv, mu3, mu_lr, conv_w, vecs, decay_up_pad, aaa_up_pad, gate_up_pad,
           bd_ones, vmix_up_pad, v0, mu_v_first, tri, ln_w, ln_b, r_k):
  bsz, s, n_in = p.shape
  d = conv_w.shape[1]
  tile = MIXER_TILE
  n_tiles = s // tile
  has_vmix = hv is not None
  tile_maps = [lambda b, i: (b, 0, 0),
               lambda b, i: (b, 2 * i + 1, 0),
               lambda b, i: (b, jnp.minimum(2 * i + 2, n_tiles - 1), 0)]
  full = lambda a: pl.BlockSpec(a.shape, lambda b, i: (0,) * a.ndim)
  vec = pl.BlockSpec((1, d), lambda b, i: (0, 0))
  in_specs = [pl.BlockSpec((1, tile, n_in), m) for m in tile_maps]
  args = [p] * 3
  if has_vmix:
    in_specs += [pl.BlockSpec((1, tile, LANES), m) for m in tile_maps]
    v_col = 7
    in_specs += [pl.BlockSpec((1, tile, d), lambda b, i, m=m: m(b, i)[:2] + (v_col,))
                 for m in tile_maps]
    args += [hv] * 3 + [p_first_layer] * 3
  consts = [mu3, mu_lr, conv_w, vecs, decay_up_pad, aaa_up_pad, gate_up_pad, bd_ones]
  if has_vmix:
    consts += [vmix_up_pad, v0, mu_v_first]
  consts += [tri]
  in_specs += [full(a) for a in consts] + [vec, vec, vec]
  args += consts + [ln_w.reshape(1, d), ln_b.reshape(1, d), r_k.reshape(1, d)]
  return pl.pallas_call(
      functools.partial(_mixer_kernel, has_vmix=has_vmix),
      grid=(bsz, n_tiles // 2),
      in_specs=in_specs,
      out_specs=pl.BlockSpec((1, 2 * tile, d), lambda b, i: (b, i, 0)),
      out_shape=jax.ShapeDtypeStruct((bsz, s, d), F32),
      scratch_shapes=[pltpu.VMEM((_N_PREPARED, tile, d), F32)] * 2
      + [pltpu.VMEM((SUBLANES, d), F32)] * 3
      + [pltpu.VMEM((SUBLANES, LOW_RANK), F32)]
      + [pltpu.VMEM((SUBLANES, d), F32)] * 2
      + [pltpu.VMEM((d // LANES, LANES, LANES), F32)],
      compiler_params=_params(("arbitrary", "arbitrary")),
      name="mixer",
  )(*args)


def _outproj_kernel(m_ref, x_ref, w_ref, g1_ref, gain_ref, sc_ref, sh_ref, wr_ref, rb_ref,
                    xo_ref, h2_ref, eid_ref, wrow_ref):
  mix = jnp.dot(m_ref[0].astype(BF16), w_ref[...], preferred_element_type=F32)
  x = x_ref[0] + g1_ref[0] * mix
  xo_ref[0] = x
  h2 = _modulated_norm(x, gain_ref[...], sc_ref[0], sh_ref[0])
  h2_ref[0] = h2

  logits = _dot_bf16(wr_ref[...], h2, _NT)
  tm = logits.shape[1]
  epg, ng = EXPERTS_PER_GROUP, N_GROUPS
  aff = [_sigmoid(logits[j * ng:(j + 1) * ng, :]) for j in range(epg)]
  sel = [aff[j] + rb_ref[j * ng:(j + 1) * ng, :] for j in range(epg)]
  hi01, lo01 = jnp.maximum(sel[0], sel[1]), jnp.minimum(sel[0], sel[1])
  hi23, lo23 = jnp.maximum(sel[2], sel[3]), jnp.minimum(sel[2], sel[3])
  gscore = jnp.maximum(hi01, hi23) + jnp.maximum(jnp.minimum(hi01, hi23),
                                                 jnp.maximum(lo01, lo23))
  gid = lax.broadcasted_iota(I32, (ng, tm), 0)
  gmax = jnp.max(gscore, axis=0, keepdims=True)
  gsel = jnp.min(jnp.where(gscore == gmax, gid, ng), axis=0, keepdims=True)
  in_g = gid == gsel
  cand = [jnp.sum(jnp.where(in_g, sel[j], 0.0), axis=0, keepdims=True) for j in range(epg)]
  affc = [jnp.sum(jnp.where(in_g, aff[j], 0.0), axis=0, keepdims=True) for j in range(epg)]

  def first_argmax(vals):
    best = vals[0]
    for val in vals[1:]:
      best = jnp.maximum(best, val)
    idx = jnp.full(best.shape, epg, I32)
    for j in reversed(range(epg)):
      idx = jnp.where(vals[j] == best, j, idx)
    return idx

  i1 = first_argmax(cand)
  i2 = first_argmax([jnp.where(i1 == j, -jnp.inf, cand[j]) for j in range(epg)])
  pick = lambda idx: sum(jnp.where(idx == j, affc[j], 0.0) for j in range(epg))
  a1, a2 = pick(i1), pick(i2)
  denom = a1 + a2
  eid_ref[0] = jnp.concatenate([gsel * epg + i1, gsel * epg + i2], axis=0)
  wpad = jnp.concatenate([a1 / denom, a2 / denom, jnp.zeros((LANES - TOP_K, tm), F32)], axis=0)
  wrow_ref[0] = wpad.T


def _outproj(merged, x, w_out_bf16, g1, gain2, sc2, sh2, w_router_t, router_bias_col):
  bsz, s, d = x.shape
  tm = min(OUTPROJ_ROWS, s)
  row = pl.BlockSpec((1, tm, d), lambda b, i: (b, i, 0))
  mod = pl.BlockSpec((1, 1, d), lambda b, i: (b, 0, 0))
  full = lambda a: pl.BlockSpec(a.shape, lambda b, i: (0,) * a.ndim)
  gain2 = gain2.reshape(1, d)
  return pl.pallas_call(
      _outproj_kernel,
      grid=(bsz, s // tm),
      in_specs=[row, row, full(w_out_bf16), mod, full(gain2), mod, mod,
                full(w_router_t), full(router_bias_col)],
      out_specs=[row, row,
                 pl.BlockSpec((1, TOP_K, tm), lambda b, i: (b, 0, i)),
                 pl.BlockSpec((1, tm, LANES), lambda b, i: (b, i, 0))],
      out_shape=[jax.ShapeDtypeStruct((bsz, s, d), F32),
                 jax.ShapeDtypeStruct((bsz, s, d), F32),
                 jax.ShapeDtypeStruct((bsz, TOP_K, s), I32),
                 jax.ShapeDtypeStruct((bsz, s, LANES), F32)],
      compiler_params=_params(("arbitrary", "arbitrary")),
      name="outproj_router",
  )(merged, x, w_out_bf16, g1, gain2, sc2, sh2, w_router_t, router_bias_col)


def _rank_kernel(eid_ref, triu_ref, rank_ref, size_ref, carry_scr):
  @pl.when((pl.program_id(0) == 0) & (pl.program_id(1) == 0))
  def _():
    carry_scr[...] = jnp.zeros_like(carry_scr)

  tn = eid_ref.shape[2]
  eid = eid_ref[0]
  expert = lax.broadcasted_iota(I32, (N_EXPERTS, tn), 0)
  hot = [(expert == eid[slot:slot + 1, :]).astype(F32) for slot in range(TOP_K)]
  both = hot[0] + hot[1]
  before = jnp.dot(both.astype(BF16), triu_ref[...], preferred_element_type=F32)
  carry = carry_scr[...]
  count = before + jnp.concatenate([carry] * (tn // LANES), axis=1)
  rank_ref[0] = jnp.concatenate(
      [jnp.sum(hot[slot] * count, axis=0, keepdims=True) for slot in range(TOP_K)],
      axis=0).astype(I32)
  total = carry + jnp.dot(both.astype(BF16), jnp.ones((tn, LANES), BF16),
                          preferred_element_type=F32)
  carry_scr[...] = total
  size_ref[...] = total.astype(I32)


def _rank(eid, triu_bf16):
  bsz, _, s = eid.shape
  tn = min(RANK_COLS, s)
  return pl.pallas_call(
      _rank_kernel,
      grid=(bsz, s // tn),
      in_specs=[pl.BlockSpec((1, TOP_K, tn), lambda b, i: (b, 0, i)),
                pl.BlockSpec((tn, tn), lambda b, i: (0, 0))],
      out_specs=[pl.BlockSpec((1, TOP_K, tn), lambda b, i: (b, 0, i)),
                 pl.BlockSpec((N_EXPERTS, LANES), lambda b, i: (0, 0))],
      out_shape=[jax.ShapeDtypeStruct((bsz, TOP_K, s), I32),
                 jax.ShapeDtypeStruct((N_EXPERTS, LANES), I32)],
      scratch_shapes=[pltpu.VMEM((N_EXPERTS, LANES), F32)],
      compiler_params=_params(("arbitrary", "arbitrary")),
      name="rank",
  )(eid, triu_bf16)


def _token_row(ref, group, sub):
  return ref.at[group, pl.ds(sub, 1), :]


def _table_row(ref, row):
  return ref.at[pl.ds(row, 1), :]


def _dest_kernel(eid_ref, rank_ref, start_ref, o_ref):
  tn = eid_ref.shape[2]
  expert = lax.broadcasted_iota(I32, (N_EXPERTS, tn), 0)
  rows = []
  for slot in range(TOP_K):
    hot = expert == eid_ref[0, slot:slot + 1, :]
    start = jnp.sum(jnp.where(hot, start_ref[...], 0.0), axis=0, keepdims=True)
    rows.append(start.astype(I32) + rank_ref[0, slot:slot + 1, :])
  o_ref[0] = jnp.concatenate(rows, axis=0)


def _dest(eid, rank, pad_start):
  bsz, _, s = eid.shape
  tn = min(RANK_COLS, s)
  blk = pl.BlockSpec((1, TOP_K, tn), lambda b, i: (b, 0, i))
  dest = pl.pallas_call(
      _dest_kernel,
      grid=(bsz, s // tn),
      in_specs=[blk, blk, pl.BlockSpec((N_EXPERTS, 1), lambda b, i: (0, 0))],
      out_specs=blk,
      out_shape=jax.ShapeDtypeStruct((bsz, TOP_K, s), I32),
      compiler_params=_params(("arbitrary", "arbitrary")),
      name="dest",
  )(eid, rank, pad_start.astype(F32).reshape(N_EXPERTS, 1))
  return jnp.transpose(dest, (0, 2, 1)).reshape(-1)


def _for_each_assignment(n_groups, fn):
  def body(j, carry):
    for u in range(SUBLANES):
      for slot in range(TOP_K):
        fn(j, u, slot, (j * SUBLANES + u) * TOP_K + slot)
    return carry

  lax.fori_loop(0, n_groups, body, 0)


def _scatter_kernel(pad_end_ref, padded_ref, dest_ref, h_ref, buf_ref, zero_scr, sem, zero_sem):
  src = h_ref.at[0]
  n_groups = src.shape[0]

  @pl.when((pl.program_id(0) == 0) & (pl.program_id(1) == 0))
  def _():
    zero_scr[...] = jnp.zeros_like(zero_scr)
    n_blocks = buf_ref.shape[0] // M_BLK
    n_used = pad_end_ref[N_EXPERTS - 1] // M_BLK

    def zero_block(first_row):
      first_row = pl.multiple_of(first_row, M_BLK)
      return pltpu.make_async_copy(zero_scr, buf_ref.at[pl.ds(first_row, M_BLK), :], zero_sem)

    def for_each_zero_block(act):
      def tail(e, carry):
        @pl.when(padded_ref[e] > 0)
        def _():
          act(zero_block(pad_end_ref[e] - M_BLK))
        return carry

      def unused(blk, carry):
        act(zero_block(blk * M_BLK))
        return carry

      lax.fori_loop(0, N_EXPERTS, tail, 0)
      lax.fori_loop(n_used, n_blocks, unused, 0)

    for_each_zero_block(lambda cp: cp.start())
    for_each_zero_block(lambda cp: cp.wait())

  def copy(group, sub, row):
    return pltpu.make_async_copy(_token_row(src, group, sub), _table_row(buf_ref, row), sem)

  _for_each_assignment(n_groups, lambda j, u, slot, idx: copy(j, u, dest_ref[idx]).start())
  _for_each_assignment(n_groups, lambda j, u, slot, idx: copy(0, 0, 0).wait())


def _scatter(pad_end, padded, dest_flat, h2, n_rows):
  bsz, s, d = h2.shape
  tn = min(SCATTER_ROWS, s)
  n_tiles = s // tn
  grid_spec = pltpu.PrefetchScalarGridSpec(
      num_scalar_prefetch=2,
      grid=(bsz, n_tiles),
      in_specs=[pl.BlockSpec((TOP_K * tn,), lambda b, i, pe, pd: (b * n_tiles + i,),
                             memory_space=pltpu.SMEM),
                pl.BlockSpec((1, tn // SUBLANES, SUBLANES, d), lambda b, i, pe, pd: (b, i, 0, 0))],
      out_specs=pl.BlockSpec(memory_space=pl.ANY),
      scratch_shapes=[pltpu.VMEM((M_BLK, d), F32), pltpu.SemaphoreType.DMA(()),
                      pltpu.SemaphoreType.DMA(())],
  )
  return pl.pallas_call(
      _scatter_kernel,
      grid_spec=grid_spec,
      out_shape=jax.ShapeDtypeStruct((n_rows, d), F32),
      compiler_params=_params(("arbitrary", "arbitrary")),
      name="scatter_rows",
  )(pad_end, padded, dest_flat, h2.reshape(bsz, s // SUBLANES, SUBLANES, d))


def _expert_kernel(blk_e_ref, n_used_ref, x_ref, win_ref, wout_ref, o_ref, win_scr, wout_scr):
  i = pl.program_id(0)

  @pl.when(i >= n_used_ref[0])
  def _():
    o_ref[...] = jnp.zeros_like(o_ref)

  @pl.when(i < n_used_ref[0])
  def _():
    changed = (i == 0) | (blk_e_ref[i] != blk_e_ref[jnp.maximum(i - 1, 0)])

    @pl.when(changed)
    def _():
      win_scr[...] = win_ref[0, 0].astype(BF16)
      wout_scr[...] = wout_ref[0, 0].astype(BF16)

    hidden = jnp.dot(x_ref[...].astype(BF16), win_scr[...], preferred_element_type=F32)
    half = hidden.shape[1] // 2
    gate, up = hidden[:, :half], hidden[:, half:]
    act = (gate * _sigmoid(gate)) * up
    o_ref[...] = jnp.dot(act.astype(BF16), wout_scr[...], preferred_element_type=F32)


def _experts(blk_e, n_used, buf, w_e_in, w_e_out, layer):
  n_rows, d = buf.shape
  d_hidden2 = w_e_in.shape[3]
  block = lambda i, be, nu: jnp.minimum(i, nu[0] - 1)
  grid_spec = pltpu.PrefetchScalarGridSpec(
      num_scalar_prefetch=2,
      grid=(n_rows // M_BLK,),
      in_specs=[pl.BlockSpec((M_BLK, d), lambda i, be, nu: (block(i, be, nu), 0)),
                pl.BlockSpec((1, 1, d, d_hidden2),
                             lambda i, be, nu: (layer, be[block(i, be, nu)], 0, 0)),
                pl.BlockSpec((1, 1, d_hidden2 // 2, d),
                             lambda i, be, nu: (layer, be[block(i, be, nu)], 0, 0))],
      out_specs=pl.BlockSpec((M_BLK, d), lambda i, be, nu: (i, 0)),
      scratch_shapes=[pltpu.VMEM((d, d_hidden2), BF16),
                      pltpu.VMEM((d_hidden2 // 2, d), BF16)],
  )
  return pl.pallas_call(
      _expert_kernel,
      grid_spec=grid_spec,
      out_shape=jax.ShapeDtypeStruct((n_rows, d), F32),
      compiler_params=_params(("arbitrary",)),
      name="experts",
  )(blk_e, n_used, buf, w_e_in, w_e_out)


def _combine_kernel(dest_ref, x_ref, wrow_ref, g2_ref, gain_ref, yb_ref,
                    o_ref, y_scr, sem, final_norm):
  tn, d = x_ref.shape[1], x_ref.shape[2]
  n_groups = y_scr.shape[1]

  def copy(row, slot, group, sub):
    return pltpu.make_async_copy(_table_row(yb_ref, row), _token_row(y_scr.at[slot], group, sub),
                                 sem)

  _for_each_assignment(n_groups, lambda j, u, slot, idx: copy(dest_ref[idx], slot, j, u).start())
  _for_each_assignment(n_groups, lambda j, u, slot, idx: copy(0, slot, 0, 0).wait())

  wrow = wrow_ref[0]
  moe = (wrow[:, 0:1] * y_scr[0].reshape(tn, d) + wrow[:, 1:2] * y_scr[1].reshape(tn, d))
  x = x_ref[0] + g2_ref[0] * moe
  if final_norm:
    x = (x * lax.rsqrt(jnp.mean(x * x, axis=-1, keepdims=True) + NORM_EPS)) * gain_ref[...]
  o_ref[0] = x


def _combine(dest_flat, x, wrow, g2, final_gain, yb, final_norm):
  bsz, s, d = x.shape
  tn = min(COMBINE_ROWS, s)
  n_tiles = s // tn
  row = pl.BlockSpec((1, tn, d), lambda b, i: (b, i, 0))
  return pl.pallas_call(
      functools.partial(_combine_kernel, final_norm=final_norm),
      grid=(bsz, n_tiles),
      in_specs=[pl.BlockSpec((TOP_K * tn,), lambda b, i: (b * n_tiles + i,),
                             memory_space=pltpu.SMEM),
                row,
                pl.BlockSpec((1, tn, LANES), lambda b, i: (b, i, 0)),
                pl.BlockSpec((1, 1, d), lambda b, i: (b, 0, 0)),
                pl.BlockSpec((1, d), lambda b, i: (0, 0)),
                pl.BlockSpec(memory_space=pl.ANY)],
      out_specs=row,
      scratch_shapes=[pltpu.VMEM((TOP_K, tn // SUBLANES, SUBLANES, d), F32),
                      pltpu.SemaphoreType.DMA(())],
      out_shape=jax.ShapeDtypeStruct((bsz, s, d), F32),
      compiler_params=_params(("arbitrary", "arbitrary")),
      name="combine",
  )(dest_flat, x, wrow, g2, final_gain.reshape(1, d), yb)


def _pad_rows(w, offset, total):
  return jnp.pad(w.astype(BF16), ((offset, total - offset - w.shape[0]), (0, 0)))


def kernel(x, c, w_ada, b_ada, norm1_gain, norm2_gain, w_in, token_mu, conv_w, w0, decay_up,
           a0, aaa_up, gate_up, k_k, k_a, r_k, ln_x_w, ln_x_b, vmix_down, vmix_up, v0, w_out,
           w_router, router_bias, w_expert_in, w_expert_out, final_gain):
  bsz, s, d = x.shape
  depth = w_in.shape[0]
  n_tok = bsz * s
  n_rows = n_tok * TOP_K + N_EXPERTS * M_BLK

  ada = _ada(c, w_ada, b_ada)
  mods = ada.reshape(depth, bsz, 6, 1, d)

  lane = jnp.arange(LANES)
  bd_ones = (lane[:, None] // HEAD_DIM == lane[None, :] // HEAD_DIM).astype(BF16)
  t = jnp.arange(WKV_CHUNK)
  tri = (t[:, None] >= t[None, :]).astype(BF16)
  tc = jnp.arange(min(RANK_COLS, s))
  triu = (tc[:, None] < tc[None, :]).astype(BF16)

  perm = (jnp.arange(N_EXPERTS) % N_GROUPS) * EXPERTS_PER_GROUP + jnp.arange(N_EXPERTS) // N_GROUPS
  w_router_t = w_router.T[perm].astype(BF16)
  router_bias_col = router_bias[perm].reshape(N_EXPERTS, 1)

  p_first_layer = None
  for l in range(depth):
    sh1, sc1, g1, sh2, sc2, g2 = (mods[l, :, j] for j in range(6))
    has_vmix = l > 0
    vmix_down_pad = None
    if has_vmix:
      vmix_down_pad = jnp.zeros((d, LANES), BF16).at[:, :VMIX_RANK].set(
          vmix_down[l - 1].astype(BF16))
    p, hv = _inproj(x, norm1_gain[l], sc1, sh1, w_in[l].astype(BF16), vmix_down_pad)

    n_shift = token_mu.shape[1]
    mu3 = token_mu[l, :3 * d].reshape(3, d)
    mu_lr = token_mu[l, 3 * d:].reshape(1, n_shift - 3 * d)
    vecs = jnp.stack([w0[l], a0[l], k_k[l], k_a[l]])
    if not has_vmix:
      p_first_layer = p
    merged = _mixer(
        p, p_first_layer, hv, mu3, mu_lr, conv_w[l], vecs,
        _pad_rows(decay_up[l], 0, LOW_RANK),
        _pad_rows(aaa_up[l], DECAY_RANK, LOW_RANK),
        _pad_rows(gate_up[l], DECAY_RANK + AAA_RANK, LOW_RANK),
        bd_ones,
        _pad_rows(vmix_up[l - 1], 0, LANES) if has_vmix else None,
        v0[l - 1].reshape(1, d) if has_vmix else None,
        token_mu[0, 2 * d:3 * d].reshape(1, d) if has_vmix else None,
        tri, ln_x_w[l], ln_x_b[l], r_k[l])
    x, h2, eid, wrow = _outproj(merged, x, w_out[l].astype(BF16), g1, norm2_gain[l], sc2, sh2,
                                w_router_t, router_bias_col)

    rank, sizes = _rank(eid, triu)
    sizes = sizes[:, 0]
    padded = (sizes + M_BLK - 1) // M_BLK * M_BLK
    pad_end = jnp.cumsum(padded)
    pad_start = (pad_end - padded).astype(I32)
    blk_first_row = jnp.arange(n_rows // M_BLK, dtype=I32) * M_BLK
    blk_e = jnp.minimum(
        jnp.sum((pad_end[None, :] <= blk_first_row[:, None]).astype(I32), axis=1),
        N_EXPERTS - 1)

    dest_flat = _dest(eid, rank, pad_start)
    buf = _scatter(pad_end.astype(I32), padded.astype(I32), dest_flat, h2, n_rows)
    n_used = (pad_end[-1:] // M_BLK).astype(I32)
    yb = _experts(blk_e, n_used, buf, w_expert_in, w_expert_out, l)
    x = _combine(dest_flat, x, wrow, g2, final_gain, yb, final_norm=(l == depth - 1))
  return x
```

```python
import functools
import itertools
import math

import jax
import jax.numpy as jnp
from jax import lax
from jax.experimental import pallas as pl
from jax.experimental.pallas import tpu as pltpu

F32 = jnp.float32
BF16 = jnp.bfloat16
I32 = jnp.int32

HEAD_DIM = 64
CONV_WIDTH = 3
DECAY_RANK = 64
AAA_RANK = 64
GATE_RANK = 128
LOW_RANK = DECAY_RANK + AAA_RANK + GATE_RANK
VMIX_RANK = 32
N_EXPERTS = 32
N_GROUPS = 8
EXPERTS_PER_GROUP = N_EXPERTS // N_GROUPS
TOP_K = 2
M_BLK = 512
NORM_EPS = 1e-6
LN_X_EPS = 64e-5
LOG2_E = 1.4426950408889634

LANES = 128
SUBLANES = 8
HEADS_PER_VREG = LANES // HEAD_DIM
VMEM_LIMIT_BYTES = 56 * 1024 * 1024

WKV_CHUNK = 64
WKV_STAGGER = 2
MIXER_TILE = 128
INPROJ_ROWS = 1024
INPROJ_COLS = 1408
OUTPROJ_ROWS = 256
RANK_COLS = 512
SCATTER_ROWS = 512
COMBINE_ROWS = 512


def _dot_bf16(a, b, dims=(((1,), (0,)), ((), ()))):
  return lax.dot_general(a.astype(BF16), b.astype(BF16), dims,
                         preferred_element_type=F32)


_NT = (((1,), (1,)), ((), ()))
_TN = (((0,), (0,)), ((), ()))


def _sigmoid(x):
  return 1.0 / (1.0 + jnp.exp(-x))


def _params(semantics):
  return pltpu.CompilerParams(dimension_semantics=semantics,
                              vmem_limit_bytes=VMEM_LIMIT_BYTES)


def _ada_kernel(c_ref, w_ref, b_ref, o_ref):
  c = c_ref[...]
  cond = c * _sigmoid(c)
  o_ref[0] = _dot_bf16(cond, w_ref[0]) + b_ref[0]


def _ada(c, w_ada, b_ada):
  depth, d, n = w_ada.shape
  bsz = c.shape[0]
  tn = d
  return pl.pallas_call(
      _ada_kernel,
      grid=(depth, n // tn),
      in_specs=[
          pl.BlockSpec((bsz, d), lambda l, j: (0, 0)),
          pl.BlockSpec((1, d, tn), lambda l, j: (l, 0, j)),
          pl.BlockSpec((1, 1, tn), lambda l, j: (l, 0, j)),
      ],
      out_specs=pl.BlockSpec((1, bsz, tn), lambda l, j: (l, 0, j)),
      out_shape=jax.ShapeDtypeStruct((depth, bsz, n), F32),
      compiler_params=_params(("arbitrary", "arbitrary")),
      name="ada",
  )(c, w_ada, b_ada.reshape(depth, 1, n))


def _modulated_norm(x, gain, scale, shift):
  y = x * lax.rsqrt(jnp.mean(x * x, axis=-1, keepdims=True) + NORM_EPS)
  return (y * gain) * (1.0 + scale) + shift


def _inproj_kernel(x_ref, gain_ref, sc_ref, sh_ref, w_ref, *rest, has_vmix):
  if has_vmix:
    wv_ref, p_ref, hv_ref, h_scr = rest
  else:
    p_ref, h_scr = rest

  @pl.when(pl.program_id(2) == 0)
  def _():
    h = _modulated_norm(x_ref[0], gain_ref[...], sc_ref[0], sh_ref[0])
    hb = h.astype(BF16)
    h_scr[...] = hb
    if has_vmix:
      hv_ref[0] = jnp.dot(hb, wv_ref[...], preferred_element_type=F32)

  p_ref[0] = jnp.dot(h_scr[...], w_ref[...], preferred_element_type=F32)


def _inproj(x, gain, sc, sh, w_bf16, vmix_down_bf16):
  bsz, s, d = x.shape
  n = w_bf16.shape[1]
  tm = min(INPROJ_ROWS, s)
  tn = INPROJ_COLS
  has_vmix = vmix_down_bf16 is not None
  in_specs = [
      pl.BlockSpec((1, tm, d), lambda b, i, j: (b, i, 0)),
      pl.BlockSpec((1, d), lambda b, i, j: (0, 0)),
      pl.BlockSpec((1, 1, d), lambda b, i, j: (b, 0, 0)),
      pl.BlockSpec((1, 1, d), lambda b, i, j: (b, 0, 0)),
      pl.BlockSpec((d, tn), lambda b, i, j: (0, j)),
  ]
  args = [x, gain.reshape(1, d), sc, sh, w_bf16]
  out_specs = [pl.BlockSpec((1, tm, tn), lambda b, i, j: (b, i, j))]
  out_shape = [jax.ShapeDtypeStruct((bsz, s, n), F32)]
  if has_vmix:
    in_specs.append(pl.BlockSpec((d, LANES), lambda b, i, j: (0, 0)))
    args.append(vmix_down_bf16)
    out_specs.append(pl.BlockSpec((1, tm, LANES), lambda b, i, j: (b, i, 0)))
    out_shape.append(jax.ShapeDtypeStruct((bsz, s, LANES), F32))
  outs = pl.pallas_call(
      functools.partial(_inproj_kernel, has_vmix=has_vmix),
      grid=(bsz, s // tm, n // tn),
      in_specs=in_specs,
      out_specs=out_specs,
      out_shape=out_shape,
      scratch_shapes=[pltpu.VMEM((tm, d), BF16)],
      compiler_params=_params(("arbitrary", "arbitrary", "arbitrary")),
      name="inproj",
  )(*args)
  return outs if has_vmix else (outs[0], None)


def _split_bf16(x, terms):
  parts = []
  for _ in range(terms - 1):
    part = x.astype(BF16)
    parts.append(part)
    x = x - part.astype(F32)
  parts.append(x.astype(BF16))
  return parts


def _head_sum(x, bd_ones):
  out = [jnp.dot(x[:, j * LANES:(j + 1) * LANES].astype(BF16), bd_ones,
                 preferred_element_type=F32) for j in range(x.shape[1] // LANES)]
  return out[0] if len(out) == 1 else jnp.concatenate(out, axis=1)


def _shift_rows(cur, prev_tail, n):
  rolled = pltpu.roll(cur, n, axis=0)
  head = rolled[:SUBLANES]
  row = lax.broadcasted_iota(I32, head.shape, 0)
  for j in range(n):
    head = jnp.where(row == j, prev_tail[SUBLANES - n + j:SUBLANES - n + j + 1, :], head)
  return jnp.concatenate([head, rolled[SUBLANES:]], axis=0)


_R, _K, _V, _LW, _KK, _BETA, _G, _MCONV, _SGB = range(9)
_N_PREPARED = 9


def _mixer_kernel(*refs, has_vmix):
  assert WKV_CHUNK == HEAD_DIM and HEADS_PER_VREG * WKV_CHUNK == LANES
  refs = list(refs)
  p_tiles = refs[:3]
  refs = refs[3:]
  if has_vmix:
    hv_tiles, pv_tiles = refs[:3], refs[3:6]
    refs = refs[6:]
  else:
    hv_tiles = pv_tiles = (None,) * 3
  mu_ref, mulr_ref, convw_ref, vec_ref, dup_ref, aup_ref, gup_ref, bd_ref = refs[:8]
  refs = refs[8:]
  if has_vmix:
    vup_ref, v0_ref, muvf_ref = refs[:3]
    refs = refs[3:]
  tri_ref, lnw_ref, lnb_ref, rk_ref, o_ref = refs[:5]
  buf_a, buf_b, pr_scr, pk_scr, pv_scr, plr_scr, pu_scr, pvf_scr, s_scr = refs[5:]

  c = WKV_CHUNK
  d = o_ref.shape[2]
  tile = buf_a.shape[1]
  n_pairs = d // LANES
  step = pl.program_id(1)
  bd_ones = bd_ref[...]
  tri = tri_ref[...]

  tail = slice(tile - SUBLANES, tile)

  def token_shift(cur, scr, mu):
    prev = _shift_rows(cur, scr[...], 1)
    scr[...] = cur[tail, :]
    return cur + (prev - cur) * mu

  def prepare(p_ref, hv_ref, pv_ref, dst):
    col = lambda j: p_ref[0, :, j * d:(j + 1) * d]
    w0, a0, k_k, k_a = (vec_ref[j:j + 1, :] for j in range(4))

    lr = token_shift(p_ref[0, :, 8 * d:8 * d + LOW_RANK], plr_scr, mulr_ref[...])
    w_pre = w0 + _dot_bf16(jnp.tanh(lr), dup_ref[...])
    dst[_LW] = (-LOG2_E * math.exp(-0.5)) * _sigmoid(w_pre)
    a = _sigmoid(a0 + _dot_bf16(lr, aup_ref[...]))
    dst[_G] = _dot_bf16(_sigmoid(lr), gup_ref[...])
    yield

    dst[_R] = token_shift(col(5), pr_scr, mu_ref[0:1, :])
    yield

    k = token_shift(col(6), pk_scr, mu_ref[1:2, :])
    kk = k * k_k
    kk = kk / jnp.maximum(jnp.sqrt(_head_sum(kk * kk, bd_ones)), 1e-12)
    dst[_K] = k * (1.0 + (a - 1.0) * k_a)
    dst[_KK] = kk
    dst[_BETA] = a * kk
    yield

    v = token_shift(col(7), pv_scr, mu_ref[2:3, :])
    if has_vmix:
      v_first = token_shift(pv_ref[0], pvf_scr, muvf_ref[...])
      mix = _sigmoid(v0_ref[...] + _dot_bf16(hv_ref[0], vup_ref[...]))
      v = v + (v_first - v) * mix
    dst[_V] = v
    yield

    u = col(1) * col(2)
    prev_u = pu_scr[...]
    conv = (convw_ref[0:1, :] * _shift_rows(u, prev_u, 2)
            + convw_ref[1:2, :] * _shift_rows(u, prev_u, 1)
            + convw_ref[2:3, :] * u)
    pu_scr[...] = u[tail, :]
    dst[_MCONV] = _sigmoid(col(3)) * (col(0) * conv)
    yield
    dst[_SGB] = _sigmoid(col(4))

  lane = lax.broadcasted_iota(I32, (c, LANES), 1)
  head_masks = [(lane // HEAD_DIM == h).astype(BF16) for h in range(HEADS_PER_VREG)]
  t_row = lax.broadcasted_iota(I32, (c, LANES), 0)
  t_col = lane % c
  strict = t_row > t_col
  incl = t_row >= t_col
  eye = (t_row == t_col).astype(F32)
  same_head = (lax.broadcasted_iota(I32, (LANES, LANES), 0) // HEAD_DIM
               == lax.broadcasted_iota(I32, (LANES, LANES), 1) // HEAD_DIM)
  doubling_steps = c.bit_length() - 2

  def stack(xb):
    return jnp.concatenate([xb * m for m in head_masks], axis=0)

  def mm(a, b, dims=(((1,), (0,)), ((), ()))):
    return lax.dot_general(a, b, dims, preferred_element_type=F32)

  def cumulative_log_decay(src, rows):
    lw_terms = jnp.concatenate(_split_bf16(src[_LW, rows, :], 3), axis=1)
    cs = jnp.dot(tri, lw_terms, preferred_element_type=F32)
    return cs[:, :d] + cs[:, d:2 * d] + cs[:, 2 * d:]

  def pair_chunk(p, src, rows, out_rows, cum, delay):
    for _ in range(delay):
      yield
    ln = slice(p * LANES, (p + 1) * LANES)
    r = src[_R, rows, ln]
    k = src[_K, rows, ln]
    v = src[_V, rows, ln]
    lw = src[_LW, rows, ln]
    kk = src[_KK, rows, ln]
    beta = src[_BETA, rows, ln]

    cum_last = cum[c - 1:c, :]
    inv_p = jnp.exp2(-cum)
    rem_p = jnp.exp2(cum_last - cum)
    rt = (r * jnp.exp2(cum)).astype(BF16)
    at = (kk * jnp.exp2(cum - lw)).astype(BF16)
    kh_s = stack((k * inv_p).astype(BF16))
    bh_s = stack((beta * inv_p).astype(BF16))
    kb = (k * rem_p).astype(BF16)
    bb_neg = (-(beta * rem_p)).astype(BF16)
    vb = v.astype(BF16)
    v_s = stack(vb)
    yield

    sc = mm(jnp.concatenate([at, rt], axis=0), jnp.concatenate([kh_s, bh_s], axis=0), _NT)
    m_k = jnp.where(strict, sc[:c, :LANES], 0.0).astype(BF16)
    q_pow = jnp.where(strict, -sc[:c, LANES:], 0.0)
    q_k = jnp.where(incl, sc[c:, :LANES], 0.0).astype(BF16)
    q_b = jnp.where(incl, sc[c:, LANES:], 0.0).astype(BF16)
    yield

    w_inv = eye + q_pow
    q_pow = q_pow.astype(BF16)
    kv = mm(jnp.concatenate([m_k, q_k], axis=0), v_s)
    q_pow = mm(q_pow, stack(q_pow)).astype(BF16)
    yield
    for i in range(doubling_steps):
      w_s = stack(w_inv.astype(BF16))
      if i < doubling_steps - 1:
        prod = mm(q_pow, jnp.concatenate([w_s, stack(q_pow)], axis=1))
        q_pow = prod[:, LANES:].astype(BF16)
      else:
        prod = mm(q_pow, w_s)
      w_inv = w_inv + prod[:, :LANES]
      yield

    wa = mm(w_inv.astype(BF16),
            jnp.concatenate([stack(at), stack(kv[:c].astype(BF16))], axis=1))
    a2 = wa[:, :LANES].astype(BF16)
    v2 = wa[:, LANES:]
    yield

    state = s_scr[p]
    res = mm(jnp.concatenate([a2, rt], axis=0), state.astype(BF16), _NT)
    u = res[:c] + v2
    ub = u.astype(BF16)
    yield
    y = res[c:] + kv[c:] - mm(q_b, stack(ub))
    update = mm(jnp.concatenate([vb, ub], axis=0), jnp.concatenate([kb, bb_neg], axis=0), _TN)
    s_scr[p] = state * jnp.exp2(cum_last) + jnp.where(same_head, update, 0.0)
    yield

    inv_n = 1.0 / HEAD_DIM
    sums = _head_sum(jnp.concatenate([y, r * k * rk_ref[:, ln]], axis=0), bd_ones)
    yield
    yc = y - sums[:c] * inv_n
    var = _head_sum(yc * yc, bd_ones) * inv_n
    yield
    yn = yc * lax.rsqrt(var + LN_X_EPS) * lnw_ref[:, ln] + lnb_ref[:, ln]
    y_rwkv = (yn + sums[c:] * v) * src[_G, rows, ln]
    o_ref[0, out_rows, ln] = src[_MCONV, rows, ln] + src[_SGB, rows, ln] * y_rwkv

  def run_round_robin(chains):
    for _ in itertools.zip_longest(*chains):
      pass

  def slot(src, out_row0, p_ref, hv_ref, pv_ref, dst):
    chains = [prepare(p_ref, hv_ref, pv_ref, dst)]
    for j in range(tile // c):
      rows = slice(j * c, (j + 1) * c)
      out_rows = slice(out_row0 + j * c, out_row0 + (j + 1) * c)
      cum = cumulative_log_decay(src, rows)
      chains += [pair_chunk(p, src, rows, out_rows, cum[:, p * LANES:(p + 1) * LANES],
                            j * WKV_STAGGER) for p in range(n_pairs)]
    run_round_robin(chains)

  @pl.when(step == 0)
  def _():
    for scr in (pr_scr, pk_scr, pv_scr, plr_scr, pu_scr, pvf_scr, s_scr):
      scr[...] = jnp.zeros_like(scr)
    run_round_robin([prepare(p_tiles[0], hv_tiles[0], pv_tiles[0], buf_a)])

  slot(buf_a, 0, p_tiles[1], hv_tiles[1], pv_tiles[1], buf_b)
  slot(buf_b, tile, p_tiles[2], hv_tiles[2], pv_tiles[2], buf_a)


def _mixer(p, p_first_layer, hv, mu3, mu_lr, conv_w, vecs, decay_up_pad, aaa_up_pad, gate_up_pad,
           bd_ones, vmix_up_pad, v0, mu_v_first, tri, ln_w, ln_b, r_k):
  bsz, s, n_in = p.shape
  d = conv_w.shape[1]
  tile = MIXER_TILE
  n_tiles = s // tile
  has_vmix = hv is not None
  tile_maps = [lambda b, i: (b, 0, 0),
               lambda b, i: (b, 2 * i + 1, 0),
               lambda b, i: (b, jnp.minimum(2 * i + 2, n_tiles - 1), 0)]
  full = lambda a: pl.BlockSpec(a.shape, lambda b, i: (0,) * a.ndim)
  vec = pl.BlockSpec((1, d), lambda b, i: (0, 0))
  in_specs = [pl.BlockSpec((1, tile, n_in), m) for m in tile_maps]
  args = [p] * 3
  if has_vmix:
    in_specs += [pl.BlockSpec((1, tile, LANES), m) for m in tile_maps]
    v_col = 7
    in_specs += [pl.BlockSpec((1, tile, d), lambda b, i, m=m: m(b, i)[:2] + (v_col,))
                 for m in tile_maps]
    args += [hv] * 3 + [p_first_layer] * 3
  consts = [mu3, mu_lr, conv_w, vecs, decay_up_pad, aaa_up_pad, gate_up_pad, bd_ones]
  if has_vmix:
    consts += [vmix_up_pad, v0, mu_v_first]
  consts += [tri]
  in_specs += [full(a) for a in consts] + [vec, vec, vec]
  args += consts + [ln_w.reshape(1, d), ln_b.reshape(1, d), r_k.reshape(1, d)]
  return pl.pallas_call(
      functools.partial(_mixer_kernel, has_vmix=has_vmix),
      grid=(bsz, n_tiles // 2),
      in_specs=in_specs,
      out_specs=pl.BlockSpec((1, 2 * tile, d), lambda b, i: (b, i, 0)),
      out_shape=jax.ShapeDtypeStruct((bsz, s, d), F32),
      scratch_shapes=[pltpu.VMEM((_N_PREPARED, tile, d), F32)] * 2
      + [pltpu.VMEM((SUBLANES, d), F32)] * 3
      + [pltpu.VMEM((SUBLANES, LOW_RANK), F32)]
      + [pltpu.VMEM((SUBLANES, d), F32)] * 2
      + [pltpu.VMEM((d // LANES, LANES, LANES), F32)],
      compiler_params=_params(("arbitrary", "arbitrary")),
      name="mixer",
  )(*args)


def _outproj_kernel(m_ref, x_ref, w_ref, g1_ref, gain_ref, sc_ref, sh_ref, wr_ref, rb_ref,
                    xo_ref, h2_ref, eid_ref, wrow_ref):
  mix = jnp.dot(m_ref[0].astype(BF16), w_ref[...], preferred_element_type=F32)
  x = x_ref[0] + g1_ref[0] * mix
  xo_ref[0] = x
  h2 = _modulated_norm(x, gain_ref[...], sc_ref[0], sh_ref[0])
  h2_ref[0] = h2

  logits = _dot_bf16(wr_ref[...], h2, _NT)
  tm = logits.shape[1]
  epg, ng = EXPERTS_PER_GROUP, N_GROUPS
  aff = [_sigmoid(logits[j * ng:(j + 1) * ng, :]) for j in range(epg)]
  sel = [aff[j] + rb_ref[j * ng:(j + 1) * ng, :] for j in range(epg)]
  hi01, lo01 = jnp.maximum(sel[0], sel[1]), jnp.minimum(sel[0], sel[1])
  hi23, lo23 = jnp.maximum(sel[2], sel[3]), jnp.minimum(sel[2], sel[3])
  gscore = jnp.maximum(hi01, hi23) + jnp.maximum(jnp.minimum(hi01, hi23),
                                                 jnp.maximum(lo01, lo23))
  gid = lax.broadcasted_iota(I32, (ng, tm), 0)
  gmax = jnp.max(gscore, axis=0, keepdims=True)
  gsel = jnp.min(jnp.where(gscore == gmax, gid, ng), axis=0, keepdims=True)
  in_g = gid == gsel
  cand = [jnp.sum(jnp.where(in_g, sel[j], 0.0), axis=0, keepdims=True) for j in range(epg)]
  affc = [jnp.sum(jnp.where(in_g, aff[j], 0.0), axis=0, keepdims=True) for j in range(epg)]

  def first_argmax(vals):
    best = vals[0]
    for val in vals[1:]:
      best = jnp.maximum(best, val)
    idx = jnp.full(best.shape, epg, I32)
    for j in reversed(range(epg)):
      idx = jnp.where(vals[j] == best, j, idx)
    return idx

  i1 = first_argmax(cand)
  i2 = first_argmax([jnp.where(i1 == j, -jnp.inf, cand[j]) for j in range(epg)])
  pick = lambda idx: sum(jnp.where(idx == j, affc[j], 0.0) for j in range(epg))
  a1, a2 = pick(i1), pick(i2)
  denom = a1 + a2
  eid_ref[0] = jnp.concatenate([gsel * epg + i1, gsel * epg + i2], axis=0)
  wpad = jnp.concatenate([a1 / denom, a2 / denom, jnp.zeros((LANES - TOP_K, tm), F32)], axis=0)
  wrow_ref[0] = wpad.T


def _outproj(merged, x, w_out_bf16, g1, gain2, sc2, sh2, w_router_t, router_bias_col):
  bsz, s, d = x.shape
  tm = min(OUTPROJ_ROWS, s)
  row = pl.BlockSpec((1, tm, d), lambda b, i: (b, i, 0))
  mod = pl.BlockSpec((1, 1, d), lambda b, i: (b, 0, 0))
  full = lambda a: pl.BlockSpec(a.shape, lambda b, i: (0,) * a.ndim)
  gain2 = gain2.reshape(1, d)
  return pl.pallas_call(
      _outproj_kernel,
      grid=(bsz, s // tm),
      in_specs=[row, row, full(w_out_bf16), mod, full(gain2), mod, mod,
                full(w_router_t), full(router_bias_col)],
      out_specs=[row, row,
                 pl.BlockSpec((1, TOP_K, tm), lambda b, i: (b, 0, i)),
                 pl.BlockSpec((1, tm, LANES), lambda b, i: (b, i, 0))],
      out_shape=[jax.ShapeDtypeStruct((bsz, s, d), F32),
                 jax.ShapeDtypeStruct((bsz, s, d), F32),
                 jax.ShapeDtypeStruct((bsz, TOP_K, s), I32),
                 jax.ShapeDtypeStruct((bsz, s, LANES), F32)],
      compiler_params=_params(("arbitrary", "arbitrary")),
      name="outproj_router",
  )(merged, x, w_out_bf16, g1, gain2, sc2, sh2, w_router_t, router_bias_col)


def _rank_kernel(eid_ref, triu_ref, rank_ref, size_ref, carry_scr):
  @pl.when((pl.program_id(0) == 0) & (pl.program_id(1) == 0))
  def _():
    carry_scr[...] = jnp.zeros_like(carry_scr)

  tn = eid_ref.shape[2]
  eid = eid_ref[0]
  expert = lax.broadcasted_iota(I32, (N_EXPERTS, tn), 0)
  hot = [(expert == eid[slot:slot + 1, :]).astype(F32) for slot in range(TOP_K)]
  both = hot[0] + hot[1]
  before = jnp.dot(both.astype(BF16), triu_ref[...], preferred_element_type=F32)
  carry = carry_scr[...]
  count = before + jnp.concatenate([carry] * (tn // LANES), axis=1)
  rank_ref[0] = jnp.concatenate(
      [jnp.sum(hot[slot] * count, axis=0, keepdims=True) for slot in range(TOP_K)],
      axis=0).astype(I32)
  total = carry + jnp.dot(both.astype(BF16), jnp.ones((tn, LANES), BF16),
                          preferred_element_type=F32)
  carry_scr[...] = total
  size_ref[...] = total.astype(I32)


def _rank(eid, triu_bf16):
  bsz, _, s = eid.shape
  tn = min(RANK_COLS, s)
  return pl.pallas_call(
      _rank_kernel,
      grid=(bsz, s // tn),
      in_specs=[pl.BlockSpec((1, TOP_K, tn), lambda b, i: (b, 0, i)),
                pl.BlockSpec((tn, tn), lambda b, i: (0, 0))],
      out_specs=[pl.BlockSpec((1, TOP_K, tn), lambda b, i: (b, 0, i)),
                 pl.BlockSpec((N_EXPERTS, LANES), lambda b, i: (0, 0))],
      out_shape=[jax.ShapeDtypeStruct((bsz, TOP_K, s), I32),
                 jax.ShapeDtypeStruct((N_EXPERTS, LANES), I32)],
      scratch_shapes=[pltpu.VMEM((N_EXPERTS, LANES), F32)],
      compiler_params=_params(("arbitrary", "arbitrary")),
      name="rank",
  )(eid, triu_bf16)


def _token_row(ref, group, sub):
  return ref.at[group, pl.ds(sub, 1), :]


def _table_row(ref, row):
  return ref.at[pl.ds(row, 1), :]


def _dest_kernel(eid_ref, rank_ref, start_ref, o_ref):
  tn = eid_ref.shape[2]
  expert = lax.broadcasted_iota(I32, (N_EXPERTS, tn), 0)
  rows = []
  for slot in range(TOP_K):
    hot = expert == eid_ref[0, slot:slot + 1, :]
    start = jnp.sum(jnp.where(hot, start_ref[...], 0.0), axis=0, keepdims=True)
    rows.append(start.astype(I32) + rank_ref[0, slot:slot + 1, :])
  o_ref[0] = jnp.concatenate(rows, axis=0)


def _dest(eid, rank, pad_start):
  bsz, _, s = eid.shape
  tn = min(RANK_COLS, s)
  blk = pl.BlockSpec((1, TOP_K, tn), lambda b, i: (b, 0, i))
  dest = pl.pallas_call(
      _dest_kernel,
      grid=(bsz, s // tn),
      in_specs=[blk, blk, pl.BlockSpec((N_EXPERTS, 1), lambda b, i: (0, 0))],
      out_specs=blk,
      out_shape=jax.ShapeDtypeStruct((bsz, TOP_K, s), I32),
      compiler_params=_params(("arbitrary", "arbitrary")),
      name="dest",
  )(eid, rank, pad_start.astype(F32).reshape(N_EXPERTS, 1))
  return jnp.transpose(dest, (0, 2, 1)).reshape(-1)


def _for_each_assignment(n_groups, fn):
  def body(j, carry):
    for u in range(SUBLANES):
      for slot in range(TOP_K):
        fn(j, u, slot, (j * SUBLANES + u) * TOP_K + slot)
    return carry

  lax.fori_loop(0, n_groups, body, 0)


def _scatter_kernel(pad_end_ref, padded_ref, dest_ref, h_ref, buf_ref, zero_scr, sem, zero_sem):
  src = h_ref.at[0]
  n_groups = src.shape[0]

  @pl.when((pl.program_id(0) == 0) & (pl.program_id(1) == 0))
  def _():
    zero_scr[...] = jnp.zeros_like(zero_scr)
    n_blocks = buf_ref.shape[0] // M_BLK
    n_used = pad_end_ref[N_EXPERTS - 1] // M_BLK

    def zero_block(first_row):
      first_row = pl.multiple_of(first_row, M_BLK)
      return pltpu.make_async_copy(zero_scr, buf_ref.at[pl.ds(first_row, M_BLK), :], zero_sem)

    def for_each_zero_block(act):
      def tail(e, carry):
        @pl.when(padded_ref[e] > 0)
        def _():
          act(zero_block(pad_end_ref[e] - M_BLK))
        return carry

      def unused(blk, carry):
        act(zero_block(blk * M_BLK))
        return carry

      lax.fori_loop(0, N_EXPERTS, tail, 0)
      lax.fori_loop(n_used, n_blocks, unused, 0)

    for_each_zero_block(lambda cp: cp.start())
    for_each_zero_block(lambda cp: cp.wait())

  def copy(group, sub, row):
    return pltpu.make_async_copy(_token_row(src, group, sub), _table_row(buf_ref, row), sem)

  _for_each_assignment(n_groups, lambda j, u, slot, idx: copy(j, u, dest_ref[idx]).start())
  _for_each_assignment(n_groups, lambda j, u, slot, idx: copy(0, 0, 0).wait())


def _scatter(pad_end, padded, dest_flat, h2, n_rows):
  bsz, s, d = h2.shape
  tn = min(SCATTER_ROWS, s)
  n_tiles = s // tn
  grid_spec = pltpu.PrefetchScalarGridSpec(
      num_scalar_prefetch=2,
      grid=(bsz, n_tiles),
      in_specs=[pl.BlockSpec((TOP_K * tn,), lambda b, i, pe, pd: (b * n_tiles + i,),
                             memory_space=pltpu.SMEM),
                pl.BlockSpec((1, tn // SUBLANES, SUBLANES, d), lambda b, i, pe, pd: (b, i, 0, 0))],
      out_specs=pl.BlockSpec(memory_space=pl.ANY),
      scratch_shapes=[pltpu.VMEM((M_BLK, d), F32), pltpu.SemaphoreType.DMA(()),
                      pltpu.SemaphoreType.DMA(())],
  )
  return pl.pallas_call(
      _scatter_kernel,
      grid_spec=grid_spec,
      out_shape=jax.ShapeDtypeStruct((n_rows, d), F32),
      compiler_params=_params(("arbitrary", "arbitrary")),
      name="scatter_rows",
  )(pad_end, padded, dest_flat, h2.reshape(bsz, s // SUBLANES, SUBLANES, d))


def _expert_kernel(blk_e_ref, n_used_ref, x_ref, win_ref, wout_ref, o_ref, win_scr, wout_scr):
  i = pl.program_id(0)

  @pl.when(i >= n_used_ref[0])
  def _():
    o_ref[...] = jnp.zeros_like(o_ref)

  @pl.when(i < n_used_ref[0])
  def _():
    changed = (i == 0) | (blk_e_ref[i] != blk_e_ref[jnp.maximum(i - 1, 0)])

    @pl.when(changed)
    def _():
      win_scr[...] = win_ref[0, 0].astype(BF16)
      wout_scr[...] = wout_ref[0, 0].astype(BF16)

    hidden = jnp.dot(x_ref[...].astype(BF16), win_scr[...], preferred_element_type=F32)
    half = hidden.shape[1] // 2
    gate, up = hidden[:, :half], hidden[:, half:]
    act = (gate * _sigmoid(gate)) * up
    o_ref[...] = jnp.dot(act.astype(BF16), wout_scr[...], preferred_element_type=F32)


def _experts(blk_e, n_used, buf, w_e_in, w_e_out, layer):
  n_rows, d = buf.shape
  d_hidden2 = w_e_in.shape[3]
  block = lambda i, be, nu: jnp.minimum(i, nu[0] - 1)
  grid_spec = pltpu.PrefetchScalarGridSpec(
      num_scalar_prefetch=2,
      grid=(n_rows // M_BLK,),
      in_specs=[pl.BlockSpec((M_BLK, d), lambda i, be, nu: (block(i, be, nu), 0)),
                pl.BlockSpec((1, 1, d, d_hidden2),
                             lambda i, be, nu: (layer, be[block(i, be, nu)], 0, 0)),
                pl.BlockSpec((1, 1, d_hidden2 // 2, d),
                             lambda i, be, nu: (layer, be[block(i, be, nu)], 0, 0))],
      out_specs=pl.BlockSpec((M_BLK, d), lambda i, be, nu: (i, 0)),
      scratch_shapes=[pltpu.VMEM((d, d_hidden2), BF16),
                      pltpu.VMEM((d_hidden2 // 2, d), BF16)],
  )
  return pl.pallas_call(
      _expert_kernel,
      grid_spec=grid_spec,
      out_shape=jax.ShapeDtypeStruct((n_rows, d), F32),
      compiler_params=_params(("arbitrary",)),
      name="experts",
  )(blk_e, n_used, buf, w_e_in, w_e_out)


def _combine_kernel(dest_ref, x_ref, wrow_ref, g2_ref, gain_ref, yb_ref,
                    o_ref, y_scr, sem, final_norm):
  tn, d = x_ref.shape[1], x_ref.shape[2]
  n_groups = y_scr.shape[1]

  def copy(row, slot, group, sub):
    return pltpu.make_async_copy(_table_row(yb_ref, row), _token_row(y_scr.at[slot], group, sub),
                                 sem)

  _for_each_assignment(n_groups, lambda j, u, slot, idx: copy(dest_ref[idx], slot, j, u).start())
  _for_each_assignment(n_groups, lambda j, u, slot, idx: copy(0, slot, 0, 0).wait())

  wrow = wrow_ref[0]
  moe = (wrow[:, 0:1] * y_scr[0].reshape(tn, d) + wrow[:, 1:2] * y_scr[1].reshape(tn, d))
  x = x_ref[0] + g2_ref[0] * moe
  if final_norm:
    x = (x * lax.rsqrt(jnp.mean(x * x, axis=-1, keepdims=True) + NORM_EPS)) * gain_ref[...]
  o_ref[0] = x


def _combine(dest_flat, x, wrow, g2, final_gain, yb, final_norm):
  bsz, s, d = x.shape
  tn = min(COMBINE_ROWS, s)
  n_tiles = s // tn
  row = pl.BlockSpec((1, tn, d), lambda b, i: (b, i, 0))
  return pl.pallas_call(
      functools.partial(_combine_kernel, final_norm=final_norm),
      grid=(bsz, n_tiles),
      in_specs=[pl.BlockSpec((TOP_K * tn,), lambda b, i: (b * n_tiles + i,),
                             memory_space=pltpu.SMEM),
                row,
                pl.BlockSpec((1, tn, LANES), lambda b, i: (b, i, 0)),
                pl.BlockSpec((1, 1, d), lambda b, i: (b, 0, 0)),
                pl.BlockSpec((1, d), lambda b, i: (0, 0)),
                pl.BlockSpec(memory_space=pl.ANY)],
      out_specs=row,
      scratch_shapes=[pltpu.VMEM((TOP_K, tn // SUBLANES, SUBLANES, d), F32),
                      pltpu.SemaphoreType.DMA(())],
      out_shape=jax.ShapeDtypeStruct((bsz, s, d), F32),
      compiler_params=_params(("arbitrary", "arbitrary")),
      name="combine",
  )(dest_flat, x, wrow, g2, final_gain.reshape(1, d), yb)


def _pad_rows(w, offset, total):
  return jnp.pad(w.astype(BF16), ((offset, total - offset - w.shape[0]), (0, 0)))


def kernel(x, c, w_ada, b_ada, norm1_gain, norm2_gain, w_in, token_mu, conv_w, w0, decay_up,
           a0, aaa_up, gate_up, k_k, k_a, r_k, ln_x_w, ln_x_b, vmix_down, vmix_up, v0, w_out,
           w_router, router_bias, w_expert_in, w_expert_out, final_gain):
  bsz, s, d = x.shape
  depth = w_in.shape[0]
  n_tok = bsz * s
  n_rows = n_tok * TOP_K + N_EXPERTS * M_BLK

  ada = _ada(c, w_ada, b_ada)
  mods = ada.reshape(depth, bsz, 6, 1, d)

  lane = jnp.arange(LANES)
  bd_ones = (lane[:, None] // HEAD_DIM == lane[None, :] // HEAD_DIM).astype(BF16)
  t = jnp.arange(WKV_CHUNK)
  tri = (t[:, None] >= t[None, :]).astype(BF16)
  tc = jnp.arange(min(RANK_COLS, s))
  triu = (tc[:, None] < tc[None, :]).astype(BF16)

  perm = (jnp.arange(N_EXPERTS) % N_GROUPS) * EXPERTS_PER_GROUP + jnp.arange(N_EXPERTS) // N_GROUPS
  w_router_t = w_router.T[perm].astype(BF16)
  router_bias_col = router_bias[perm].reshape(N_EXPERTS, 1)

  p_first_layer = None
  for l in range(depth):
    sh1, sc1, g1, sh2, sc2, g2 = (mods[l, :, j] for j in range(6))
    has_vmix = l > 0
    vmix_down_pad = None
    if has_vmix:
      vmix_down_pad = jnp.zeros((d, LANES), BF16).at[:, :VMIX_RANK].set(
          vmix_down[l - 1].astype(BF16))
    p, hv = _inproj(x, norm1_gain[l], sc1, sh1, w_in[l].astype(BF16), vmix_down_pad)

    n_shift = token_mu.shape[1]
    mu3 = token_mu[l, :3 * d].reshape(3, d)
    mu_lr = token_mu[l, 3 * d:].reshape(1, n_shift - 3 * d)
    vecs = jnp.stack([w0[l], a0[l], k_k[l], k_a[l]])
    if not has_vmix:
      p_first_layer = p
    merged = _mixer(
        p, p_first_layer, hv, mu3, mu_lr, conv_w[l], vecs,
        _pad_rows(decay_up[l], 0, LOW_RANK),
        _pad_rows(aaa_up[l], DECAY_RANK, LOW_RANK),
        _pad_rows(gate_up[l], DECAY_RANK + AAA_RANK, LOW_RANK),
        bd_ones,
        _pad_rows(vmix_up[l - 1], 0, LANES) if has_vmix else None,
        v0[l - 1].reshape(1, d) if has_vmix else None,
        token_mu[0, 2 * d:3 * d].reshape(1, d) if has_vmix else None,
        tri, ln_x_w[l], ln_x_b[l], r_k[l])
    x, h2, eid, wrow = _outproj(merged, x, w_out[l].astype(BF16), g1, norm2_gain[l], sc2, sh2,
                                w_router_t, router_bias_col)

    rank, sizes = _rank(eid, triu)
    sizes = sizes[:, 0]
    padded = (sizes + M_BLK - 1) // M_BLK * M_BLK
    pad_end = jnp.cumsum(padded)
    pad_start = (pad_end - padded).astype(I32)
    blk_first_row = jnp.arange(n_rows // M_BLK, dtype=I32) * M_BLK
    blk_e = jnp.minimum(
        jnp.sum((pad_end[None, :] <= blk_first_row[:, None]).astype(I32), axis=1),
        N_EXPERTS - 1)

    dest_flat = _dest(eid, rank, pad_start)
    buf = _scatter(pad_end.astype(I32), padded.astype(I32), dest_flat, h2, n_rows)
    n_used = (pad_end[-1:] // M_BLK).astype(I32)
    yb = _experts(blk_e, n_used, buf, w_expert_in, w_expert_out, l)
    x = _combine(dest_flat, x, wrow, g2, final_gain, yb, final_norm=(l == depth - 1))
  return x
```

```python
import functools
import itertools
import math

import jax
import jax.numpy as jnp
from jax import lax
from jax.experimental import pallas as pl
from jax.experimental.pallas import tpu as pltpu

F32 = jnp.float32
BF16 = jnp.bfloat16
I32 = jnp.int32

HEAD_DIM = 64
CONV_WIDTH = 3
DECAY_RANK = 64
AAA_RANK = 64
GATE_RANK = 128
LOW_RANK = DECAY_RANK + AAA_RANK + GATE_RANK
VMIX_RANK = 32
N_EXPERTS = 32
N_GROUPS = 8
EXPERTS_PER_GROUP = N_EXPERTS // N_GROUPS
TOP_K = 2
M_BLK = 512
NORM_EPS = 1e-6
LN_X_EPS = 64e-5
LOG2_E = 1.4426950408889634

LANES = 128
SUBLANES = 8
HEADS_PER_VREG = LANES // HEAD_DIM
VMEM_LIMIT_BYTES = 56 * 1024 * 1024

WKV_CHUNK = 64
WKV_STAGGER = 2
PREPARE_DELAY = 8
MIXER_TILE = 128
INPROJ_ROWS = 1024
INPROJ_COLS = 1408
OUTPROJ_ROWS = 256
RANK_COLS = 512
SCATTER_ROWS = 512
COMBINE_ROWS = 512


def _dot_bf16(a, b, dims=(((1,), (0,)), ((), ()))):
  return lax.dot_general(a.astype(BF16), b.astype(BF16), dims,
                         preferred_element_type=F32)


_NT = (((1,), (1,)), ((), ()))
_TN = (((0,), (0,)), ((), ()))


def _sigmoid(x):
  return 1.0 / (1.0 + jnp.exp(-x))


def _params(semantics):
  return pltpu.CompilerParams(dimension_semantics=semantics,
                              vmem_limit_bytes=VMEM_LIMIT_BYTES)


def _ada_kernel(c_ref, w_ref, b_ref, o_ref):
  c = c_ref[...]
  cond = c * _sigmoid(c)
  o_ref[0] = _dot_bf16(cond, w_ref[0]) + b_ref[0]


def _ada(c, w_ada, b_ada):
  depth, d, n = w_ada.shape
  bsz = c.shape[0]
  tn = d
  return pl.pallas_call(
      _ada_kernel,
      grid=(depth, n // tn),
      in_specs=[
          pl.BlockSpec((bsz, d), lambda l, j: (0, 0)),
          pl.BlockSpec((1, d, tn), lambda l, j: (l, 0, j)),
          pl.BlockSpec((1, 1, tn), lambda l, j: (l, 0, j)),
      ],
      out_specs=pl.BlockSpec((1, bsz, tn), lambda l, j: (l, 0, j)),
      out_shape=jax.ShapeDtypeStruct((depth, bsz, n), F32),
      compiler_params=_params(("arbitrary", "arbitrary")),
      name="ada",
  )(c, w_ada, b_ada.reshape(depth, 1, n))


def _modulated_norm(x, gain, scale, shift):
  y = x * lax.rsqrt(jnp.mean(x * x, axis=-1, keepdims=True) + NORM_EPS)
  return (y * gain) * (1.0 + scale) + shift


def _inproj_kernel(x_ref, gain_ref, sc_ref, sh_ref, w_ref, *rest, has_vmix):
  if has_vmix:
    wv_ref, p_ref, hv_ref, h_scr = rest
  else:
    p_ref, h_scr = rest

  @pl.when(pl.program_id(2) == 0)
  def _():
    h = _modulated_norm(x_ref[0], gain_ref[...], sc_ref[0], sh_ref[0])
    hb = h.astype(BF16)
    h_scr[...] = hb
    if has_vmix:
      hv_ref[0] = jnp.dot(hb, wv_ref[...], preferred_element_type=F32)

  p_ref[0] = jnp.dot(h_scr[...], w_ref[...], preferred_element_type=F32)


def _inproj(x, gain, sc, sh, w_bf16, vmix_down_bf16):
  bsz, s, d = x.shape
  n = w_bf16.shape[1]
  tm = min(INPROJ_ROWS, s)
  tn = INPROJ_COLS
  has_vmix = vmix_down_bf16 is not None
  in_specs = [
      pl.BlockSpec((1, tm, d), lambda b, i, j: (b, i, 0)),
      pl.BlockSpec((1, d), lambda b, i, j: (0, 0)),
      pl.BlockSpec((1, 1, d), lambda b, i, j: (b, 0, 0)),
      pl.BlockSpec((1, 1, d), lambda b, i, j: (b, 0, 0)),
      pl.BlockSpec((d, tn), lambda b, i, j: (0, j)),
  ]
  args = [x, gain.reshape(1, d), sc, sh, w_bf16]
  out_specs = [pl.BlockSpec((1, tm, tn), lambda b, i, j: (b, i, j))]
  out_shape = [jax.ShapeDtypeStruct((bsz, s, n), F32)]
  if has_vmix:
    in_specs.append(pl.BlockSpec((d, LANES), lambda b, i, j: (0, 0)))
    args.append(vmix_down_bf16)
    out_specs.append(pl.BlockSpec((1, tm, LANES), lambda b, i, j: (b, i, 0)))
    out_shape.append(jax.ShapeDtypeStruct((bsz, s, LANES), F32))
  outs = pl.pallas_call(
      functools.partial(_inproj_kernel, has_vmix=has_vmix),
      grid=(bsz, s // tm, n // tn),
      in_specs=in_specs,
      out_specs=out_specs,
      out_shape=out_shape,
      scratch_shapes=[pltpu.VMEM((tm, d), BF16)],
      compiler_params=_params(("arbitrary", "arbitrary", "arbitrary")),
      name="inproj",
  )(*args)
  return outs if has_vmix else (outs[0], None)


def _split_bf16(x, terms):
  parts = []
  for _ in range(terms - 1):
    part = x.astype(BF16)
    parts.append(part)
    x = x - part.astype(F32)
  parts.append(x.astype(BF16))
  return parts


def _head_sum(x, bd_ones):
  out = [jnp.dot(x[:, j * LANES:(j + 1) * LANES].astype(BF16), bd_ones,
                 preferred_element_type=F32) for j in range(x.shape[1] // LANES)]
  return out[0] if len(out) == 1 else jnp.concatenate(out, axis=1)


def _shift_rows(cur, prev_tail, n):
  rolled = pltpu.roll(cur, n, axis=0)
  head = rolled[:SUBLANES]
  row = lax.broadcasted_iota(I32, head.shape, 0)
  for j in range(n):
    head = jnp.where(row == j, prev_tail[SUBLANES - n + j:SUBLANES - n + j + 1, :], head)
  return jnp.concatenate([head, rolled[SUBLANES:]], axis=0)


_R, _K, _V, _LW, _KK, _BETA, _G, _MCONV, _SGB = range(9)
_N_PREPARED = 9


def _mixer_kernel(*refs, has_vmix):
  assert WKV_CHUNK == HEAD_DIM and HEADS_PER_VREG * WKV_CHUNK == LANES
  refs = list(refs)
  p_tiles = refs[:3]
  refs = refs[3:]
  if has_vmix:
    hv_tiles, pv_tiles = refs[:3], refs[3:6]
    refs = refs[6:]
  else:
    hv_tiles = pv_tiles = (None,) * 3
  mu_ref, mulr_ref, convw_ref, vec_ref, dup_ref, aup_ref, gup_ref, bd_ref = refs[:8]
  refs = refs[8:]
  if has_vmix:
    vup_ref, v0_ref, muvf_ref = refs[:3]
    refs = refs[3:]
  tri_ref, lnw_ref, lnb_ref, rk_ref, o_ref = refs[:5]
  buf_a, buf_b, pr_scr, pk_scr, pv_scr, plr_scr, pu_scr, pvf_scr, s_scr = refs[5:]

  c = WKV_CHUNK
  d = o_ref.shape[2]
  tile = buf_a.shape[1]
  n_pairs = d // LANES
  step = pl.program_id(1)
  bd_ones = bd_ref[...]
  tri = tri_ref[...]

  tail = slice(tile - SUBLANES, tile)

  def token_shift(cur, scr, lanes, mu):
    prev = _shift_rows(cur, scr[:, lanes], 1)
    scr[:, lanes] = cur[tail, :]
    return cur + (prev - cur) * mu

  def prepare(p_ref, hv_ref, pv_ref, dst):
    lr = token_shift(p_ref[0, :, 8 * d:8 * d + LOW_RANK], plr_scr, slice(None), mulr_ref[...])
    tanh_lr = jnp.tanh(lr).astype(BF16)
    sig_lr = _sigmoid(lr).astype(BF16)
    lr = lr.astype(BF16)
    if has_vmix:
      hv = hv_ref[0].astype(BF16)
    for _ in range(PREPARE_DELAY):
      yield

    for j in range(n_pairs):
      ln = slice(j * LANES, (j + 1) * LANES)
      col = lambda g: p_ref[0, :, g * d + j * LANES:g * d + (j + 1) * LANES]
      w0, a0, k_k, k_a = (vec_ref[i:i + 1, ln] for i in range(4))

      w_pre = w0 + jnp.dot(tanh_lr, dup_ref[:, ln], preferred_element_type=F32)
      dst[_LW, :, ln] = (-LOG2_E * math.exp(-0.5)) * _sigmoid(w_pre)
      a = _sigmoid(a0 + jnp.dot(lr, aup_ref[:, ln], preferred_element_type=F32))
      dst[_G, :, ln] = jnp.dot(sig_lr, gup_ref[:, ln], preferred_element_type=F32)

      dst[_R, :, ln] = token_shift(col(5), pr_scr, ln, mu_ref[0:1, ln])
      k = token_shift(col(6), pk_scr, ln, mu_ref[1:2, ln])
      kk = k * k_k
      kk = kk / jnp.maximum(jnp.sqrt(_head_sum(kk * kk, bd_ones)), 1e-12)
      dst[_K, :, ln] = k * (1.0 + (a - 1.0) * k_a)
      dst[_KK, :, ln] = kk
      dst[_BETA, :, ln] = a * kk

      v = token_shift(col(7), pv_scr, ln, mu_ref[2:3, ln])
      if has_vmix:
        v_first = token_shift(pv_ref[0, :, ln], pvf_scr, ln, muvf_ref[:, ln])
        mix = _sigmoid(v0_ref[:, ln]
                       + jnp.dot(hv, vup_ref[:, ln], preferred_element_type=F32))
        v = v + (v_first - v) * mix
      dst[_V, :, ln] = v

      u = col(1) * col(2)
      prev_u = pu_scr[:, ln]
      conv = (convw_ref[0:1, ln] * _shift_rows(u, prev_u, 2)
              + convw_ref[1:2, ln] * _shift_rows(u, prev_u, 1)
              + convw_ref[2:3, ln] * u)
      pu_scr[:, ln] = u[tail, :]
      dst[_MCONV, :, ln] = _sigmoid(col(3)) * (col(0) * conv)
      dst[_SGB, :, ln] = _sigmoid(col(4))
      yield

  lane = lax.broadcasted_iota(I32, (c, LANES), 1)
  head_masks = [(lane // HEAD_DIM == h).astype(BF16) for h in range(HEADS_PER_VREG)]
  t_row = lax.broadcasted_iota(I32, (c, LANES), 0)
  t_col = lane % c
  strict = t_row > t_col
  incl = t_row >= t_col
  eye = (t_row == t_col).astype(F32)
  same_head = (lax.broadcasted_iota(I32, (LANES, LANES), 0) // HEAD_DIM
               == lax.broadcasted_iota(I32, (LANES, LANES), 1) // HEAD_DIM)
  doubling_steps = c.bit_length() - 2

  def stack(xb):
    return jnp.concatenate([xb * m for m in head_masks], axis=0)

  def mm(a, b, dims=(((1,), (0,)), ((), ()))):
    return lax.dot_general(a, b, dims, preferred_element_type=F32)

  def cumulative_log_decay(src, rows):
    lw_terms = jnp.concatenate(_split_bf16(src[_LW, rows, :], 3), axis=1)
    cs = jnp.dot(tri, lw_terms, preferred_element_type=F32)
    return cs[:, :d] + cs[:, d:2 * d] + cs[:, 2 * d:]

  def pair_chunk(p, src, rows, out_rows, cum, delay):
    for _ in range(delay):
      yield
    ln = slice(p * LANES, (p + 1) * LANES)
    r = src[_R, rows, ln]
    k = src[_K, rows, ln]
    v = src[_V, rows, ln]
    lw = src[_LW, rows, ln]
    kk = src[_KK, rows, ln]
    beta = src[_BETA, rows, ln]

    cum_last = cum[c - 1:c, :]
    inv_p = jnp.exp2(-cum)
    rem_p = jnp.exp2(cum_last - cum)
    rt = (r * jnp.exp2(cum)).astype(BF16)
    at = (kk * jnp.exp2(cum - lw)).astype(BF16)
    kh_s = stack((k * inv_p).astype(BF16))
    bh_s = stack((beta * inv_p).astype(BF16))
    kb = (k * rem_p).astype(BF16)
    bb_neg = (-(beta * rem_p)).astype(BF16)
    vb = v.astype(BF16)
    v_s = stack(vb)
    yield

    sc = mm(jnp.concatenate([at, rt], axis=0), jnp.concatenate([kh_s, bh_s], axis=0), _NT)
    m_k = jnp.where(strict, sc[:c, :LANES], 0.0).astype(BF16)
    q_pow = jnp.where(strict, -sc[:c, LANES:], 0.0)
    q_k = jnp.where(incl, sc[c:, :LANES], 0.0).astype(BF16)
    q_b = jnp.where(incl, sc[c:, LANES:], 0.0).astype(BF16)
    yield

    w_inv = eye + q_pow
    q_pow = q_pow.astype(BF16)
    kv = mm(jnp.concatenate([m_k, q_k], axis=0), v_s)
    q_pow = mm(q_pow, stack(q_pow)).astype(BF16)
    yield
    for i in range(doubling_steps):
      w_s = stack(w_inv.astype(BF16))
      if i < doubling_steps - 1:
        prod = mm(q_pow, jnp.concatenate([w_s, stack(q_pow)], axis=1))
        q_pow = prod[:, LANES:].astype(BF16)
      else:
        prod = mm(q_pow, w_s)
      w_inv = w_inv + prod[:, :LANES]
      yield

    wa = mm(w_inv.astype(BF16),
            jnp.concatenate([stack(at), stack(kv[:c].astype(BF16))], axis=1))
    a2 = wa[:, :LANES].astype(BF16)
    v2 = wa[:, LANES:]
    yield

    state = s_scr[p]
    res = mm(jnp.concatenate([a2, rt], axis=0), state.astype(BF16), _NT)
    u = res[:c] + v2
    ub = u.astype(BF16)
    yield
    y = res[c:] + kv[c:] - mm(q_b, stack(ub))
    update = mm(jnp.concatenate([vb, ub], axis=0), jnp.concatenate([kb, bb_neg], axis=0), _TN)
    s_scr[p] = state * jnp.exp2(cum_last) + jnp.where(same_head, update, 0.0)
    yield

    inv_n = 1.0 / HEAD_DIM
    sums = _head_sum(jnp.concatenate([y, r * k * rk_ref[:, ln]], axis=0), bd_ones)
    yield
    yc = y - sums[:c] * inv_n
    var = _head_sum(yc * yc, bd_ones) * inv_n
    yield
    yn = yc * lax.rsqrt(var + LN_X_EPS) * lnw_ref[:, ln] + lnb_ref[:, ln]
    y_rwkv = (yn + sums[c:] * v) * src[_G, rows, ln]
    o_ref[0, out_rows, ln] = src[_MCONV, rows, ln] + src[_SGB, rows, ln] * y_rwkv

  def run_round_robin(chains):
    for _ in itertools.zip_longest(*chains):
      pass

  def slot(src, out_row0, p_ref, hv_ref, pv_ref, dst):
    chains = [prepare(p_ref, hv_ref, pv_ref, dst)]
    for j in range(tile // c):
      rows = slice(j * c, (j + 1) * c)
      out_rows = slice(out_row0 + j * c, out_row0 + (j + 1) * c)
      cum = cumulative_log_decay(src, rows)
      chains += [pair_chunk(p, src, rows, out_rows, cum[:, p * LANES:(p + 1) * LANES],
                            j * WKV_STAGGER) for p in range(n_pairs)]
    run_round_robin(chains)

  @pl.when(step == 0)
  def _():
    for scr in (pr_scr, pk_scr, pv_scr, plr_scr, pu_scr, pvf_scr, s_scr):
      scr[...] = jnp.zeros_like(scr)
    run_round_robin([prepare(p_tiles[0], hv_tiles[0], pv_tiles[0], buf_a)])

  slot(buf_a, 0, p_tiles[1], hv_tiles[1], pv_tiles[1], buf_b)
  slot(buf_b, tile, p_tiles[2], hv_tiles[2], pv_tiles[2], buf_a)


def _mixer(p, p_first_layer, hv, mu3, mu_lr, conv_w, vecs, decay_up_pad, aaa_up_pad, gate_up_pad,
           bd_ones, vmix_up_pad, v0, mu_v_first, tri, ln_w, ln_b, r_k):
  bsz, s, n_in = p.shape
  d = conv_w.shape[1]
  tile = MIXER_TILE
  n_tiles = s // tile
  has_vmix = hv is not None
  tile_maps = [lambda b, i: (b, 0, 0),
               lambda b, i: (b, 2 * i + 1, 0),
               lambda b, i: (b, jnp.minimum(2 * i + 2, n_tiles - 1), 0)]
  full = lambda a: pl.BlockSpec(a.shape, lambda b, i: (0,) * a.ndim)
  vec = pl.BlockSpec((1, d), lambda b, i: (0, 0))
  in_specs = [pl.BlockSpec((1, tile, n_in), m) for m in tile_maps]
  args = [p] * 3
  if has_vmix:
    in_specs += [pl.BlockSpec((1, tile, LANES), m) for m in tile_maps]
    v_col = 7
    in_specs += [pl.BlockSpec((1, tile, d), lambda b, i, m=m: m(b, i)[:2] + (v_col,))
                 for m in tile_maps]
    args += [hv] * 3 + [p_first_layer] * 3
  consts = [mu3, mu_lr, conv_w, vecs, decay_up_pad, aaa_up_pad, gate_up_pad, bd_ones]
  if has_vmix:
    consts += [vmix_up_pad, v0, mu_v_first]
  consts += [tri]
  in_specs += [full(a) for a in consts] + [vec, vec, vec]
  args += consts + [ln_w.reshape(1, d), ln_b.reshape(1, d), r_k.reshape(1, d)]
  return pl.pallas_call(
      functools.partial(_mixer_kernel, has_vmix=has_vmix),
      grid=(bsz, n_tiles // 2),
      in_specs=in_specs,
      out_specs=pl.BlockSpec((1, 2 * tile, d), lambda b, i: (b, i, 0)),
      out_shape=jax.ShapeDtypeStruct((bsz, s, d), F32),
      scratch_shapes=[pltpu.VMEM((_N_PREPARED, tile, d), F32)] * 2
      + [pltpu.VMEM((SUBLANES, d), F32)] * 3
      + [pltpu.VMEM((SUBLANES, LOW_RANK), F32)]
      + [pltpu.VMEM((SUBLANES, d), F32)] * 2
      + [pltpu.VMEM((d // LANES, LANES, LANES), F32)],
      compiler_params=_params(("arbitrary", "arbitrary")),
      name="mixer",
  )(*args)


def _outproj_kernel(m_ref, x_ref, w_ref, g1_ref, gain_ref, sc_ref, sh_ref, wr_ref, rb_ref,
                    xo_ref, h2_ref, eid_ref, wrow_ref):
  mix = jnp.dot(m_ref[0].astype(BF16), w_ref[...], preferred_element_type=F32)
  x = x_ref[0] + g1_ref[0] * mix
  xo_ref[0] = x
  h2 = _modulated_norm(x, gain_ref[...], sc_ref[0], sh_ref[0])
  h2_ref[0] = h2

  logits = _dot_bf16(wr_ref[...], h2, _NT)
  tm = logits.shape[1]
  epg, ng = EXPERTS_PER_GROUP, N_GROUPS
  aff = [_sigmoid(logits[j * ng:(j + 1) * ng, :]) for j in range(epg)]
  sel = [aff[j] + rb_ref[j * ng:(j + 1) * ng, :] for j in range(epg)]
  hi01, lo01 = jnp.maximum(sel[0], sel[1]), jnp.minimum(sel[0], sel[1])
  hi23, lo23 = jnp.maximum(sel[2], sel[3]), jnp.minimum(sel[2], sel[3])
  gscore = jnp.maximum(hi01, hi23) + jnp.maximum(jnp.minimum(hi01, hi23),
                                                 jnp.maximum(lo01, lo23))
  gid = lax.broadcasted_iota(I32, (ng, tm), 0)
  gmax = jnp.max(gscore, axis=0, keepdims=True)
  gsel = jnp.min(jnp.where(gscore == gmax, gid, ng), axis=0, keepdims=True)
  in_g = gid == gsel
  cand = [jnp.sum(jnp.where(in_g, sel[j], 0.0), axis=0, keepdims=True) for j in range(epg)]
  affc = [jnp.sum(jnp.where(in_g, aff[j], 0.0), axis=0, keepdims=True) for j in range(epg)]

  def first_argmax(vals):
    best = vals[0]
    for val in vals[1:]:
      best = jnp.maximum(best, val)
    idx = jnp.full(best.shape, epg, I32)
    for j in reversed(range(epg)):
      idx = jnp.where(vals[j] == best, j, idx)
    return idx

  i1 = first_argmax(cand)
  i2 = first_argmax([jnp.where(i1 == j, -jnp.inf, cand[j]) for j in range(epg)])
  pick = lambda idx: sum(jnp.where(idx == j, affc[j], 0.0) for j in range(epg))
  a1, a2 = pick(i1), pick(i2)
  denom = a1 + a2
  eid_ref[0] = jnp.concatenate([gsel * epg + i1, gsel * epg + i2], axis=0)
  wpad = jnp.concatenate([a1 / denom, a2 / denom, jnp.zeros((LANES - TOP_K, tm), F32)], axis=0)
  wrow_ref[0] = wpad.T


def _outproj(merged, x, w_out_bf16, g1, gain2, sc2, sh2, w_router_t, router_bias_col):
  bsz, s, d = x.shape
  tm = min(OUTPROJ_ROWS, s)
  row = pl.BlockSpec((1, tm, d), lambda b, i: (b, i, 0))
  mod = pl.BlockSpec((1, 1, d), lambda b, i: (b, 0, 0))
  full = lambda a: pl.BlockSpec(a.shape, lambda b, i: (0,) * a.ndim)
  gain2 = gain2.reshape(1, d)
  return pl.pallas_call(
      _outproj_kernel,
      grid=(bsz, s // tm),
      in_specs=[row, row, full(w_out_bf16), mod, full(gain2), mod, mod,
                full(w_router_t), full(router_bias_col)],
      out_specs=[row, row,
                 pl.BlockSpec((1, TOP_K, tm), lambda b, i: (b, 0, i)),
                 pl.BlockSpec((1, tm, LANES), lambda b, i: (b, i, 0))],
      out_shape=[jax.ShapeDtypeStruct((bsz, s, d), F32),
                 jax.ShapeDtypeStruct((bsz, s, d), F32),
                 jax.ShapeDtypeStruct((bsz, TOP_K, s), I32),
                 jax.ShapeDtypeStruct((bsz, s, LANES), F32)],
      compiler_params=_params(("arbitrary", "arbitrary")),
      name="outproj_router",
  )(merged, x, w_out_bf16, g1, gain2, sc2, sh2, w_router_t, router_bias_col)


def _rank_kernel(eid_ref, triu_ref, rank_ref, size_ref, carry_scr):
  @pl.when((pl.program_id(0) == 0) & (pl.program_id(1) == 0))
  def _():
    carry_scr[...] = jnp.zeros_like(carry_scr)

  tn = eid_ref.shape[2]
  eid = eid_ref[0]
  expert = lax.broadcasted_iota(I32, (N_EXPERTS, tn), 0)
  hot = [(expert == eid[slot:slot + 1, :]).astype(F32) for slot in range(TOP_K)]
  both = hot[0] + hot[1]
  before = jnp.dot(both.astype(BF16), triu_ref[...], preferred_element_type=F32)
  carry = carry_scr[...]
  count = before + jnp.concatenate([carry] * (tn // LANES), axis=1)
  rank_ref[0] = jnp.concatenate(
      [jnp.sum(hot[slot] * count, axis=0, keepdims=True) for slot in range(TOP_K)],
      axis=0).astype(I32)
  total = carry + jnp.dot(both.astype(BF16), jnp.ones((tn, LANES), BF16),
                          preferred_element_type=F32)
  carry_scr[...] = total
  size_ref[...] = total.astype(I32)


def _rank(eid, triu_bf16):
  bsz, _, s = eid.shape
  tn = min(RANK_COLS, s)
  return pl.pallas_call(
      _rank_kernel,
      grid=(bsz, s // tn),
      in_specs=[pl.BlockSpec((1, TOP_K, tn), lambda b, i: (b, 0, i)),
                pl.BlockSpec((tn, tn), lambda b, i: (0, 0))],
      out_specs=[pl.BlockSpec((1, TOP_K, tn), lambda b, i: (b, 0, i)),
                 pl.BlockSpec((N_EXPERTS, LANES), lambda b, i: (0, 0))],
      out_shape=[jax.ShapeDtypeStruct((bsz, TOP_K, s), I32),
                 jax.ShapeDtypeStruct((N_EXPERTS, LANES), I32)],
      scratch_shapes=[pltpu.VMEM((N_EXPERTS, LANES), F32)],
      compiler_params=_params(("arbitrary", "arbitrary")),
      name="rank",
  )(eid, triu_bf16)


def _token_row(ref, group, sub):
  return ref.at[group, pl.ds(sub, 1), :]


def _table_row(ref, row):
  return ref.at[pl.ds(row, 1), :]


def _dest_kernel(eid_ref, rank_ref, start_ref, o_ref):
  tn = eid_ref.shape[2]
  expert = lax.broadcasted_iota(I32, (N_EXPERTS, tn), 0)
  rows = []
  for slot in range(TOP_K):
    hot = expert == eid_ref[0, slot:slot + 1, :]
    start = jnp.sum(jnp.where(hot, start_ref[...], 0.0), axis=0, keepdims=True)
    rows.append(start.astype(I32) + rank_ref[0, slot:slot + 1, :])
  o_ref[0] = jnp.concatenate(rows, axis=0)


def _dest(eid, rank, pad_start):
  bsz, _, s = eid.shape
  tn = min(RANK_COLS, s)
  blk = pl.BlockSpec((1, TOP_K, tn), lambda b, i: (b, 0, i))
  dest = pl.pallas_call(
      _dest_kernel,
      grid=(bsz, s // tn),
      in_specs=[blk, blk, pl.BlockSpec((N_EXPERTS, 1), lambda b, i: (0, 0))],
      out_specs=blk,
      out_shape=jax.ShapeDtypeStruct((bsz, TOP_K, s), I32),
      compiler_params=_params(("arbitrary", "arbitrary")),
      name="dest",
  )(eid, rank, pad_start.astype(F32).reshape(N_EXPERTS, 1))
  return jnp.transpose(dest, (0, 2, 1)).reshape(-1)


def _for_each_assignment(n_groups, fn):
  def body(j, carry):
    for u in range(SUBLANES):
      for slot in range(TOP_K):
        fn(j, u, slot, (j * SUBLANES + u) * TOP_K + slot)
    return carry

  lax.fori_loop(0, n_groups, body, 0)


def _scatter_kernel(pad_end_ref, padded_ref, dest_ref, h_ref, buf_ref, zero_scr, sem, zero_sem):
  src = h_ref.at[0]
  n_groups = src.shape[0]

  @pl.when((pl.program_id(0) == 0) & (pl.program_id(1) == 0))
  def _():
    zero_scr[...] = jnp.zeros_like(zero_scr)
    n_blocks = buf_ref.shape[0] // M_BLK
    n_used = pad_end_ref[N_EXPERTS - 1] // M_BLK

    def zero_block(first_row):
      first_row = pl.multiple_of(first_row, M_BLK)
      return pltpu.make_async_copy(zero_scr, buf_ref.at[pl.ds(first_row, M_BLK), :], zero_sem)

    def for_each_zero_block(act):
      def tail(e, carry):
        @pl.when(padded_ref[e] > 0)
        def _():
          act(zero_block(pad_end_ref[e] - M_BLK))
        return carry

      def unused(blk, carry):
        act(zero_block(blk * M_BLK))
        return carry

      lax.fori_loop(0, N_EXPERTS, tail, 0)
      lax.fori_loop(n_used, n_blocks, unused, 0)

    for_each_zero_block(lambda cp: cp.start())
    for_each_zero_block(lambda cp: cp.wait())

  def copy(group, sub, row):
    return pltpu.make_async_copy(_token_row(src, group, sub), _table_row(buf_ref, row), sem)

  _for_each_assignment(n_groups, lambda j, u, slot, idx: copy(j, u, dest_ref[idx]).start())
  _for_each_assignment(n_groups, lambda j, u, slot, idx: copy(0, 0, 0).wait())


def _scatter(pad_end, padded, dest_flat, h2, n_rows):
  bsz, s, d = h2.shape
  tn = min(SCATTER_ROWS, s)
  n_tiles = s // tn
  grid_spec = pltpu.PrefetchScalarGridSpec(
      num_scalar_prefetch=2,
      grid=(bsz, n_tiles),
      in_specs=[pl.BlockSpec((TOP_K * tn,), lambda b, i, pe, pd: (b * n_tiles + i,),
                             memory_space=pltpu.SMEM),
                pl.BlockSpec((1, tn // SUBLANES, SUBLANES, d), lambda b, i, pe, pd: (b, i, 0, 0))],
      out_specs=pl.BlockSpec(memory_space=pl.ANY),
      scratch_shapes=[pltpu.VMEM((M_BLK, d), F32), pltpu.SemaphoreType.DMA(()),
                      pltpu.SemaphoreType.DMA(())],
  )
  return pl.pallas_call(
      _scatter_kernel,
      grid_spec=grid_spec,
      out_shape=jax.ShapeDtypeStruct((n_rows, d), F32),
      compiler_params=_params(("arbitrary", "arbitrary")),
      name="scatter_rows",
  )(pad_end, padded, dest_flat, h2.reshape(bsz, s // SUBLANES, SUBLANES, d))


def _expert_kernel(blk_e_ref, n_used_ref, x_ref, win_ref, wout_ref, o_ref, win_scr, wout_scr):
  i = pl.program_id(0)

  @pl.when(i >= n_used_ref[0])
  def _():
    o_ref[...] = jnp.zeros_like(o_ref)

  @pl.when(i < n_used_ref[0])
  def _():
    changed = (i == 0) | (blk_e_ref[i] != blk_e_ref[jnp.maximum(i - 1, 0)])

    @pl.when(changed)
    def _():
      win_scr[...] = win_ref[0, 0].astype(BF16)
      wout_scr[...] = wout_ref[0, 0].astype(BF16)

    hidden = jnp.dot(x_ref[...].astype(BF16), win_scr[...], preferred_element_type=F32)
    half = hidden.shape[1] // 2
    gate, up = hidden[:, :half], hidden[:, half:]
    act = (gate * _sigmoid(gate)) * up
    o_ref[...] = jnp.dot(act.astype(BF16), wout_scr[...], preferred_element_type=F32)


def _experts(blk_e, n_used, buf, w_e_in, w_e_out, layer):
  n_rows, d = buf.shape
  d_hidden2 = w_e_in.shape[3]
  block = lambda i, be, nu: jnp.minimum(i, nu[0] - 1)
  grid_spec = pltpu.PrefetchScalarGridSpec(
      num_scalar_prefetch=2,
      grid=(n_rows // M_BLK,),
      in_specs=[pl.BlockSpec((M_BLK, d), lambda i, be, nu: (block(i, be, nu), 0)),
                pl.BlockSpec((1, 1, d, d_hidden2),
                             lambda i, be, nu: (layer, be[block(i, be, nu)], 0, 0)),
                pl.BlockSpec((1, 1, d_hidden2 // 2, d),
                             lambda i, be, nu: (layer, be[block(i, be, nu)], 0, 0))],
      out_specs=pl.BlockSpec((M_BLK, d), lambda i, be, nu: (i, 0)),
      scratch_shapes=[pltpu.VMEM((d, d_hidden2), BF16),
                      pltpu.VMEM((d_hidden2 // 2, d), BF16)],
  )
  return pl.pallas_call(
      _expert_kernel,
      grid_spec=grid_spec,
      out_shape=jax.ShapeDtypeStruct((n_rows, d), F32),
      compiler_params=_params(("arbitrary",)),
      name="experts",
  )(blk_e, n_used, buf, w_e_in, w_e_out)


def _combine_kernel(dest_ref, x_ref, wrow_ref, g2_ref, gain_ref, yb_ref,
                    o_ref, y_scr, sem, final_norm):
  tn, d = x_ref.shape[1], x_ref.shape[2]
  n_groups = y_scr.shape[1]

  def copy(row, slot, group, sub):
    return pltpu.make_async_copy(_table_row(yb_ref, row), _token_row(y_scr.at[slot], group, sub),
                                 sem)

  _for_each_assignment(n_groups, lambda j, u, slot, idx: copy(dest_ref[idx], slot, j, u).start())
  _for_each_assignment(n_groups, lambda j, u, slot, idx: copy(0, slot, 0, 0).wait())

  wrow = wrow_ref[0]
  moe = (wrow[:, 0:1] * y_scr[0].reshape(tn, d) + wrow[:, 1:2] * y_scr[1].reshape(tn, d))
  x = x_ref[0] + g2_ref[0] * moe
  if final_norm:
    x = (x * lax.rsqrt(jnp.mean(x * x, axis=-1, keepdims=True) + NORM_EPS)) * gain_ref[...]
  o_ref[0] = x


def _combine(dest_flat, x, wrow, g2, final_gain, yb, final_norm):
  bsz, s, d = x.shape
  tn = min(COMBINE_ROWS, s)
  n_tiles = s // tn
  row = pl.BlockSpec((1, tn, d), lambda b, i: (b, i, 0))
  return pl.pallas_call(
      functools.partial(_combine_kernel, final_norm=final_norm),
      grid=(bsz, n_tiles),
      in_specs=[pl.BlockSpec((TOP_K * tn,), lambda b, i: (b * n_tiles + i,),
                             memory_space=pltpu.SMEM),
                row,
                pl.BlockSpec((1, tn, LANES), lambda b, i: (b, i, 0)),
                pl.BlockSpec((1, 1, d), lambda b, i: (b, 0, 0)),
                pl.BlockSpec((1, d), lambda b, i: (0, 0)),
                pl.BlockSpec(memory_space=pl.ANY)],
      out_specs=row,
      scratch_shapes=[pltpu.VMEM((TOP_K, tn // SUBLANES, SUBLANES, d), F32),
                      pltpu.SemaphoreType.DMA(())],
      out_shape=jax.ShapeDtypeStruct((bsz, s, d), F32),
      compiler_params=_params(("arbitrary", "arbitrary")),
      name="combine",
  )(dest_flat, x, wrow, g2, final_gain.reshape(1, d), yb)


def _pad_rows(w, offset, total):
  return jnp.pad(w.astype(BF16), ((offset, total - offset - w.shape[0]), (0, 0)))


def kernel(x, c, w_ada, b_ada, norm1_gain, norm2_gain, w_in, token_mu, conv_w, w0, decay_up,
           a0, aaa_up, gate_up, k_k, k_a, r_k, ln_x_w, ln_x_b, vmix_down, vmix_up, v0, w_out,
           w_router, router_bias, w_expert_in, w_expert_out, final_gain):
  bsz, s, d = x.shape
  depth = w_in.shape[0]
  n_tok = bsz * s
  n_rows = n_tok * TOP_K + N_EXPERTS * M_BLK

  ada = _ada(c, w_ada, b_ada)
  mods = ada.reshape(depth, bsz, 6, 1, d)

  lane = jnp.arange(LANES)
  bd_ones = (lane[:, None] // HEAD_DIM == lane[None, :] // HEAD_DIM).astype(BF16)
  t = jnp.arange(WKV_CHUNK)
  tri = (t[:, None] >= t[None, :]).astype(BF16)
  tc = jnp.arange(min(RANK_COLS, s))
  triu = (tc[:, None] < tc[None, :]).astype(BF16)

  perm = (jnp.arange(N_EXPERTS) % N_GROUPS) * EXPERTS_PER_GROUP + jnp.arange(N_EXPERTS) // N_GROUPS
  w_router_t = w_router.T[perm].astype(BF16)
  router_bias_col = router_bias[perm].reshape(N_EXPERTS, 1)

  p_first_layer = None
  for l in range(depth):
    sh1, sc1, g1, sh2, sc2, g2 = (mods[l, :, j] for j in range(6))
    has_vmix = l > 0
    vmix_down_pad = None
    if has_vmix:
      vmix_down_pad = jnp.zeros((d, LANES), BF16).at[:, :VMIX_RANK].set(
          vmix_down[l - 1].astype(BF16))
    p, hv = _inproj(x, norm1_gain[l], sc1, sh1, w_in[l].astype(BF16), vmix_down_pad)

    n_shift = token_mu.shape[1]
    mu3 = token_mu[l, :3 * d].reshape(3, d)
    mu_lr = token_mu[l, 3 * d:].reshape(1, n_shift - 3 * d)
    vecs = jnp.stack([w0[l], a0[l], k_k[l], k_a[l]])
    if not has_vmix:
      p_first_layer = p
    merged = _mixer(
        p, p_first_layer, hv, mu3, mu_lr, conv_w[l], vecs,
        _pad_rows(decay_up[l], 0, LOW_RANK),
        _pad_rows(aaa_up[l], DECAY_RANK, LOW_RANK),
        _pad_rows(gate_up[l], DECAY_RANK + AAA_RANK, LOW_RANK),
        bd_ones,
        _pad_rows(vmix_up[l - 1], 0, LANES) if has_vmix else None,
        v0[l - 1].reshape(1, d) if has_vmix else None,
        token_mu[0, 2 * d:3 * d].reshape(1, d) if has_vmix else None,
        tri, ln_x_w[l], ln_x_b[l], r_k[l])
    x, h2, eid, wrow = _outproj(merged, x, w_out[l].astype(BF16), g1, norm2_gain[l], sc2, sh2,
                                w_router_t, router_bias_col)

    rank, sizes = _rank(eid, triu)
    sizes = sizes[:, 0]
    padded = (sizes + M_BLK - 1) // M_BLK * M_BLK
    pad_end = jnp.cumsum(padded)
    pad_start = (pad_end - padded).astype(I32)
    blk_first_row = jnp.arange(n_rows // M_BLK, dtype=I32) * M_BLK
    blk_e = jnp.minimum(
        jnp.sum((pad_end[None, :] <= blk_first_row[:, None]).astype(I32), axis=1),
        N_EXPERTS - 1)

    dest_flat = _dest(eid, rank, pad_start)
    buf = _scatter(pad_end.astype(I32), padded.astype(I32), dest_flat, h2, n_rows)
    n_used = (pad_end[-1:] // M_BLK).astype(I32)
    yb = _experts(blk_e, n_used, buf, w_expert_in, w_expert_out, l)
    x = _combine(dest_flat, x, wrow, g2, final_gain, yb, final_norm=(l == depth - 1))
  return x
```

```python
import functools
import itertools
import math

import jax
import jax.numpy as jnp
from jax import lax
from jax.experimental import pallas as pl
from jax.experimental.pallas import tpu as pltpu

F32 = jnp.float32
BF16 = jnp.bfloat16
I32 = jnp.int32

HEAD_DIM = 64
CONV_WIDTH = 3
DECAY_RANK = 64
AAA_RANK = 64
GATE_RANK = 128
LOW_RANK = DECAY_RANK + AAA_RANK + GATE_RANK
VMIX_RANK = 32
N_EXPERTS = 32
N_GROUPS = 8
EXPERTS_PER_GROUP = N_EXPERTS // N_GROUPS
TOP_K = 2
M_BLK = 512
NORM_EPS = 1e-6
LN_X_EPS = 64e-5
LOG2_E = 1.4426950408889634

LANES = 128
SUBLANES = 8
HEADS_PER_VREG = LANES // HEAD_DIM
VMEM_LIMIT_BYTES = 56 * 1024 * 1024

WKV_CHUNK = 64
WKV_STAGGER = 2
PREPARE_DELAY = 8
MIXER_TILE = 128
INPROJ_ROWS = 1024
INPROJ_COLS = 1408
OUTPROJ_ROWS = 256
RANK_COLS = 512
SCATTER_ROWS = 512
COMBINE_ROWS = 512


def _dot_bf16(a, b, dims=(((1,), (0,)), ((), ()))):
  return lax.dot_general(a.astype(BF16), b.astype(BF16), dims,
                         preferred_element_type=F32)


_NT = (((1,), (1,)), ((), ()))
_TN = (((0,), (0,)), ((), ()))


def _sigmoid(x):
  return 1.0 / (1.0 + jnp.exp(-x))


def _params(semantics):
  return pltpu.CompilerParams(dimension_semantics=semantics,
                              vmem_limit_bytes=VMEM_LIMIT_BYTES)


def _ada_kernel(c_ref, w_ref, b_ref, o_ref):
  c = c_ref[...]
  cond = c * _sigmoid(c)
  o_ref[0] = _dot_bf16(cond, w_ref[0]) + b_ref[0]


def _ada(c, w_ada, b_ada):
  depth, d, n = w_ada.shape
  bsz = c.shape[0]
  tn = d
  return pl.pallas_call(
      _ada_kernel,
      grid=(depth, n // tn),
      in_specs=[
          pl.BlockSpec((bsz, d), lambda l, j: (0, 0)),
          pl.BlockSpec((1, d, tn), lambda l, j: (l, 0, j)),
          pl.BlockSpec((1, 1, tn), lambda l, j: (l, 0, j)),
      ],
      out_specs=pl.BlockSpec((1, bsz, tn), lambda l, j: (l, 0, j)),
      out_shape=jax.ShapeDtypeStruct((depth, bsz, n), F32),
      compiler_params=_params(("arbitrary", "arbitrary")),
      name="ada",
  )(c, w_ada, b_ada.reshape(depth, 1, n))


def _modulated_norm(x, gain, scale, shift):
  y = x * lax.rsqrt(jnp.mean(x * x, axis=-1, keepdims=True) + NORM_EPS)
  return (y * gain) * (1.0 + scale) + shift


def _inproj_kernel(x_ref, gain_ref, sc_ref, sh_ref, w_ref, *rest, has_vmix):
  if has_vmix:
    wv_ref, p_ref, hv_ref, h_scr = rest
  else:
    p_ref, h_scr = rest

  @pl.when(pl.program_id(2) == 0)
  def _():
    h = _modulated_norm(x_ref[0], gain_ref[...], sc_ref[0], sh_ref[0])
    hb = h.astype(BF16)
    h_scr[...] = hb
    if has_vmix:
      hv_ref[0] = jnp.dot(hb, wv_ref[...], preferred_element_type=F32)

  p_ref[0] = jnp.dot(h_scr[...], w_ref[...], preferred_element_type=F32)


def _inproj(x, gain, sc, sh, w_bf16, vmix_down_bf16):
  bsz, s, d = x.shape
  n = w_bf16.shape[1]
  tm = min(INPROJ_ROWS, s)
  tn = INPROJ_COLS
  has_vmix = vmix_down_bf16 is not None
  in_specs = [
      pl.BlockSpec((1, tm, d), lambda b, i, j: (b, i, 0)),
      pl.BlockSpec((1, d), lambda b, i, j: (0, 0)),
      pl.BlockSpec((1, 1, d), lambda b, i, j: (b, 0, 0)),
      pl.BlockSpec((1, 1, d), lambda b, i, j: (b, 0, 0)),
      pl.BlockSpec((d, tn), lambda b, i, j: (0, j)),
  ]
  args = [x, gain.reshape(1, d), sc, sh, w_bf16]
  out_specs = [pl.BlockSpec((1, tm, tn), lambda b, i, j: (b, i, j))]
  out_shape = [jax.ShapeDtypeStruct((bsz, s, n), F32)]
  if has_vmix:
    in_specs.append(pl.BlockSpec((d, LANES), lambda b, i, j: (0, 0)))
    args.append(vmix_down_bf16)
    out_specs.append(pl.BlockSpec((1, tm, LANES), lambda b, i, j: (b, i, 0)))
    out_shape.append(jax.ShapeDtypeStruct((bsz, s, LANES), F32))
  outs = pl.pallas_call(
      functools.partial(_inproj_kernel, has_vmix=has_vmix),
      grid=(bsz, s // tm, n // tn),
      in_specs=in_specs,
      out_specs=out_specs,
      out_shape=out_shape,
      scratch_shapes=[pltpu.VMEM((tm, d), BF16)],
      compiler_params=_params(("arbitrary", "arbitrary", "arbitrary")),
      name="inproj",
  )(*args)
  return outs if has_vmix else (outs[0], None)


def _split_bf16(x, terms):
  parts = []
  for _ in range(terms - 1):
    part = x.astype(BF16)
    parts.append(part)
    x = x - part.astype(F32)
  parts.append(x.astype(BF16))
  return parts


def _head_sum(x, bd_ones):
  out = [jnp.dot(x[:, j * LANES:(j + 1) * LANES].astype(BF16), bd_ones,
                 preferred_element_type=F32) for j in range(x.shape[1] // LANES)]
  return out[0] if len(out) == 1 else jnp.concatenate(out, axis=1)


def _shift_rows(cur, prev_tail, n):
  rolled = pltpu.roll(cur, n, axis=0)
  head = rolled[:SUBLANES]
  row = lax.broadcasted_iota(I32, head.shape, 0)
  for j in range(n):
    head = jnp.where(row == j, prev_tail[SUBLANES - n + j:SUBLANES - n + j + 1, :], head)
  return jnp.concatenate([head, rolled[SUBLANES:]], axis=0)


_R, _K, _V, _LW, _KK, _BETA, _G, _MCONV, _SGB = range(9)
_N_PREPARED = 9


def _mixer_kernel(*refs, has_vmix):
  assert WKV_CHUNK == HEAD_DIM and HEADS_PER_VREG * WKV_CHUNK == LANES
  refs = list(refs)
  p_tiles = refs[:3]
  refs = refs[3:]
  if has_vmix:
    hv_tiles, pv_tiles = refs[:3], refs[3:6]
    refs = refs[6:]
  else:
    hv_tiles = pv_tiles = (None,) * 3
  mu_ref, mulr_ref, convw_ref, vec_ref, dup_ref, aup_ref, gup_ref, bd_ref = refs[:8]
  refs = refs[8:]
  if has_vmix:
    vup_ref, v0_ref, muvf_ref = refs[:3]
    refs = refs[3:]
  tri_ref, lnw_ref, lnb_ref, rk_ref, o_ref = refs[:5]
  buf_a, buf_b, pr_scr, pk_scr, pv_scr, plr_scr, pu_scr, pvf_scr, s_scr = refs[5:]

  c = WKV_CHUNK
  d = o_ref.shape[2]
  tile = buf_a.shape[1]
  n_pairs = d // LANES
  step = pl.program_id(1)
  bd_ones = bd_ref[...]
  tri = tri_ref[...]

  tail = slice(tile - SUBLANES, tile)

  def token_shift(cur, scr, lanes, mu):
    prev = _shift_rows(cur, scr[:, lanes], 1)
    scr[:, lanes] = cur[tail, :]
    return cur + (prev - cur) * mu

  def prepare(p_ref, hv_ref, pv_ref, dst):
    lr = token_shift(p_ref[0, :, 8 * d:8 * d + LOW_RANK], plr_scr, slice(None), mulr_ref[...])
    tanh_lr = jnp.tanh(lr).astype(BF16)
    sig_lr = _sigmoid(lr).astype(BF16)
    lr = lr.astype(BF16)
    if has_vmix:
      hv = hv_ref[0].astype(BF16)
    for _ in range(PREPARE_DELAY):
      yield

    for j in range(n_pairs):
      ln = slice(j * LANES, (j + 1) * LANES)
      col = lambda g: p_ref[0, :, g * d + j * LANES:g * d + (j + 1) * LANES]
      w0, a0, k_k, k_a = (vec_ref[i:i + 1, ln] for i in range(4))

      w_pre = w0 + jnp.dot(tanh_lr, dup_ref[:, ln], preferred_element_type=F32)
      dst[_LW, :, ln] = (-LOG2_E * math.exp(-0.5)) * _sigmoid(w_pre)
      a = _sigmoid(a0 + jnp.dot(lr, aup_ref[:, ln], preferred_element_type=F32))
      dst[_G, :, ln] = jnp.dot(sig_lr, gup_ref[:, ln], preferred_element_type=F32)

      dst[_R, :, ln] = token_shift(col(5), pr_scr, ln, mu_ref[0:1, ln])
      k = token_shift(col(6), pk_scr, ln, mu_ref[1:2, ln])
      kk = k * k_k
      kk = kk / jnp.maximum(jnp.sqrt(_head_sum(kk * kk, bd_ones)), 1e-12)
      dst[_K, :, ln] = k * (1.0 + (a - 1.0) * k_a)
      dst[_KK, :, ln] = kk
      dst[_BETA, :, ln] = a * kk

      v = token_shift(col(7), pv_scr, ln, mu_ref[2:3, ln])
      if has_vmix:
        v_first = token_shift(pv_ref[0, :, ln], pvf_scr, ln, muvf_ref[:, ln])
        mix = _sigmoid(v0_ref[:, ln]
                       + jnp.dot(hv, vup_ref[:, ln], preferred_element_type=F32))
        v = v + (v_first - v) * mix
      dst[_V, :, ln] = v

      u = col(1) * col(2)
      prev_u = pu_scr[:, ln]
      conv = (convw_ref[0:1, ln] * _shift_rows(u, prev_u, 2)
              + convw_ref[1:2, ln] * _shift_rows(u, prev_u, 1)
              + convw_ref[2:3, ln] * u)
      pu_scr[:, ln] = u[tail, :]
      dst[_MCONV, :, ln] = _sigmoid(col(3)) * (col(0) * conv)
      dst[_SGB, :, ln] = _sigmoid(col(4))
      yield

  lane = lax.broadcasted_iota(I32, (c, LANES), 1)
  head_masks = [(lane // HEAD_DIM == h).astype(BF16) for h in range(HEADS_PER_VREG)]
  t_row = lax.broadcasted_iota(I32, (c, LANES), 0)
  t_col = lane % c
  strict = t_row > t_col
  incl = t_row >= t_col
  eye = (t_row == t_col).astype(F32)
  same_head = (lax.broadcasted_iota(I32, (LANES, LANES), 0) // HEAD_DIM
               == lax.broadcasted_iota(I32, (LANES, LANES), 1) // HEAD_DIM)
  doubling_steps = c.bit_length() - 2

  def stack(xb):
    return jnp.concatenate([xb * m for m in head_masks], axis=0)

  def mm(a, b, dims=(((1,), (0,)), ((), ()))):
    return lax.dot_general(a, b, dims, preferred_element_type=F32)

  def cumulative_log_decay(src, rows):
    lw_terms = jnp.concatenate(_split_bf16(src[_LW, rows, :], 3), axis=1)
    cs = jnp.dot(tri, lw_terms, preferred_element_type=F32)
    return cs[:, :d] + cs[:, d:2 * d] + cs[:, 2 * d:]

  def pair_chunk(p, src, rows, out_rows, cum, delay):
    for _ in range(delay):
      yield
    ln = slice(p * LANES, (p + 1) * LANES)
    r = src[_R, rows, ln]
    k = src[_K, rows, ln]
    v = src[_V, rows, ln]
    lw = src[_LW, rows, ln]
    kk = src[_KK, rows, ln]
    beta = src[_BETA, rows, ln]

    cum_last = cum[c - 1:c, :]
    inv_p = jnp.exp2(-cum)
    rem_p = jnp.exp2(cum_last - cum)
    rt = (r * jnp.exp2(cum)).astype(BF16)
    at = (kk * jnp.exp2(cum - lw)).astype(BF16)
    kh_s = stack((k * inv_p).astype(BF16))
    bh_s = stack((beta * inv_p).astype(BF16))
    kb = (k * rem_p).astype(BF16)
    bb_neg = (-(beta * rem_p)).astype(BF16)
    vb = v.astype(BF16)
    v_s = stack(vb)
    yield

    sc = mm(jnp.concatenate([at, rt], axis=0), jnp.concatenate([kh_s, bh_s], axis=0), _NT)
    m_k = jnp.where(strict, sc[:c, :LANES], 0.0).astype(BF16)
    q_pow = jnp.where(strict, -sc[:c, LANES:], 0.0)
    q_k = jnp.where(incl, sc[c:, :LANES], 0.0).astype(BF16)
    q_b = jnp.where(incl, sc[c:, LANES:], 0.0).astype(BF16)
    yield

    w_inv = eye + q_pow
    q_pow = q_pow.astype(BF16)
    kv = mm(jnp.concatenate([m_k, q_k], axis=0), v_s)
    q_pow = mm(q_pow, stack(q_pow)).astype(BF16)
    yield
    for i in range(doubling_steps):
      w_s = stack(w_inv.astype(BF16))
      if i < doubling_steps - 1:
        prod = mm(q_pow, jnp.concatenate([w_s, stack(q_pow)], axis=1))
        q_pow = prod[:, LANES:].astype(BF16)
      else:
        prod = mm(q_pow, w_s)
      w_inv = w_inv + prod[:, :LANES]
      yield

    wa = mm(w_inv.astype(BF16),
            jnp.concatenate([stack(at), stack(kv[:c].astype(BF16))], axis=1))
    a2 = wa[:, :LANES].astype(BF16)
    v2 = wa[:, LANES:]
    yield

    state = s_scr[p]
    res = mm(jnp.concatenate([a2, rt], axis=0), state.astype(BF16), _NT)
    u = res[:c] + v2
    ub = u.astype(BF16)
    yield
    y = res[c:] + kv[c:] - mm(q_b, stack(ub))
    update = mm(jnp.concatenate([vb, ub], axis=0), jnp.concatenate([kb, bb_neg], axis=0), _TN)
    s_scr[p] = state * jnp.exp2(cum_last) + jnp.where(same_head, update, 0.0)
    yield

    inv_n = 1.0 / HEAD_DIM
    sums = _head_sum(jnp.concatenate([y, r * k * rk_ref[:, ln]], axis=0), bd_ones)
    yield
    yc = y - sums[:c] * inv_n
    var = _head_sum(yc * yc, bd_ones) * inv_n
    yield
    yn = yc * lax.rsqrt(var + LN_X_EPS) * lnw_ref[:, ln] + lnb_ref[:, ln]
    y_rwkv = (yn + sums[c:] * v) * src[_G, rows, ln]
    o_ref[0, out_rows, ln] = src[_MCONV, rows, ln] + src[_SGB, rows, ln] * y_rwkv

  def run_round_robin(chains):
    for _ in itertools.zip_longest(*chains):
      pass

  def slot(src, out_row0, p_ref, hv_ref, pv_ref, dst):
    chains = [prepare(p_ref, hv_ref, pv_ref, dst)]
    for j in range(tile // c):
      rows = slice(j * c, (j + 1) * c)
      out_rows = slice(out_row0 + j * c, out_row0 + (j + 1) * c)
      cum = cumulative_log_decay(src, rows)
      chains += [pair_chunk(p, src, rows, out_rows, cum[:, p * LANES:(p + 1) * LANES],
                            j * WKV_STAGGER) for p in range(n_pairs)]
    run_round_robin(chains)

  @pl.when(step == 0)
  def _():
    for scr in (pr_scr, pk_scr, pv_scr, plr_scr, pu_scr, pvf_scr, s_scr):
      scr[...] = jnp.zeros_like(scr)
    run_round_robin([prepare(p_tiles[0], hv_tiles[0], pv_tiles[0], buf_a)])

  slot(buf_a, 0, p_tiles[1], hv_tiles[1], pv_tiles[1], buf_b)
  slot(buf_b, tile, p_tiles[2], hv_tiles[2], pv_tiles[2], buf_a)


def _mixer(p, p_first_layer, hv, mu3, mu_lr, conv_w, vecs, decay_up_pad, aaa_up_pad, gate_up_pad,
           bd_ones, vmix_up_pad, v0, mu_v_first, tri, ln_w, ln_b, r_k):
  bsz, s, n_in = p.shape
  d = conv_w.shape[1]
  tile = MIXER_TILE
  n_tiles = s // tile
  has_vmix = hv is not None
  tile_maps = [lambda b, i: (b, 0, 0),
               lambda b, i: (b, 2 * i + 1, 0),
               lambda b, i: (b, jnp.minimum(2 * i + 2, n_tiles - 1), 0)]
  full = lambda a: pl.BlockSpec(a.shape, lambda b, i: (0,) * a.ndim)
  vec = pl.BlockSpec((1, d), lambda b, i: (0, 0))
  in_specs = [pl.BlockSpec((1, tile, n_in), m) for m in tile_maps]
  args = [p] * 3
  if has_vmix:
    in_specs += [pl.BlockSpec((1, tile, LANES), m) for m in tile_maps]
    v_col = 7
    in_specs += [pl.BlockSpec((1, tile, d), lambda b, i, m=m: m(b, i)[:2] + (v_col,))
                 for m in tile_maps]
    args += [hv] * 3 + [p_first_layer] * 3
  consts = [mu3, mu_lr, conv_w, vecs, decay_up_pad, aaa_up_pad, gate_up_pad, bd_ones]
  if has_vmix:
    consts += [vmix_up_pad, v0, mu_v_first]
  consts += [tri]
  in_specs += [full(a) for a in consts] + [vec, vec, vec]
  args += consts + [ln_w.reshape(1, d), ln_b.reshape(1, d), r_k.reshape(1, d)]
  return pl.pallas_call(
      functools.partial(_mixer_kernel, has_vmix=has_vmix),
      grid=(bsz, n_tiles // 2),
      in_specs=in_specs,
      out_specs=pl.BlockSpec((1, 2 * tile, d), lambda b, i: (b, i, 0)),
      out_shape=jax.ShapeDtypeStruct((bsz, s, d), F32),
      scratch_shapes=[pltpu.VMEM((_N_PREPARED, tile, d), F32)] * 2
      + [pltpu.VMEM((SUBLANES, d), F32)] * 3
      + [pltpu.VMEM((SUBLANES, LOW_RANK), F32)]
      + [pltpu.VMEM((SUBLANES, d), F32)] * 2
      + [pltpu.VMEM((d // LANES, LANES, LANES), F32)],
      compiler_params=_params(("arbitrary", "arbitrary")),
      name="mixer",
  )(*args)


def _outproj_kernel(m_ref, x_ref, w_ref, g1_ref, gain_ref, sc_ref, sh_ref, wr_ref, rb_ref,
                    xo_ref, h2_ref, eid_ref, wrow_ref):
  mix = jnp.dot(m_ref[0].astype(BF16), w_ref[...], preferred_element_type=F32)
  x = x_ref[0] + g1_ref[0] * mix
  xo_ref[0] = x
  h2 = _modulated_norm(x, gain_ref[...], sc_ref[0], sh_ref[0])
  h2_ref[0] = h2

  logits = _dot_bf16(wr_ref[...], h2, _NT)
  tm = logits.shape[1]
  epg, ng = EXPERTS_PER_GROUP, N_GROUPS
  aff = [_sigmoid(logits[j * ng:(j + 1) * ng, :]) for j in range(epg)]
  sel = [aff[j] + rb_ref[j * ng:(j + 1) * ng, :] for j in range(epg)]
  hi01, lo01 = jnp.maximum(sel[0], sel[1]), jnp.minimum(sel[0], sel[1])
  hi23, lo23 = jnp.maximum(sel[2], sel[3]), jnp.minimum(sel[2], sel[3])
  gscore = jnp.maximum(hi01, hi23) + jnp.maximum(jnp.minimum(hi01, hi23),
                                                 jnp.maximum(lo01, lo23))
  gid = lax.broadcasted_iota(I32, (ng, tm), 0)
  gmax = jnp.max(gscore, axis=0, keepdims=True)
  gsel = jnp.min(jnp.where(gscore == gmax, gid, ng), axis=0, keepdims=True)
  in_g = gid == gsel
  cand = [jnp.sum(jnp.where(in_g, sel[j], 0.0), axis=0, keepdims=True) for j in range(epg)]
  affc = [jnp.sum(jnp.where(in_g, aff[j], 0.0), axis=0, keepdims=True) for j in range(epg)]

  def first_argmax(vals):
    best = vals[0]
    for val in vals[1:]:
      best = jnp.maximum(best, val)
    idx = jnp.full(best.shape, epg, I32)
    for j in reversed(range(epg)):
      idx = jnp.where(vals[j] == best, j, idx)
    return idx

  i1 = first_argmax(cand)
  i2 = first_argmax([jnp.where(i1 == j, -jnp.inf, cand[j]) for j in range(epg)])
  pick = lambda idx: sum(jnp.where(idx == j, affc[j], 0.0) for j in range(epg))
  a1, a2 = pick(i1), pick(i2)
  denom = a1 + a2
  eid_ref[0] = jnp.concatenate([gsel * epg + i1, gsel * epg + i2], axis=0)
  wpad = jnp.concatenate([a1 / denom, a2 / denom, jnp.zeros((LANES - TOP_K, tm), F32)], axis=0)
  wrow_ref[0] = wpad.T


def _outproj(merged, x, w_out_bf16, g1, gain2, sc2, sh2, w_router_t, router_bias_col):
  bsz, s, d = x.shape
  tm = min(OUTPROJ_ROWS, s)
  row = pl.BlockSpec((1, tm, d), lambda b, i: (b, i, 0))
  mod = pl.BlockSpec((1, 1, d), lambda b, i: (b, 0, 0))
  full = lambda a: pl.BlockSpec(a.shape, lambda b, i: (0,) * a.ndim)
  gain2 = gain2.reshape(1, d)
  return pl.pallas_call(
      _outproj_kernel,
      grid=(bsz, s // tm),
      in_specs=[row, row, full(w_out_bf16), mod, full(gain2), mod, mod,
                full(w_router_t), full(router_bias_col)],
      out_specs=[row, row,
                 pl.BlockSpec((1, TOP_K, tm), lambda b, i: (b, 0, i)),
                 pl.BlockSpec((1, tm, LANES), lambda b, i: (b, i, 0))],
      out_shape=[jax.ShapeDtypeStruct((bsz, s, d), F32),
                 jax.ShapeDtypeStruct((bsz, s, d), F32),
                 jax.ShapeDtypeStruct((bsz, TOP_K, s), I32),
                 jax.ShapeDtypeStruct((bsz, s, LANES), F32)],
      compiler_params=_params(("arbitrary", "arbitrary")),
      name="outproj_router",
  )(merged, x, w_out_bf16, g1, gain2, sc2, sh2, w_router_t, router_bias_col)


def _rank_kernel(eid_ref, triu_ref, rank_ref, size_ref, carry_scr):
  @pl.when((pl.program_id(0) == 0) & (pl.program_id(1) == 0))
  def _():
    carry_scr[...] = jnp.zeros_like(carry_scr)

  tn = eid_ref.shape[2]
  eid = eid_ref[0]
  expert = lax.broadcasted_iota(I32, (N_EXPERTS, tn), 0)
  hot = [(expert == eid[slot:slot + 1, :]).astype(F32) for slot in range(TOP_K)]
  both = hot[0] + hot[1]
  before = jnp.dot(both.astype(BF16), triu_ref[...], preferred_element_type=F32)
  carry = carry_scr[...]
  count = before + jnp.concatenate([carry] * (tn // LANES), axis=1)
  rank_ref[0] = jnp.concatenate(
      [jnp.sum(hot[slot] * count, axis=0, keepdims=True) for slot in range(TOP_K)],
      axis=0).astype(I32)
  total = carry + jnp.dot(both.astype(BF16), jnp.ones((tn, LANES), BF16),
                          preferred_element_type=F32)
  carry_scr[...] = total
  size_ref[...] = total.astype(I32)


def _rank(eid, triu_bf16):
  bsz, _, s = eid.shape
  tn = min(RANK_COLS, s)
  return pl.pallas_call(
      _rank_kernel,
      grid=(bsz, s // tn),
      in_specs=[pl.BlockSpec((1, TOP_K, tn), lambda b, i: (b, 0, i)),
                pl.BlockSpec((tn, tn), lambda b, i: (0, 0))],
      out_specs=[pl.BlockSpec((1, TOP_K, tn), lambda b, i: (b, 0, i)),
                 pl.BlockSpec((N_EXPERTS, LANES), lambda b, i: (0, 0))],
      out_shape=[jax.ShapeDtypeStruct((bsz, TOP_K, s), I32),
                 jax.ShapeDtypeStruct((N_EXPERTS, LANES), I32)],
      scratch_shapes=[pltpu.VMEM((N_EXPERTS, LANES), F32)],
      compiler_params=_params(("arbitrary", "arbitrary")),
      name="rank",
  )(eid, triu_bf16)


def _token_row(ref, group, sub):
  return ref.at[group, pl.ds(sub, 1), :]


def _table_row(ref, row):
  return ref.at[pl.ds(row, 1), :]


def _dest_kernel(eid_ref, rank_ref, start_ref, o_ref):
  tn = eid_ref.shape[2]
  expert = lax.broadcasted_iota(I32, (N_EXPERTS, tn), 0)
  rows = []
  for slot in range(TOP_K):
    hot = expert == eid_ref[0, slot:slot + 1, :]
    start = jnp.sum(jnp.where(hot, start_ref[...], 0.0), axis=0, keepdims=True)
    rows.append(start.astype(I32) + rank_ref[0, slot:slot + 1, :])
  o_ref[0] = jnp.concatenate(rows, axis=0)


def _dest(eid, rank, pad_start):
  bsz, _, s = eid.shape
  tn = min(RANK_COLS, s)
  blk = pl.BlockSpec((1, TOP_K, tn), lambda b, i: (b, 0, i))
  dest = pl.pallas_call(
      _dest_kernel,
      grid=(bsz, s // tn),
      in_specs=[blk, blk, pl.BlockSpec((N_EXPERTS, 1), lambda b, i: (0, 0))],
      out_specs=blk,
      out_shape=jax.ShapeDtypeStruct((bsz, TOP_K, s), I32),
      compiler_params=_params(("arbitrary", "arbitrary")),
      name="dest",
  )(eid, rank, pad_start.astype(F32).reshape(N_EXPERTS, 1))
  return jnp.transpose(dest, (0, 2, 1)).reshape(-1)


def _for_each_assignment(n_groups, fn):
  def body(j, carry):
    for u in range(SUBLANES):
      for slot in range(TOP_K):
        fn(j, u, slot, (j * SUBLANES + u) * TOP_K + slot)
    return carry

  lax.fori_loop(0, n_groups, body, 0)


def _scatter_kernel(pad_end_ref, padded_ref, dest_ref, h_ref, buf_ref, zero_scr, sem, zero_sem):
  src = h_ref.at[0]
  n_groups = src.shape[0]

  @pl.when((pl.program_id(0) == 0) & (pl.program_id(1) == 0))
  def _():
    zero_scr[...] = jnp.zeros_like(zero_scr)
    n_blocks = buf_ref.shape[0] // M_BLK
    n_used = pad_end_ref[N_EXPERTS - 1] // M_BLK

    def zero_block(first_row):
      first_row = pl.multiple_of(first_row, M_BLK)
      return pltpu.make_async_copy(zero_scr, buf_ref.at[pl.ds(first_row, M_BLK), :], zero_sem)

    def for_each_zero_block(act):
      def tail(e, carry):
        @pl.when(padded_ref[e] > 0)
        def _():
          act(zero_block(pad_end_ref[e] - M_BLK))
        return carry

      def unused(blk, carry):
        act(zero_block(blk * M_BLK))
        return carry

      lax.fori_loop(0, N_EXPERTS, tail, 0)
      lax.fori_loop(n_used, n_blocks, unused, 0)

    for_each_zero_block(lambda cp: cp.start())
    for_each_zero_block(lambda cp: cp.wait())

  def copy(group, sub, row):
    return pltpu.make_async_copy(_token_row(src, group, sub), _table_row(buf_ref, row), sem)

  _for_each_assignment(n_groups, lambda j, u, slot, idx: copy(j, u, dest_ref[idx]).start())
  _for_each_assignment(n_groups, lambda j, u, slot, idx: copy(0, 0, 0).wait())


def _scatter(pad_end, padded, dest_flat, h2, n_rows):
  bsz, s, d = h2.shape
  tn = min(SCATTER_ROWS, s)
  n_tiles = s // tn
  grid_spec = pltpu.PrefetchScalarGridSpec(
      num_scalar_prefetch=2,
      grid=(bsz, n_tiles),
      in_specs=[pl.BlockSpec((TOP_K * tn,), lambda b, i, pe, pd: (b * n_tiles + i,),
                             memory_space=pltpu.SMEM),
                pl.BlockSpec((1, tn // SUBLANES, SUBLANES, d), lambda b, i, pe, pd: (b, i, 0, 0))],
      out_specs=pl.BlockSpec(memory_space=pl.ANY),
      scratch_shapes=[pltpu.VMEM((M_BLK, d), F32), pltpu.SemaphoreType.DMA(()),
                      pltpu.SemaphoreType.DMA(())],
  )
  return pl.pallas_call(
      _scatter_kernel,
      grid_spec=grid_spec,
      out_shape=jax.ShapeDtypeStruct((n_rows, d), F32),
      compiler_params=_params(("arbitrary", "arbitrary")),
      name="scatter_rows",
  )(pad_end, padded, dest_flat, h2.reshape(bsz, s // SUBLANES, SUBLANES, d))


def _expert_kernel(blk_e_ref, next_e_ref, slot_ref, n_used_ref, x_ref, win_hbm, wout_hbm, o_ref,
                   win_buf, wout_buf, win_scr, wout_scr, win_sem, wout_sem, *, layer):
  i = pl.program_id(0)

  @pl.when(i >= n_used_ref[0])
  def _():
    o_ref[...] = jnp.zeros_like(o_ref)

  def weight_copies(expert, slot):
    return (pltpu.make_async_copy(win_hbm.at[layer, expert], win_buf.at[slot], win_sem.at[slot]),
            pltpu.make_async_copy(wout_hbm.at[layer, expert], wout_buf.at[slot],
                                  wout_sem.at[slot]))

  @pl.when(i < n_used_ref[0])
  def _():
    expert, slot = blk_e_ref[i], slot_ref[i]

    @pl.when(i == 0)
    def _():
      for cp in weight_copies(expert, slot):
        cp.start()

    @pl.when((i == 0) | (expert != blk_e_ref[jnp.maximum(i - 1, 0)]))
    def _():
      for cp in weight_copies(expert, slot):
        cp.wait()
      win_scr[...] = win_buf[slot].astype(BF16)
      wout_scr[...] = wout_buf[slot].astype(BF16)

      @pl.when(next_e_ref[i] >= 0)
      def _():
        for cp in weight_copies(next_e_ref[i], 1 - slot):
          cp.start()

    hidden = jnp.dot(x_ref[...].astype(BF16), win_scr[...], preferred_element_type=F32)
    half = hidden.shape[1] // 2
    gate, up = hidden[:, :half], hidden[:, half:]
    act = (gate * _sigmoid(gate)) * up
    o_ref[...] = jnp.dot(act.astype(BF16), wout_scr[...], preferred_element_type=F32)


def _experts(blk_e, n_used, buf, w_e_in, w_e_out, layer):
  n_rows, d = buf.shape
  d_hidden2 = w_e_in.shape[3]
  n_blocks = n_rows // M_BLK
  used = jnp.arange(n_blocks) < n_used[0]
  n_upto = jnp.sum((blk_e[None, :] <= blk_e[:, None]) & used[None, :], axis=1)
  next_e = jnp.where(n_upto < n_used[0], blk_e[jnp.minimum(n_upto, n_blocks - 1)], -1)
  first = jnp.concatenate([jnp.ones((1,), bool), blk_e[1:] != blk_e[:-1]])
  slot = (jnp.cumsum(first.astype(I32)) - 1) % 2
  grid_spec = pltpu.PrefetchScalarGridSpec(
      num_scalar_prefetch=4,
      grid=(n_blocks,),
      in_specs=[pl.BlockSpec((M_BLK, d), lambda i, be, ne, sl, nu: (jnp.minimum(i, nu[0] - 1), 0)),
                pl.BlockSpec(memory_space=pl.ANY),
                pl.BlockSpec(memory_space=pl.ANY)],
      out_specs=pl.BlockSpec((M_BLK, d), lambda i, be, ne, sl, nu: (i, 0)),
      scratch_shapes=[pltpu.VMEM((2, d, d_hidden2), F32),
                      pltpu.VMEM((2, d_hidden2 // 2, d), F32),
                      pltpu.VMEM((d, d_hidden2), BF16),
                      pltpu.VMEM((d_hidden2 // 2, d), BF16),
                      pltpu.SemaphoreType.DMA((2,)),
                      pltpu.SemaphoreType.DMA((2,))],
  )
  return pl.pallas_call(
      functools.partial(_expert_kernel, layer=layer),
      grid_spec=grid_spec,
      out_shape=jax.ShapeDtypeStruct((n_rows, d), F32),
      compiler_params=_params(("arbitrary",)),
      name="experts",
  )(blk_e, next_e.astype(I32), slot.astype(I32), n_used, buf, w_e_in, w_e_out)


def _combine_kernel(dest_ref, x_ref, wrow_ref, g2_ref, gain_ref, yb_ref,
                    o_ref, y_scr, sem, final_norm):
  tn, d = x_ref.shape[1], x_ref.shape[2]
  n_groups = y_scr.shape[1]

  def copy(row, slot, group, sub):
    return pltpu.make_async_copy(_table_row(yb_ref, row), _token_row(y_scr.at[slot], group, sub),
                                 sem)

  _for_each_assignment(n_groups, lambda j, u, slot, idx: copy(dest_ref[idx], slot, j, u).start())
  _for_each_assignment(n_groups, lambda j, u, slot, idx: copy(0, slot, 0, 0).wait())

  wrow = wrow_ref[0]
  moe = (wrow[:, 0:1] * y_scr[0].reshape(tn, d) + wrow[:, 1:2] * y_scr[1].reshape(tn, d))
  x = x_ref[0] + g2_ref[0] * moe
  if final_norm:
    x = (x * lax.rsqrt(jnp.mean(x * x, axis=-1, keepdims=True) + NORM_EPS)) * gain_ref[...]
  o_ref[0] = x


def _combine(dest_flat, x, wrow, g2, final_gain, yb, final_norm):
  bsz, s, d = x.shape
  tn = min(COMBINE_ROWS, s)
  n_tiles = s // tn
  row = pl.BlockSpec((1, tn, d), lambda b, i: (b, i, 0))
  return pl.pallas_call(
      functools.partial(_combine_kernel, final_norm=final_norm),
      grid=(bsz, n_tiles),
      in_specs=[pl.BlockSpec((TOP_K * tn,), lambda b, i: (b * n_tiles + i,),
                             memory_space=pltpu.SMEM),
                row,
                pl.BlockSpec((1, tn, LANES), lambda b, i: (b, i, 0)),
                pl.BlockSpec((1, 1, d), lambda b, i: (b, 0, 0)),
                pl.BlockSpec((1, d), lambda b, i: (0, 0)),
                pl.BlockSpec(memory_space=pl.ANY)],
      out_specs=row,
      scratch_shapes=[pltpu.VMEM((TOP_K, tn // SUBLANES, SUBLANES, d), F32),
                      pltpu.SemaphoreType.DMA(())],
      out_shape=jax.ShapeDtypeStruct((bsz, s, d), F32),
      compiler_params=_params(("arbitrary", "arbitrary")),
      name="combine",
  )(dest_flat, x, wrow, g2, final_gain.reshape(1, d), yb)


def _pad_rows(w, offset, total):
  return jnp.pad(w.astype(BF16), ((offset, total - offset - w.shape[0]), (0, 0)))


def kernel(x, c, w_ada, b_ada, norm1_gain, norm2_gain, w_in, token_mu, conv_w, w0, decay_up,
           a0, aaa_up, gate_up, k_k, k_a, r_k, ln_x_w, ln_x_b, vmix_down, vmix_up, v0, w_out,
           w_router, router_bias, w_expert_in, w_expert_out, final_gain):
  bsz, s, d = x.shape
  depth = w_in.shape[0]
  n_tok = bsz * s
  n_rows = n_tok * TOP_K + N_EXPERTS * M_BLK

  ada = _ada(c, w_ada, b_ada)
  mods = ada.reshape(depth, bsz, 6, 1, d)

  lane = jnp.arange(LANES)
  bd_ones = (lane[:, None] // HEAD_DIM == lane[None, :] // HEAD_DIM).astype(BF16)
  t = jnp.arange(WKV_CHUNK)
  tri = (t[:, None] >= t[None, :]).astype(BF16)
  tc = jnp.arange(min(RANK_COLS, s))
  triu = (tc[:, None] < tc[None, :]).astype(BF16)

  perm = (jnp.arange(N_EXPERTS) % N_GROUPS) * EXPERTS_PER_GROUP + jnp.arange(N_EXPERTS) // N_GROUPS
  w_router_t = w_router.T[perm].astype(BF16)
  router_bias_col = router_bias[perm].reshape(N_EXPERTS, 1)

  p_first_layer = None
  for l in range(depth):
    sh1, sc1, g1, sh2, sc2, g2 = (mods[l, :, j] for j in range(6))
    has_vmix = l > 0
    vmix_down_pad = None
    if has_vmix:
      vmix_down_pad = jnp.zeros((d, LANES), BF16).at[:, :VMIX_RANK].set(
          vmix_down[l - 1].astype(BF16))
    p, hv = _inproj(x, norm1_gain[l], sc1, sh1, w_in[l].astype(BF16), vmix_down_pad)

    n_shift = token_mu.shape[1]
    mu3 = token_mu[l, :3 * d].reshape(3, d)
    mu_lr = token_mu[l, 3 * d:].reshape(1, n_shift - 3 * d)
    vecs = jnp.stack([w0[l], a0[l], k_k[l], k_a[l]])
    if not has_vmix:
      p_first_layer = p
    merged = _mixer(
        p, p_first_layer, hv, mu3, mu_lr, conv_w[l], vecs,
        _pad_rows(decay_up[l], 0, LOW_RANK),
        _pad_rows(aaa_up[l], DECAY_RANK, LOW_RANK),
        _pad_rows(gate_up[l], DECAY_RANK + AAA_RANK, LOW_RANK),
        bd_ones,
        _pad_rows(vmix_up[l - 1], 0, LANES) if has_vmix else None,
        v0[l - 1].reshape(1, d) if has_vmix else None,
        token_mu[0, 2 * d:3 * d].reshape(1, d) if has_vmix else None,
        tri, ln_x_w[l], ln_x_b[l], r_k[l])
    x, h2, eid, wrow = _outproj(merged, x, w_out[l].astype(BF16), g1, norm2_gain[l], sc2, sh2,
                                w_router_t, router_bias_col)

    rank, sizes = _rank(eid, triu)
    sizes = sizes[:, 0]
    padded = (sizes + M_BLK - 1) // M_BLK * M_BLK
    pad_end = jnp.cumsum(padded)
    pad_start = (pad_end - padded).astype(I32)
    blk_first_row = jnp.arange(n_rows // M_BLK, dtype=I32) * M_BLK
    blk_e = jnp.minimum(
        jnp.sum((pad_end[None, :] <= blk_first_row[:, None]).astype(I32), axis=1),
        N_EXPERTS - 1)

    dest_flat = _dest(eid, rank, pad_start)
    buf = _scatter(pad_end.astype(I32), padded.astype(I32), dest_flat, h2, n_rows)
    n_used = (pad_end[-1:] // M_BLK).astype(I32)
    yb = _experts(blk_e, n_used, buf, w_expert_in, w_expert_out, l)
    x = _combine(dest_flat, x, wrow, g2, final_gain, yb, final_norm=(l == depth - 1))
  return x
```

```python
import functools
import itertools
import math

import jax
import jax.numpy as jnp
from jax import lax
from jax.experimental import pallas as pl
from jax.experimental.pallas import tpu as pltpu

F32 = jnp.float32
BF16 = jnp.bfloat16
I32 = jnp.int32

HEAD_DIM = 64
CONV_WIDTH = 3
DECAY_RANK = 64
AAA_RANK = 64
GATE_RANK = 128
LOW_RANK = DECAY_RANK + AAA_RANK + GATE_RANK
VMIX_RANK = 32
N_EXPERTS = 32
N_GROUPS = 8
EXPERTS_PER_GROUP = N_EXPERTS // N_GROUPS
TOP_K = 2
M_BLK = 512
NORM_EPS = 1e-6
LN_X_EPS = 64e-5
LOG2_E = 1.4426950408889634

LANES = 128
SUBLANES = 8
HEADS_PER_VREG = LANES // HEAD_DIM
VMEM_LIMIT_BYTES = 56 * 1024 * 1024

WKV_CHUNK = 64
WKV_STAGGER = 2
PREPARE_DELAY = 8
MIXER_TILE = 128
INPROJ_ROWS = 1024
INPROJ_COLS = 1408
OUTPROJ_ROWS = 512
RANK_COLS = 512
SCATTER_ROWS = 1024
COMBINE_ROWS = 1024


def _dot_bf16(a, b, dims=(((1,), (0,)), ((), ()))):
  return lax.dot_general(a.astype(BF16), b.astype(BF16), dims,
                         preferred_element_type=F32)


_NT = (((1,), (1,)), ((), ()))
_TN = (((0,), (0,)), ((), ()))


def _sigmoid(x):
  return 1.0 / (1.0 + jnp.exp(-x))


def _params(semantics):
  return pltpu.CompilerParams(dimension_semantics=semantics,
                              vmem_limit_bytes=VMEM_LIMIT_BYTES)


def _ada_kernel(c_ref, w_ref, b_ref, o_ref):
  c = c_ref[...]
  cond = c * _sigmoid(c)
  o_ref[0] = _dot_bf16(cond, w_ref[0]) + b_ref[0]


def _ada(c, w_ada, b_ada):
  depth, d, n = w_ada.shape
  bsz = c.shape[0]
  tn = d
  return pl.pallas_call(
      _ada_kernel,
      grid=(depth, n // tn),
      in_specs=[
          pl.BlockSpec((bsz, d), lambda l, j: (0, 0)),
          pl.BlockSpec((1, d, tn), lambda l, j: (l, 0, j)),
          pl.BlockSpec((1, 1, tn), lambda l, j: (l, 0, j)),
      ],
      out_specs=pl.BlockSpec((1, bsz, tn), lambda l, j: (l, 0, j)),
      out_shape=jax.ShapeDtypeStruct((depth, bsz, n), F32),
      compiler_params=_params(("arbitrary", "arbitrary")),
      name="ada",
  )(c, w_ada, b_ada.reshape(depth, 1, n))


def _modulated_norm(x, gain, scale, shift):
  y = x * lax.rsqrt(jnp.mean(x * x, axis=-1, keepdims=True) + NORM_EPS)
  return (y * gain) * (1.0 + scale) + shift


def _inproj_kernel(x_ref, gain_ref, sc_ref, sh_ref, w_ref, *rest, has_vmix):
  if has_vmix:
    wv_ref, p_ref, hv_ref, h_scr = rest
  else:
    p_ref, h_scr = rest

  @pl.when(pl.program_id(2) == 0)
  def _():
    h = _modulated_norm(x_ref[0], gain_ref[...], sc_ref[0], sh_ref[0])
    hb = h.astype(BF16)
    h_scr[...] = hb
    if has_vmix:
      hv_ref[0] = jnp.dot(hb, wv_ref[...], preferred_element_type=F32)

  p_ref[0] = jnp.dot(h_scr[...], w_ref[...], preferred_element_type=F32)


def _inproj(x, gain, sc, sh, w_bf16, vmix_down_bf16):
  bsz, s, d = x.shape
  n = w_bf16.shape[1]
  tm = min(INPROJ_ROWS, s)
  tn = INPROJ_COLS
  has_vmix = vmix_down_bf16 is not None
  in_specs = [
      pl.BlockSpec((1, tm, d), lambda b, i, j: (b, i, 0)),
      pl.BlockSpec((1, d), lambda b, i, j: (0, 0)),
      pl.BlockSpec((1, 1, d), lambda b, i, j: (b, 0, 0)),
      pl.BlockSpec((1, 1, d), lambda b, i, j: (b, 0, 0)),
      pl.BlockSpec((d, tn), lambda b, i, j: (0, j)),
  ]
  args = [x, gain.reshape(1, d), sc, sh, w_bf16]
  out_specs = [pl.BlockSpec((1, tm, tn), lambda b, i, j: (b, i, j))]
  out_shape = [jax.ShapeDtypeStruct((bsz, s, n), F32)]
  if has_vmix:
    in_specs.append(pl.BlockSpec((d, LANES), lambda b, i, j: (0, 0)))
    args.append(vmix_down_bf16)
    out_specs.append(pl.BlockSpec((1, tm, LANES), lambda b, i, j: (b, i, 0)))
    out_shape.append(jax.ShapeDtypeStruct((bsz, s, LANES), F32))
  outs = pl.pallas_call(
      functools.partial(_inproj_kernel, has_vmix=has_vmix),
      grid=(bsz, s // tm, n // tn),
      in_specs=in_specs,
      out_specs=out_specs,
      out_shape=out_shape,
      scratch_shapes=[pltpu.VMEM((tm, d), BF16)],
      compiler_params=_params(("arbitrary", "arbitrary", "arbitrary")),
      name="inproj",
  )(*args)
  return outs if has_vmix else (outs[0], None)


def _split_bf16(x, terms):
  parts = []
  for _ in range(terms - 1):
    part = x.astype(BF16)
    parts.append(part)
    x = x - part.astype(F32)
  parts.append(x.astype(BF16))
  return parts


def _head_sum(x, bd_ones):
  out = [jnp.dot(x[:, j * LANES:(j + 1) * LANES].astype(BF16), bd_ones,
                 preferred_element_type=F32) for j in range(x.shape[1] // LANES)]
  return out[0] if len(out) == 1 else jnp.concatenate(out, axis=1)


def _shift_rows(cur, prev_tail, n):
  rolled = pltpu.roll(cur, n, axis=0)
  head = rolled[:SUBLANES]
  row = lax.broadcasted_iota(I32, head.shape, 0)
  for j in range(n):
    head = jnp.where(row == j, prev_tail[SUBLANES - n + j:SUBLANES - n + j + 1, :], head)
  return jnp.concatenate([head, rolled[SUBLANES:]], axis=0)


_R, _K, _V, _LW, _KK, _BETA, _G, _MCONV, _SGB = range(9)
_N_PREPARED = 9


def _mixer_kernel(*refs, has_vmix):
  assert WKV_CHUNK == HEAD_DIM and HEADS_PER_VREG * WKV_CHUNK == LANES
  refs = list(refs)
  p_tiles = refs[:3]
  refs = refs[3:]
  if has_vmix:
    hv_tiles, pv_tiles = refs[:3], refs[3:6]
    refs = refs[6:]
  else:
    hv_tiles = pv_tiles = (None,) * 3
  mu_ref, mulr_ref, convw_ref, vec_ref, dup_ref, aup_ref, gup_ref, bd_ref = refs[:8]
  refs = refs[8:]
  if has_vmix:
    vup_ref, v0_ref, muvf_ref = refs[:3]
    refs = refs[3:]
  tri_ref, lnw_ref, lnb_ref, rk_ref, o_ref = refs[:5]
  buf_a, buf_b, pr_scr, pk_scr, pv_scr, plr_scr, pu_scr, pvf_scr, s_scr = refs[5:]

  c = WKV_CHUNK
  d = o_ref.shape[2]
  tile = buf_a.shape[1]
  n_pairs = d // LANES
  step = pl.program_id(1)
  bd_ones = bd_ref[...]
  tri = tri_ref[...]

  tail = slice(tile - SUBLANES, tile)

  def token_shift(cur, scr, lanes, mu):
    prev = _shift_rows(cur, scr[:, lanes], 1)
    scr[:, lanes] = cur[tail, :]
    return cur + (prev - cur) * mu

  def prepare(p_ref, hv_ref, pv_ref, dst):
    lr = token_shift(p_ref[0, :, 8 * d:8 * d + LOW_RANK], plr_scr, slice(None), mulr_ref[...])
    tanh_lr = jnp.tanh(lr).astype(BF16)
    sig_lr = _sigmoid(lr).astype(BF16)
    lr = lr.astype(BF16)
    if has_vmix:
      hv = hv_ref[0].astype(BF16)
    for _ in range(PREPARE_DELAY):
      yield

    for j in range(n_pairs):
      ln = slice(j * LANES, (j + 1) * LANES)
      col = lambda g: p_ref[0, :, g * d + j * LANES:g * d + (j + 1) * LANES]
      w0, a0, k_k, k_a = (vec_ref[i:i + 1, ln] for i in range(4))

      w_pre = w0 + jnp.dot(tanh_lr, dup_ref[:, ln], preferred_element_type=F32)
      dst[_LW, :, ln] = (-LOG2_E * math.exp(-0.5)) * _sigmoid(w_pre)
      a = _sigmoid(a0 + jnp.dot(lr, aup_ref[:, ln], preferred_element_type=F32))
      dst[_G, :, ln] = jnp.dot(sig_lr, gup_ref[:, ln], preferred_element_type=F32)

      dst[_R, :, ln] = token_shift(col(5), pr_scr, ln, mu_ref[0:1, ln])
      k = token_shift(col(6), pk_scr, ln, mu_ref[1:2, ln])
      kk = k * k_k
      kk = kk / jnp.maximum(jnp.sqrt(_head_sum(kk * kk, bd_ones)), 1e-12)
      dst[_K, :, ln] = k * (1.0 + (a - 1.0) * k_a)
      dst[_KK, :, ln] = kk
      dst[_BETA, :, ln] = a * kk

      v = token_shift(col(7), pv_scr, ln, mu_ref[2:3, ln])
      if has_vmix:
        v_first = token_shift(pv_ref[0, :, ln], pvf_scr, ln, muvf_ref[:, ln])
        mix = _sigmoid(v0_ref[:, ln]
                       + jnp.dot(hv, vup_ref[:, ln], preferred_element_type=F32))
        v = v + (v_first - v) * mix
      dst[_V, :, ln] = v

      u = col(1) * col(2)
      prev_u = pu_scr[:, ln]
      conv = (convw_ref[0:1, ln] * _shift_rows(u, prev_u, 2)
              + convw_ref[1:2, ln] * _shift_rows(u, prev_u, 1)
              + convw_ref[2:3, ln] * u)
      pu_scr[:, ln] = u[tail, :]
      dst[_MCONV, :, ln] = _sigmoid(col(3)) * (col(0) * conv)
      dst[_SGB, :, ln] = _sigmoid(col(4))
      yield

  lane = lax.broadcasted_iota(I32, (c, LANES), 1)
  head_masks = [(lane // HEAD_DIM == h).astype(BF16) for h in range(HEADS_PER_VREG)]
  t_row = lax.broadcasted_iota(I32, (c, LANES), 0)
  t_col = lane % c
  strict = t_row > t_col
  incl = t_row >= t_col
  eye = (t_row == t_col).astype(F32)
  same_head = (lax.broadcasted_iota(I32, (LANES, LANES), 0) // HEAD_DIM
               == lax.broadcasted_iota(I32, (LANES, LANES), 1) // HEAD_DIM)
  doubling_steps = c.bit_length() - 2

  def stack(xb):
    return jnp.concatenate([xb * m for m in head_masks], axis=0)

  def mm(a, b, dims=(((1,), (0,)), ((), ()))):
    return lax.dot_general(a, b, dims, preferred_element_type=F32)

  def cumulative_log_decay(src, rows):
    lw_terms = jnp.concatenate(_split_bf16(src[_LW, rows, :], 3), axis=1)
    cs = jnp.dot(tri, lw_terms, preferred_element_type=F32)
    return cs[:, :d] + cs[:, d:2 * d] + cs[:, 2 * d:]

  def pair_chunk(p, src, rows, out_rows, cum, delay):
    for _ in range(delay):
      yield
    ln = slice(p * LANES, (p + 1) * LANES)
    r = src[_R, rows, ln]
    k = src[_K, rows, ln]
    v = src[_V, rows, ln]
    lw = src[_LW, rows, ln]
    kk = src[_KK, rows, ln]
    beta = src[_BETA, rows, ln]

    cum_last = cum[c - 1:c, :]
    inv_p = jnp.exp2(-cum)
    rem_p = jnp.exp2(cum_last - cum)
    rt = (r * jnp.exp2(cum)).astype(BF16)
    at = (kk * jnp.exp2(cum - lw)).astype(BF16)
    kh_s = stack((k * inv_p).astype(BF16))
    bh_s = stack((beta * inv_p).astype(BF16))
    kb = (k * rem_p).astype(BF16)
    bb_neg = (-(beta * rem_p)).astype(BF16)
    vb = v.astype(BF16)
    v_s = stack(vb)
    yield

    sc = mm(jnp.concatenate([at, rt], axis=0), jnp.concatenate([kh_s, bh_s], axis=0), _NT)
    m_k = jnp.where(strict, sc[:c, :LANES], 0.0).astype(BF16)
    q_pow = jnp.where(strict, -sc[:c, LANES:], 0.0)
    q_k = jnp.where(incl, sc[c:, :LANES], 0.0).astype(BF16)
    q_b = jnp.where(incl, sc[c:, LANES:], 0.0).astype(BF16)
    yield

    w_inv = eye + q_pow
    q_pow = q_pow.astype(BF16)
    kv = mm(jnp.concatenate([m_k, q_k], axis=0), v_s)
    q_pow = mm(q_pow, stack(q_pow)).astype(BF16)
    yield
    for i in range(doubling_steps):
      w_s = stack(w_inv.astype(BF16))
      if i < doubling_steps - 1:
        prod = mm(q_pow, jnp.concatenate([w_s, stack(q_pow)], axis=1))
        q_pow = prod[:, LANES:].astype(BF16)
      else:
        prod = mm(q_pow, w_s)
      w_inv = w_inv + prod[:, :LANES]
      yield

    wa = mm(w_inv.astype(BF16),
            jnp.concatenate([stack(at), stack(kv[:c].astype(BF16))], axis=1))
    a2 = wa[:, :LANES].astype(BF16)
    v2 = wa[:, LANES:]
    yield

    state = s_scr[p]
    res = mm(jnp.concatenate([a2, rt], axis=0), state.astype(BF16), _NT)
    u = res[:c] + v2
    ub = u.astype(BF16)
    yield
    y = res[c:] + kv[c:] - mm(q_b, stack(ub))
    update = mm(jnp.concatenate([vb, ub], axis=0), jnp.concatenate([kb, bb_neg], axis=0), _TN)
    s_scr[p] = state * jnp.exp2(cum_last) + jnp.where(same_head, update, 0.0)
    yield

    inv_n = 1.0 / HEAD_DIM
    sums = _head_sum(jnp.concatenate([y, r * k * rk_ref[:, ln]], axis=0), bd_ones)
    yield
    yc = y - sums[:c] * inv_n
    var = _head_sum(yc * yc, bd_ones) * inv_n
    yield
    yn = yc * lax.rsqrt(var + LN_X_EPS) * lnw_ref[:, ln] + lnb_ref[:, ln]
    y_rwkv = (yn + sums[c:] * v) * src[_G, rows, ln]
    o_ref[0, out_rows, ln] = src[_MCONV, rows, ln] + src[_SGB, rows, ln] * y_rwkv

  def run_round_robin(chains):
    for _ in itertools.zip_longest(*chains):
      pass

  def slot(src, out_row0, p_ref, hv_ref, pv_ref, dst):
    chains = [prepare(p_ref, hv_ref, pv_ref, dst)]
    for j in range(tile // c):
      rows = slice(j * c, (j + 1) * c)
      out_rows = slice(out_row0 + j * c, out_row0 + (j + 1) * c)
      cum = cumulative_log_decay(src, rows)
      chains += [pair_chunk(p, src, rows, out_rows, cum[:, p * LANES:(p + 1) * LANES],
                            j * WKV_STAGGER) for p in range(n_pairs)]
    run_round_robin(chains)

  @pl.when(step == 0)
  def _():
    for scr in (pr_scr, pk_scr, pv_scr, plr_scr, pu_scr, pvf_scr, s_scr):
      scr[...] = jnp.zeros_like(scr)
    run_round_robin([prepare(p_tiles[0], hv_tiles[0], pv_tiles[0], buf_a)])

  slot(buf_a, 0, p_tiles[1], hv_tiles[1], pv_tiles[1], buf_b)
  slot(buf_b, tile, p_tiles[2], hv_tiles[2], pv_tiles[2], buf_a)


def _mixer(p, p_first_layer, hv, mu3, mu_lr, conv_w, vecs, decay_up_pad, aaa_up_pad, gate_up_pad,
           bd_ones, vmix_up_pad, v0, mu_v_first, tri, ln_w, ln_b, r_k):
  bsz, s, n_in = p.shape
  d = conv_w.shape[1]
  tile = MIXER_TILE
  n_tiles = s // tile
  has_vmix = hv is not None
  tile_maps = [lambda b, i: (b, 0, 0),
               lambda b, i: (b, 2 * i + 1, 0),
               lambda b, i: (b, jnp.minimum(2 * i + 2, n_tiles - 1), 0)]
  full = lambda a: pl.BlockSpec(a.shape, lambda b, i: (0,) * a.ndim)
  vec = pl.BlockSpec((1, d), lambda b, i: (0, 0))
  in_specs = [pl.BlockSpec((1, tile, n_in), m) for m in tile_maps]
  args = [p] * 3
  if has_vmix:
    in_specs += [pl.BlockSpec((1, tile, LANES), m) for m in tile_maps]
    v_col = 7
    in_specs += [pl.BlockSpec((1, tile, d), lambda b, i, m=m: m(b, i)[:2] + (v_col,))
                 for m in tile_maps]
    args += [hv] * 3 + [p_first_layer] * 3
  consts = [mu3, mu_lr, conv_w, vecs, decay_up_pad, aaa_up_pad, gate_up_pad, bd_ones]
  if has_vmix:
    consts += [vmix_up_pad, v0, mu_v_first]
  consts += [tri]
  in_specs += [full(a) for a in consts] + [vec, vec, vec]
  args += consts + [ln_w.reshape(1, d), ln_b.reshape(1, d), r_k.reshape(1, d)]
  return pl.pallas_call(
      functools.partial(_mixer_kernel, has_vmix=has_vmix),
      grid=(bsz, n_tiles // 2),
      in_specs=in_specs,
      out_specs=pl.BlockSpec((1, 2 * tile, d), lambda b, i: (b, i, 0)),
      out_shape=jax.ShapeDtypeStruct((bsz, s, d), F32),
      scratch_shapes=[pltpu.VMEM((_N_PREPARED, tile, d), F32)] * 2
      + [pltpu.VMEM((SUBLANES, d), F32)] * 3
      + [pltpu.VMEM((SUBLANES, LOW_RANK), F32)]
      + [pltpu.VMEM((SUBLANES, d), F32)] * 2
      + [pltpu.VMEM((d // LANES, LANES, LANES), F32)],
      compiler_params=_params(("arbitrary", "arbitrary")),
      name="mixer",
  )(*args)


def _outproj_kernel(m_ref, x_ref, w_ref, g1_ref, gain_ref, sc_ref, sh_ref, wr_ref, rb_ref,
                    xo_ref, h2_ref, eid_ref, wrow_ref):
  mix = jnp.dot(m_ref[0].astype(BF16), w_ref[...], preferred_element_type=F32)
  x = x_ref[0] + g1_ref[0] * mix
  xo_ref[0] = x
  h2 = _modulated_norm(x, gain_ref[...], sc_ref[0], sh_ref[0])
  h2_ref[0] = h2

  logits = _dot_bf16(wr_ref[...], h2, _NT)
  tm = logits.shape[1]
  epg, ng = EXPERTS_PER_GROUP, N_GROUPS
  aff = [_sigmoid(logits[j * ng:(j + 1) * ng, :]) for j in range(epg)]
  sel = [aff[j] + rb_ref[j * ng:(j + 1) * ng, :] for j in range(epg)]
  hi01, lo01 = jnp.maximum(sel[0], sel[1]), jnp.minimum(sel[0], sel[1])
  hi23, lo23 = jnp.maximum(sel[2], sel[3]), jnp.minimum(sel[2], sel[3])
  gscore = jnp.maximum(hi01, hi23) + jnp.maximum(jnp.minimum(hi01, hi23),
                                                 jnp.maximum(lo01, lo23))
  gid = lax.broadcasted_iota(I32, (ng, tm), 0)
  gmax = jnp.max(gscore, axis=0, keepdims=True)
  gsel = jnp.min(jnp.where(gscore == gmax, gid, ng), axis=0, keepdims=True)
  in_g = gid == gsel
  cand = [jnp.sum(jnp.where(in_g, sel[j], 0.0), axis=0, keepdims=True) for j in range(epg)]
  affc = [jnp.sum(jnp.where(in_g, aff[j], 0.0), axis=0, keepdims=True) for j in range(epg)]

  def first_argmax(vals):
    best = vals[0]
    for val in vals[1:]:
      best = jnp.maximum(best, val)
    idx = jnp.full(best.shape, epg, I32)
    for j in reversed(range(epg)):
      idx = jnp.where(vals[j] == best, j, idx)
    return idx

  i1 = first_argmax(cand)
  i2 = first_argmax([jnp.where(i1 == j, -jnp.inf, cand[j]) for j in range(epg)])
  pick = lambda idx: sum(jnp.where(idx == j, affc[j], 0.0) for j in range(epg))
  a1, a2 = pick(i1), pick(i2)
  denom = a1 + a2
  eid_ref[0] = jnp.concatenate([gsel * epg + i1, gsel * epg + i2], axis=0)
  wpad = jnp.concatenate([a1 / denom, a2 / denom, jnp.zeros((LANES - TOP_K, tm), F32)], axis=0)
  wrow_ref[0] = wpad.T


def _outproj(merged, x, w_out_bf16, g1, gain2, sc2, sh2, w_router_t, router_bias_col):
  bsz, s, d = x.shape
  tm = min(OUTPROJ_ROWS, s)
  row = pl.BlockSpec((1, tm, d), lambda b, i: (b, i, 0))
  mod = pl.BlockSpec((1, 1, d), lambda b, i: (b, 0, 0))
  full = lambda a: pl.BlockSpec(a.shape, lambda b, i: (0,) * a.ndim)
  gain2 = gain2.reshape(1, d)
  return pl.pallas_call(
      _outproj_kernel,
      grid=(bsz, s // tm),
      in_specs=[row, row, full(w_out_bf16), mod, full(gain2), mod, mod,
                full(w_router_t), full(router_bias_col)],
      out_specs=[row, row,
                 pl.BlockSpec((1, TOP_K, tm), lambda b, i: (b, 0, i)),
                 pl.BlockSpec((1, tm, LANES), lambda b, i: (b, i, 0))],
      out_shape=[jax.ShapeDtypeStruct((bsz, s, d), F32),
                 jax.ShapeDtypeStruct((bsz, s, d), F32),
                 jax.ShapeDtypeStruct((bsz, TOP_K, s), I32),
                 jax.ShapeDtypeStruct((bsz, s, LANES), F32)],
      compiler_params=_params(("arbitrary", "arbitrary")),
      name="outproj_router",
  )(merged, x, w_out_bf16, g1, gain2, sc2, sh2, w_router_t, router_bias_col)


def _rank_kernel(eid_ref, triu_ref, rank_ref, size_ref, carry_scr):
  @pl.when((pl.program_id(0) == 0) & (pl.program_id(1) == 0))
  def _():
    carry_scr[...] = jnp.zeros_like(carry_scr)

  tn = eid_ref.shape[2]
  eid = eid_ref[0]
  expert = lax.broadcasted_iota(I32, (N_EXPERTS, tn), 0)
  hot = [(expert == eid[slot:slot + 1, :]).astype(F32) for slot in range(TOP_K)]
  both = hot[0] + hot[1]
  before = jnp.dot(both.astype(BF16), triu_ref[...], preferred_element_type=F32)
  carry = carry_scr[...]
  count = before + jnp.concatenate([carry] * (tn // LANES), axis=1)
  rank_ref[0] = jnp.concatenate(
      [jnp.sum(hot[slot] * count, axis=0, keepdims=True) for slot in range(TOP_K)],
      axis=0).astype(I32)
  total = carry + jnp.dot(both.astype(BF16), jnp.ones((tn, LANES), BF16),
                          preferred_element_type=F32)
  carry_scr[...] = total
  size_ref[...] = total.astype(I32)


def _rank(eid, triu_bf16):
  bsz, _, s = eid.shape
  tn = min(RANK_COLS, s)
  return pl.pallas_call(
      _rank_kernel,
      grid=(bsz, s // tn),
      in_specs=[pl.BlockSpec((1, TOP_K, tn), lambda b, i: (b, 0, i)),
                pl.BlockSpec((tn, tn), lambda b, i: (0, 0))],
      out_specs=[pl.BlockSpec((1, TOP_K, tn), lambda b, i: (b, 0, i)),
                 pl.BlockSpec((N_EXPERTS, LANES), lambda b, i: (0, 0))],
      out_shape=[jax.ShapeDtypeStruct((bsz, TOP_K, s), I32),
                 jax.ShapeDtypeStruct((N_EXPERTS, LANES), I32)],
      scratch_shapes=[pltpu.VMEM((N_EXPERTS, LANES), F32)],
      compiler_params=_params(("arbitrary", "arbitrary")),
      name="rank",
  )(eid, triu_bf16)


def _token_row(ref, group, sub):
  return ref.at[group, pl.ds(sub, 1), :]


def _table_row(ref, row):
  return ref.at[pl.ds(row, 1), :]


def _dest_kernel(eid_ref, rank_ref, start_ref, o_ref):
  tn = eid_ref.shape[2]
  expert = lax.broadcasted_iota(I32, (N_EXPERTS, tn), 0)
  rows = []
  for slot in range(TOP_K):
    hot = expert == eid_ref[0, slot:slot + 1, :]
    start = jnp.sum(jnp.where(hot, start_ref[...], 0.0), axis=0, keepdims=True)
    rows.append(start.astype(I32) + rank_ref[0, slot:slot + 1, :])
  o_ref[0] = jnp.concatenate(rows, axis=0)


def _dest(eid, rank, pad_start):
  bsz, _, s = eid.shape
  tn = min(RANK_COLS, s)
  blk = pl.BlockSpec((1, TOP_K, tn), lambda b, i: (b, 0, i))
  dest = pl.pallas_call(
      _dest_kernel,
      grid=(bsz, s // tn),
      in_specs=[blk, blk, pl.BlockSpec((N_EXPERTS, 1), lambda b, i: (0, 0))],
      out_specs=blk,
      out_shape=jax.ShapeDtypeStruct((bsz, TOP_K, s), I32),
      compiler_params=_params(("arbitrary", "arbitrary")),
      name="dest",
  )(eid, rank, pad_start.astype(F32).reshape(N_EXPERTS, 1))
  return jnp.transpose(dest, (0, 2, 1)).reshape(-1)


def _for_each_assignment(n_groups, fn):
  def body(j, carry):
    for u in range(SUBLANES):
      for slot in range(TOP_K):
        fn(j, u, slot, (j * SUBLANES + u) * TOP_K + slot)
    return carry

  lax.fori_loop(0, n_groups, body, 0)


def _scatter_kernel(pad_end_ref, padded_ref, dest_ref, h_ref, buf_ref, zero_scr, sem, zero_sem):
  src = h_ref.at[0]
  n_groups = src.shape[0]

  @pl.when((pl.program_id(0) == 0) & (pl.program_id(1) == 0))
  def _():
    zero_scr[...] = jnp.zeros_like(zero_scr)
    n_blocks = buf_ref.shape[0] // M_BLK
    n_used = pad_end_ref[N_EXPERTS - 1] // M_BLK

    def zero_block(first_row):
      first_row = pl.multiple_of(first_row, M_BLK)
      return pltpu.make_async_copy(zero_scr, buf_ref.at[pl.ds(first_row, M_BLK), :], zero_sem)

    def for_each_zero_block(act):
      def tail(e, carry):
        @pl.when(padded_ref[e] > 0)
        def _():
          act(zero_block(pad_end_ref[e] - M_BLK))
        return carry

      def unused(blk, carry):
        act(zero_block(blk * M_BLK))
        return carry

      lax.fori_loop(0, N_EXPERTS, tail, 0)
      lax.fori_loop(n_used, n_blocks, unused, 0)

    for_each_zero_block(lambda cp: cp.start())
    for_each_zero_block(lambda cp: cp.wait())

  def copy(group, sub, row):
    return pltpu.make_async_copy(_token_row(src, group, sub), _table_row(buf_ref, row), sem)

  _for_each_assignment(n_groups, lambda j, u, slot, idx: copy(j, u, dest_ref[idx]).start())
  _for_each_assignment(n_groups, lambda j, u, slot, idx: copy(0, 0, 0).wait())


def _scatter(pad_end, padded, dest_flat, h2, n_rows):
  bsz, s, d = h2.shape
  tn = min(SCATTER_ROWS, s)
  n_tiles = s // tn
  grid_spec = pltpu.PrefetchScalarGridSpec(
      num_scalar_prefetch=2,
      grid=(bsz, n_tiles),
      in_specs=[pl.BlockSpec((TOP_K * tn,), lambda b, i, pe, pd: (b * n_tiles + i,),
                             memory_space=pltpu.SMEM),
                pl.BlockSpec((1, tn // SUBLANES, SUBLANES, d), lambda b, i, pe, pd: (b, i, 0, 0))],
      out_specs=pl.BlockSpec(memory_space=pl.ANY),
      scratch_shapes=[pltpu.VMEM((M_BLK, d), F32), pltpu.SemaphoreType.DMA(()),
                      pltpu.SemaphoreType.DMA(())],
  )
  return pl.pallas_call(
      _scatter_kernel,
      grid_spec=grid_spec,
      out_shape=jax.ShapeDtypeStruct((n_rows, d), F32),
      compiler_params=_params(("arbitrary", "arbitrary")),
      name="scatter_rows",
  )(pad_end, padded, dest_flat, h2.reshape(bsz, s // SUBLANES, SUBLANES, d))


def _expert_kernel(blk_e_ref, next_e_ref, slot_ref, n_used_ref, x_ref, win_hbm, wout_hbm, o_ref,
                   win_buf, wout_buf, win_scr, wout_scr, win_sem, wout_sem, *, layer):
  i = pl.program_id(0)

  @pl.when(i >= n_used_ref[0])
  def _():
    o_ref[...] = jnp.zeros_like(o_ref)

  def weight_copies(expert, slot):
    return (pltpu.make_async_copy(win_hbm.at[layer, expert], win_buf.at[slot], win_sem.at[slot]),
            pltpu.make_async_copy(wout_hbm.at[layer, expert], wout_buf.at[slot],
                                  wout_sem.at[slot]))

  @pl.when(i < n_used_ref[0])
  def _():
    expert, slot = blk_e_ref[i], slot_ref[i]

    @pl.when(i == 0)
    def _():
      for cp in weight_copies(expert, slot):
        cp.start()

    @pl.when((i == 0) | (expert != blk_e_ref[jnp.maximum(i - 1, 0)]))
    def _():
      for cp in weight_copies(expert, slot):
        cp.wait()
      win_scr[...] = win_buf[slot].astype(BF16)
      wout_scr[...] = wout_buf[slot].astype(BF16)

      @pl.when(next_e_ref[i] >= 0)
      def _():
        for cp in weight_copies(next_e_ref[i], 1 - slot):
          cp.start()

    hidden = jnp.dot(x_ref[...].astype(BF16), win_scr[...], preferred_element_type=F32)
    half = hidden.shape[1] // 2
    gate, up = hidden[:, :half], hidden[:, half:]
    act = (gate * _sigmoid(gate)) * up
    o_ref[...] = jnp.dot(act.astype(BF16), wout_scr[...], preferred_element_type=F32)


def _experts(blk_e, n_used, buf, w_e_in, w_e_out, layer):
  n_rows, d = buf.shape
  d_hidden2 = w_e_in.shape[3]
  n_blocks = n_rows // M_BLK
  used = jnp.arange(n_blocks) < n_used[0]
  n_upto = jnp.sum((blk_e[None, :] <= blk_e[:, None]) & used[None, :], axis=1)
  next_e = jnp.where(n_upto < n_used[0], blk_e[jnp.minimum(n_upto, n_blocks - 1)], -1)
  first = jnp.concatenate([jnp.ones((1,), bool), blk_e[1:] != blk_e[:-1]])
  slot = (jnp.cumsum(first.astype(I32)) - 1) % 2
  grid_spec = pltpu.PrefetchScalarGridSpec(
      num_scalar_prefetch=4,
      grid=(n_blocks,),
      in_specs=[pl.BlockSpec((M_BLK, d), lambda i, be, ne, sl, nu: (jnp.minimum(i, nu[0] - 1), 0)),
                pl.BlockSpec(memory_space=pl.ANY),
                pl.BlockSpec(memory_space=pl.ANY)],
      out_specs=pl.BlockSpec((M_BLK, d), lambda i, be, ne, sl, nu: (i, 0)),
      scratch_shapes=[pltpu.VMEM((2, d, d_hidden2), F32),
                      pltpu.VMEM((2, d_hidden2 // 2, d), F32),
                      pltpu.VMEM((d, d_hidden2), BF16),
                      pltpu.VMEM((d_hidden2 // 2, d), BF16),
                      pltpu.SemaphoreType.DMA((2,)),
                      pltpu.SemaphoreType.DMA((2,))],
  )
  return pl.pallas_call(
      functools.partial(_expert_kernel, layer=layer),
      grid_spec=grid_spec,
      out_shape=jax.ShapeDtypeStruct((n_rows, d), F32),
      compiler_params=_params(("arbitrary",)),
      name="experts",
  )(blk_e, next_e.astype(I32), slot.astype(I32), n_used, buf, w_e_in, w_e_out)


def _combine_kernel(dest_ref, x_ref, wrow_ref, g2_ref, gain_ref, yb_ref,
                    o_ref, y_scr, sem, final_norm):
  tn, d = x_ref.shape[1], x_ref.shape[2]
  n_groups = y_scr.shape[1]

  def copy(row, slot, group, sub):
    return pltpu.make_async_copy(_table_row(yb_ref, row), _token_row(y_scr.at[slot], group, sub),
                                 sem)

  _for_each_assignment(n_groups, lambda j, u, slot, idx: copy(dest_ref[idx], slot, j, u).start())
  _for_each_assignment(n_groups, lambda j, u, slot, idx: copy(0, slot, 0, 0).wait())

  wrow = wrow_ref[0]
  moe = (wrow[:, 0:1] * y_scr[0].reshape(tn, d) + wrow[:, 1:2] * y_scr[1].reshape(tn, d))
  x = x_ref[0] + g2_ref[0] * moe
  if final_norm:
    x = (x * lax.rsqrt(jnp.mean(x * x, axis=-1, keepdims=True) + NORM_EPS)) * gain_ref[...]
  o_ref[0] = x


def _combine(dest_flat, x, wrow, g2, final_gain, yb, final_norm):
  bsz, s, d = x.shape
  tn = min(COMBINE_ROWS, s)
  n_tiles = s // tn
  row = pl.BlockSpec((1, tn, d), lambda b, i: (b, i, 0))
  return pl.pallas_call(
      functools.partial(_combine_kernel, final_norm=final_norm),
      grid=(bsz, n_tiles),
      in_specs=[pl.BlockSpec((TOP_K * tn,), lambda b, i: (b * n_tiles + i,),
                             memory_space=pltpu.SMEM),
                row,
                pl.BlockSpec((1, tn, LANES), lambda b, i: (b, i, 0)),
                pl.BlockSpec((1, 1, d), lambda b, i: (b, 0, 0)),
                pl.BlockSpec((1, d), lambda b, i: (0, 0)),
                pl.BlockSpec(memory_space=pl.ANY)],
      out_specs=row,
      scratch_shapes=[pltpu.VMEM((TOP_K, tn // SUBLANES, SUBLANES, d), F32),
                      pltpu.SemaphoreType.DMA(())],
      out_shape=jax.ShapeDtypeStruct((bsz, s, d), F32),
      compiler_params=_params(("arbitrary", "arbitrary")),
      name="combine",
  )(dest_flat, x, wrow, g2, final_gain.reshape(1, d), yb)


def _pad_rows(w, offset, total):
  return jnp.pad(w.astype(BF16), ((offset, total - offset - w.shape[0]), (0, 0)))


def kernel(x, c, w_ada, b_ada, norm1_gain, norm2_gain, w_in, token_mu, conv_w, w0, decay_up,
           a0, aaa_up, gate_up, k_k, k_a, r_k, ln_x_w, ln_x_b, vmix_down, vmix_up, v0, w_out,
           w_router, router_bias, w_expert_in, w_expert_out, final_gain):
  bsz, s, d = x.shape
  depth = w_in.shape[0]
  n_tok = bsz * s
  n_rows = n_tok * TOP_K + N_EXPERTS * M_BLK

  ada = _ada(c, w_ada, b_ada)
  mods = ada.reshape(depth, bsz, 6, 1, d)

  lane = jnp.arange(LANES)
  bd_ones = (lane[:, None] // HEAD_DIM == lane[None, :] // HEAD_DIM).astype(BF16)
  t = jnp.arange(WKV_CHUNK)
  tri = (t[:, None] >= t[None, :]).astype(BF16)
  tc = jnp.arange(min(RANK_COLS, s))
  triu = (tc[:, None] < tc[None, :]).astype(BF16)

  perm = (jnp.arange(N_EXPERTS) % N_GROUPS) * EXPERTS_PER_GROUP + jnp.arange(N_EXPERTS) // N_GROUPS
  w_router_t = w_router.T[perm].astype(BF16)
  router_bias_col = router_bias[perm].reshape(N_EXPERTS, 1)

  p_first_layer = None
  for l in range(depth):
    sh1, sc1, g1, sh2, sc2, g2 = (mods[l, :, j] for j in range(6))
    has_vmix = l > 0
    vmix_down_pad = None
    if has_vmix:
      vmix_down_pad = jnp.zeros((d, LANES), BF16).at[:, :VMIX_RANK].set(
          vmix_down[l - 1].astype(BF16))
    p, hv = _inproj(x, norm1_gain[l], sc1, sh1, w_in[l].astype(BF16), vmix_down_pad)

    n_shift = token_mu.shape[1]
    mu3 = token_mu[l, :3 * d].reshape(3, d)
    mu_lr = token_mu[l, 3 * d:].reshape(1, n_shift - 3 * d)
    vecs = jnp.stack([w0[l], a0[l], k_k[l], k_a[l]])
    if not has_vmix:
      p_first_layer = p
    merged = _mixer(
        p, p_first_layer, hv, mu3, mu_lr, conv_w[l], vecs,
        _pad_rows(decay_up[l], 0, LOW_RANK),
        _pad_rows(aaa_up[l], DECAY_RANK, LOW_RANK),
        _pad_rows(gate_up[l], DECAY_RANK + AAA_RANK, LOW_RANK),
        bd_ones,
        _pad_rows(vmix_up[l - 1], 0, LANES) if has_vmix else None,
        v0[l - 1].reshape(1, d) if has_vmix else None,
        token_mu[0, 2 * d:3 * d].reshape(1, d) if has_vmix else None,
        tri, ln_x_w[l], ln_x_b[l], r_k[l])
    x, h2, eid, wrow = _outproj(merged, x, w_out[l].astype(BF16), g1, norm2_gain[l], sc2, sh2,
                                w_router_t, router_bias_col)

    rank, sizes = _rank(eid, triu)
    sizes = sizes[:, 0]
    padded = (sizes + M_BLK - 1) // M_BLK * M_BLK
    pad_end = jnp.cumsum(padded)
    pad_start = (pad_end - padded).astype(I32)
    blk_first_row = jnp.arange(n_rows // M_BLK, dtype=I32) * M_BLK
    blk_e = jnp.minimum(
        jnp.sum((pad_end[None, :] <= blk_first_row[:, None]).astype(I32), axis=1),
        N_EXPERTS - 1)

    dest_flat = _dest(eid, rank, pad_start)
    buf = _scatter(pad_end.astype(I32), padded.astype(I32), dest_flat, h2, n_rows)
    n_used = (pad_end[-1:] // M_BLK).astype(I32)
    yb = _experts(blk_e, n_used, buf, w_expert_in, w_expert_out, l)
    x = _combine(dest_flat, x, wrow, g2, final_gain, yb, final_norm=(l == depth - 1))
  return x
```
